```python
import math
import jax, jax.numpy as jnp
from jax import lax
import numpy as np

D_MODEL = 2048
BATCH = 1
SEQ = 16384
DEPTH = 2

SSM_WIDTH = D_MODEL // 2
SSM_GROUP = 16
SSM_GROUPS = SSM_WIDTH // SSM_GROUP
SSM_STATE = 64
HEAD_DIM = 64
ATT_HEADS = D_MODEL // (2 * HEAD_DIM)
ATT_KV_HEADS = ATT_HEADS // 4
ATT_GROUP = ATT_HEADS // ATT_KV_HEADS
ATT_Q_WIDTH = ATT_HEADS * HEAD_DIM
ATT_KV_WIDTH = ATT_KV_HEADS * HEAD_DIM
WINDOW = 128
ATT_BLOCK = 128
ROPE_THETA = 10000.0
GLA_HEADS = 4
GLA_V_WIDTH = D_MODEL // 2
GLA_DV = GLA_V_WIDTH // GLA_HEADS
GLA_DK = GLA_DV // 2
GLA_QK_WIDTH = GLA_HEADS * GLA_DK
GLA_GATE_RANK = 16
GLA_TAU = 16.0
GLA_CHUNK = 64
D_FF = 4 * D_MODEL
EPS = 1e-6
IN_WIDTH = (SSM_WIDTH + ATT_Q_WIDTH + 2 * ATT_KV_WIDTH + 2 * GLA_QK_WIDTH + GLA_V_WIDTH
            + GLA_GATE_RANK + GLA_V_WIDTH + 3 * D_MODEL)

kernel_name = "hybrid_s5_swa_gla_gated_block"


def _split_points():
    sizes = [SSM_WIDTH, ATT_Q_WIDTH, ATT_KV_WIDTH, ATT_KV_WIDTH, GLA_QK_WIDTH, GLA_QK_WIDTH,
             GLA_V_WIDTH, GLA_GATE_RANK, GLA_V_WIDTH, 3 * D_MODEL]
    return [int(v) for v in np.cumsum(sizes)[:-1]]


def rms_norm(x, g):
    xf = x.astype(jnp.float32)
    y = xf * lax.rsqrt(jnp.mean(xf * xf, axis=-1, keepdims=True) + EPS)
    return (y * g.astype(jnp.float32)).astype(x.dtype)


def rope(x, pos):
    half = HEAD_DIM // 2
    inv = ROPE_THETA ** (-jnp.arange(half, dtype=jnp.float32) / half)
    ang = pos.astype(jnp.float32)[:, None] * inv[None, :]
    cos = jnp.cos(ang)[None, :, None, :]
    sin = jnp.sin(ang)[None, :, None, :]
    xf = x.astype(jnp.float32)
    x1, x2 = xf[..., :half], xf[..., half:]
    return jnp.concatenate([x1 * cos - x2 * sin, x2 * cos + x1 * sin], axis=-1).astype(x.dtype)


def _cplx_combine(e1, e2):
    a1r, a1i, b1r, b1i = e1
    a2r, a2i, b2r, b2i = e2
    return (a2r * a1r - a2i * a1i,
            a2r * a1i + a2i * a1r,
            a2r * b1r - a2i * b1i + b2r,
            a2r * b1i + a2i * b1r + b2i)


def s5_branch(u, lam_re, lam_im, log_step, b_re, b_im, c_re, c_im, d_skip, w_glu, b_glu):
    bsz, s_len, _ = u.shape
    uf = u.astype(jnp.float32)
    ug = uf.reshape(bsz, s_len, SSM_GROUPS, SSM_GROUP)
    step = jnp.exp(log_step.astype(jnp.float32))[:, None]
    lr = lam_re.astype(jnp.float32)
    li = lam_im.astype(jnp.float32)
    mag = jnp.exp(lr * step)
    ar = mag * jnp.cos(li * step)
    ai = mag * jnp.sin(li * step)
    den = lr * lr + li * li
    fr = ((ar - 1.0) * lr + ai * li) / den
    fi = (ai * lr - (ar - 1.0) * li) / den
    br = b_re.astype(jnp.float32)
    bi = b_im.astype(jnp.float32)
    bbr = fr[..., None] * br - fi[..., None] * bi
    bbi = fr[..., None] * bi + fi[..., None] * br
    xr = jnp.einsum('bsgc,gpc->sbgp', ug, bbr)
    xi = jnp.einsum('bsgc,gpc->sbgp', ug, bbi)
    a_r = jnp.broadcast_to(ar[None, None], xr.shape)
    a_i = jnp.broadcast_to(ai[None, None], xr.shape)
    _, _, hr, hi = lax.associative_scan(_cplx_combine, (a_r, a_i, xr, xi), axis=0)
    y = (jnp.einsum('sbgp,gcp->bsgc', hr, c_re.astype(jnp.float32))
         - jnp.einsum('sbgp,gcp->bsgc', hi, c_im.astype(jnp.float32)))
    y = y.reshape(bsz, s_len, SSM_WIDTH) + d_skip.astype(jnp.float32) * uf
    z = jax.nn.gelu(y)
    z = z * jax.nn.sigmoid(z @ w_glu.astype(jnp.float32) + b_glu.astype(jnp.float32))
    return z.astype(u.dtype)


def swa_branch(q, k, v, sinks, pos):
    bsz, s_len = q.shape[:2]
    nb = s_len // ATT_BLOCK
    q = rope(q, pos)
    k = rope(k, pos)
    qb = q.reshape(bsz, nb, ATT_BLOCK, ATT_KV_HEADS, ATT_GROUP, HEAD_DIM)
    pad = ((0, 0), (ATT_BLOCK, 0), (0, 0), (0, 0))
    kp = jnp.pad(k, pad).reshape(bsz, nb + 1, ATT_BLOCK, ATT_KV_HEADS, HEAD_DIM)
    vp = jnp.pad(v, pad).reshape(bsz, nb + 1, ATT_BLOCK, ATT_KV_HEADS, HEAD_DIM)
    kb = jnp.concatenate([kp[:, :-1], kp[:, 1:]], axis=2)
    vb = jnp.concatenate([vp[:, :-1], vp[:, 1:]], axis=2)
    scores = jnp.einsum('bnqhgd,bnkhd->bnhgqk', qb, kb).astype(jnp.float32) * (HEAD_DIM ** -0.5)
    qi = jnp.arange(ATT_BLOCK)[:, None]
    kj = jnp.arange(2 * ATT_BLOCK)[None, :]
    diff = qi + ATT_BLOCK - kj
    band = (diff >= 0) & (diff < WINDOW)
    key_pos = jnp.arange(nb)[:, None] * ATT_BLOCK - ATT_BLOCK + jnp.arange(2 * ATT_BLOCK)[None, :]
    valid = band[None] & (key_pos >= 0)[:, None, :]
    scores = jnp.where(valid[None, :, None, None], scores, -jnp.inf)
    sink = sinks.astype(jnp.float32).reshape(ATT_KV_HEADS, ATT_GROUP)[None, None, :, :, None, None]
    sink = jnp.broadcast_to(sink, scores.shape[:-1] + (1,))
    probs = jax.nn.softmax(jnp.concatenate([scores, sink], axis=-1), axis=-1)[..., :-1]
    o = jnp.einsum('bnhgqk,bnkhd->bnqhgd', probs.astype(v.dtype), vb)
    return o.reshape(bsz, s_len, ATT_Q_WIDTH)


def gla_branch(q, k, v, gate_lr, out_gate, w_gate_up, b_gate, norm_g):
    bsz, s_len = q.shape[:2]
    nc = s_len // GLA_CHUNK
    f32 = jnp.float32
    log_a = jax.nn.log_sigmoid(gate_lr.astype(f32) @ w_gate_up.astype(f32) + b_gate.astype(f32)) / GLA_TAU
    shp_k = (bsz, nc, GLA_CHUNK, GLA_HEADS, GLA_DK)
    shp_v = (bsz, nc, GLA_CHUNK, GLA_HEADS, GLA_DV)
    qc = q.astype(f32).reshape(shp_k) * (GLA_DK ** -0.5)
    kc = k.astype(f32).reshape(shp_k)
    vc = v.astype(f32).reshape(shp_v)
    bcum = jnp.cumsum(log_a.reshape(shp_k), axis=2)
    b_last = bcum[:, :, -1]
    q_in = qc * jnp.exp(bcum)
    k_in = kc * jnp.exp(-bcum)
    att = jnp.einsum('bnihd,bnjhd->bnhij', q_in, k_in)
    causal = jnp.tril(jnp.ones((GLA_CHUNK, GLA_CHUNK), dtype=bool))
    att = jnp.where(causal, att, 0.0)
    o_intra = jnp.einsum('bnhij,bnjhe->bnihe', att, vc)
    k_dec = kc * jnp.exp(b_last[:, :, None] - bcum)
    upd = jnp.einsum('bnjhd,bnjhe->bnhde', k_dec, vc)
    decay = jnp.exp(b_last)

    def chunk_step(state, inp):
        u_n, g_n = inp
        return g_n[..., None] * state + u_n, state

    init = jnp.zeros((bsz, GLA_HEADS, GLA_DK, GLA_DV), f32)
    _, starts = lax.scan(chunk_step, init, (jnp.moveaxis(upd, 1, 0), jnp.moveaxis(decay, 1, 0)))
    o_inter = jnp.einsum('bnihd,nbhde->bnihe', q_in, starts)
    o = (o_intra + o_inter).reshape(bsz, s_len, GLA_HEADS, GLA_DV)
    o = o * lax.rsqrt(jnp.mean(o * o, axis=-1, keepdims=True) + EPS)
    o = o.reshape(bsz, s_len, GLA_V_WIDTH) * norm_g.astype(f32)
    o = o * jax.nn.silu(out_gate.astype(f32))
    return o.astype(q.dtype)


def setup_inputs(seed: int = 0) -> dict:
    key = jax.random.key(seed)
    ks = jax.random.split(key, 32)
    L, D = DEPTH, D_MODEL
    G, P, C = SSM_GROUPS, SSM_STATE, SSM_GROUP

    def nrm(k, shape, scale):
        return jax.random.normal(k, shape, jnp.float32) * scale

    x = nrm(ks[0], (BATCH, SEQ, D), 1.0)
    norm1_g = 1.0 + nrm(ks[1], (L, D), 0.02)
    w_in = nrm(ks[2], (L, D, IN_WIDTH), D ** -0.5)
    ssm_lam_re = -0.5 + nrm(ks[3], (L, G, P), 0.01)
    ssm_lam_im = math.pi * jnp.arange(P, dtype=jnp.float32)[None, None, :] + nrm(ks[4], (L, G, P), 0.01)
    ssm_log_step = jax.random.uniform(ks[5], (L, G), jnp.float32, math.log(1e-3), math.log(1e-1))
    ssm_b_re = nrm(ks[6], (L, G, P, C), (2.0 * C) ** -0.5)
    ssm_b_im = nrm(ks[7], (L, G, P, C), (2.0 * C) ** -0.5)
    ssm_c_re = nrm(ks[8], (L, G, C, P), (2.0 * P) ** -0.5)
    ssm_c_im = nrm(ks[9], (L, G, C, P), (2.0 * P) ** -0.5)
    ssm_d = nrm(ks[10], (L, SSM_WIDTH), 1.0)
    ssm_w_glu = nrm(ks[11], (L, SSM_WIDTH, SSM_WIDTH), SSM_WIDTH ** -0.5)
    ssm_b_glu = nrm(ks[12], (L, SSM_WIDTH), 0.02)
    att_sinks = nrm(ks[13], (L, ATT_HEADS), 0.5)
    gla_w_gate = nrm(ks[14], (L, GLA_GATE_RANK, GLA_QK_WIDTH), GLA_GATE_RANK ** -0.5)
    gla_b_gate = nrm(ks[15], (L, GLA_QK_WIDTH), 0.1)
    gla_norm_g = 1.0 + nrm(ks[16], (L, GLA_V_WIDTH), 0.02)
    w_branch_ssm = nrm(ks[17], (L, SSM_WIDTH, D), SSM_WIDTH ** -0.5)
    w_branch_att = nrm(ks[18], (L, ATT_Q_WIDTH, D), ATT_Q_WIDTH ** -0.5)
    w_branch_gla = nrm(ks[19], (L, GLA_V_WIDTH, D), GLA_V_WIDTH ** -0.5)
    w_out = nrm(ks[20], (L, D, D), D ** -0.5)
    norm2_g = 1.0 + nrm(ks[21], (L, D), 0.02)
    w_ff1 = nrm(ks[22], (L, D, D_FF), D ** -0.5)
    w_ff2 = nrm(ks[23], (L, D_FF, D), D_FF ** -0.5)
    final_norm_g = 1.0 + nrm(ks[24], (D,), 0.02)
    return {"x": x, "norm1_g": norm1_g, "w_in": w_in,
            "ssm_lam_re": ssm_lam_re, "ssm_lam_im": ssm_lam_im, "ssm_log_step": ssm_log_step,
            "ssm_b_re": ssm_b_re, "ssm_b_im": ssm_b_im, "ssm_c_re": ssm_c_re, "ssm_c_im": ssm_c_im,
            "ssm_d": ssm_d, "ssm_w_glu": ssm_w_glu, "ssm_b_glu": ssm_b_glu,
            "att_sinks": att_sinks,
            "gla_w_gate": gla_w_gate, "gla_b_gate": gla_b_gate, "gla_norm_g": gla_norm_g,
            "w_branch_ssm": w_branch_ssm, "w_branch_att": w_branch_att, "w_branch_gla": w_branch_gla,
            "w_out": w_out, "norm2_g": norm2_g, "w_ff1": w_ff1, "w_ff2": w_ff2,
            "final_norm_g": final_norm_g}


def reference(x, norm1_g, w_in, ssm_lam_re, ssm_lam_im, ssm_log_step, ssm_b_re, ssm_b_im,
              ssm_c_re, ssm_c_im, ssm_d, ssm_w_glu, ssm_b_glu, att_sinks, gla_w_gate, gla_b_gate,
              gla_norm_g, w_branch_ssm, w_branch_att, w_branch_gla, w_out, norm2_g, w_ff1, w_ff2,
              final_norm_g):
    bsz, s_len, _ = x.shape
    pos = jnp.arange(s_len, dtype=jnp.int32)
    splits = _split_points()
    for l in range(DEPTH):
        h = rms_norm(x, norm1_g[l])
        proj = h @ w_in[l]
        (u_ssm, aq, ak, av, gq, gk, gv, g_lr, g_out, merge) = jnp.split(proj, splits, axis=-1)
        y_ssm = s5_branch(u_ssm, ssm_lam_re[l], ssm_lam_im[l], ssm_log_step[l], ssm_b_re[l],
                          ssm_b_im[l], ssm_c_re[l], ssm_c_im[l], ssm_d[l], ssm_w_glu[l], ssm_b_glu[l])
        y_att = swa_branch(aq.reshape(bsz, s_len, ATT_HEADS, HEAD_DIM),
                           ak.reshape(bsz, s_len, ATT_KV_HEADS, HEAD_DIM),
                           av.reshape(bsz, s_len, ATT_KV_HEADS, HEAD_DIM), att_sinks[l], pos)
        y_gla = gla_branch(gq, gk, gv, g_lr, g_out, gla_w_gate[l], gla_b_gate[l], gla_norm_g[l])
        gates = jax.nn.sigmoid(merge).reshape(bsz, s_len, 3, D_MODEL)
        mixed = (gates[:, :, 0] * (y_ssm @ w_branch_ssm[l])
                 + gates[:, :, 1] * (y_att @ w_branch_att[l])
                 + gates[:, :, 2] * (y_gla @ w_branch_gla[l]))
        x = x + mixed @ w_out[l]
        h2 = rms_norm(x, norm2_g[l])
        x = x + jnp.square(jax.nn.relu(h2 @ w_ff1[l])) @ w_ff2[l]
    return rms_norm(x, final_norm_g)
```

```python
import functools
import math

import jax
import jax.numpy as jnp
import numpy as np
from jax import lax
from jax.experimental import pallas as pl
from jax.experimental.pallas import tpu as pltpu

F32 = jnp.float32
BF16 = jnp.bfloat16

D_MODEL = 2048
SEQ = 16384
DEPTH = 2
SSM_WIDTH = 1024
SSM_GROUP = 16
SSM_GROUPS = 64
SSM_STATE = 64
HEAD_DIM = 64
ATT_HEADS = 16
ATT_KV_HEADS = 4
ATT_GROUP = 4
ATT_Q_WIDTH = 1024
ATT_KV_WIDTH = 256
WINDOW = 128
ATT_BLOCK = 128
ROPE_THETA = 10000.0
GLA_HEADS = 4
GLA_V_WIDTH = 1024
GLA_DV = 256
GLA_DK = 128
GLA_QK_WIDTH = 512
GLA_GATE_RANK = 16
GLA_TAU = 16.0
GLA_CHUNK = 64
D_FF = 8192
EPS = 1e-6

LANES = 128
VMEM_LIMIT = 56 * 1024 * 1024

COL_U = 0
COL_AQ = 1024
COL_GV = 2048
COL_GOUT = 3072
COL_MERGE = 4096
COL_GQ = 10240
COL_GK = 10752
COL_AK = 11264
COL_AV = 11520
COL_GLR = 11776
GLR_PAD = 512
PROJ_WIDTH = 12288

SSM_T = 16
SSM_NC = SEQ // SSM_T
SSM_TC = SSM_T * SSM_GROUP
SSM_PAIRS = SSM_GROUPS // 2
SSM_STATES = SSM_GROUPS * SSM_STATE
NEG_BIG = -1e30


def _cparams(sem, vmem=VMEM_LIMIT):
    return pltpu.CompilerParams(dimension_semantics=sem, vmem_limit_bytes=vmem)


def _rms_rows(x_ref, g_ref, dst_ref, rows, chunk=128):
    g = g_ref[...]

    def body(r, c):
        sl = pl.ds(pl.multiple_of(r * chunk, chunk), chunk)
        xv = x_ref[sl, :].astype(F32)
        ms = jnp.mean(xv * xv, axis=-1, keepdims=True)
        dst_ref[sl, :] = (xv * lax.rsqrt(ms + EPS) * g).astype(dst_ref.dtype)
        return c

    lax.fori_loop(0, rows // chunk, body, 0)


def _inproj_kernel(x_ref, g_ref, w_ref, o_ref, h_ref, *, tm):
    @pl.when(pl.program_id(1) == 0)
    def _():
        _rms_rows(x_ref, g_ref, h_ref, tm)

    o_ref[...] = jnp.dot(h_ref[...], w_ref[...], preferred_element_type=F32).astype(o_ref.dtype)


def _inproj(x, g, w, tm=1024, tn=1024):
    s, d = x.shape
    n = w.shape[1]
    return pl.pallas_call(
        functools.partial(_inproj_kernel, tm=tm),
        out_shape=jax.ShapeDtypeStruct((s, n), BF16),
        grid=(s // tm, n // tn),
        in_specs=[
            pl.BlockSpec((tm, d), lambda i, j: (i, 0)),
            pl.BlockSpec((1, d), lambda i, j: (0, 0)),
            pl.BlockSpec((d, tn), lambda i, j: (0, j)),
        ],
        out_specs=pl.BlockSpec((tm, tn), lambda i, j: (i, j)),
        scratch_shapes=[pltpu.VMEM((tm, d), BF16)],
        compiler_params=_cparams(("parallel", "arbitrary")),
        name="inproj",
    )(x, g, w)


def _s5_prepare(lam_re, lam_im, log_step, b_re, b_im, c_re, c_im):
    hp = lax.Precision.HIGHEST
    t_len = SSM_T
    g_n, p_n, c_n = SSM_GROUPS, SSM_STATE, SSM_GROUP
    step = jnp.exp(log_step.astype(F32))[:, None]
    lr = lam_re.astype(F32)
    li = lam_im.astype(F32)
    mag = jnp.exp(lr * step)
    ar = mag * jnp.cos(li * step)
    ai = mag * jnp.sin(li * step)
    den = lr * lr + li * li
    fr = ((ar - 1.0) * lr + ai * li) / den
    fi = (ai * lr - (ar - 1.0) * li) / den
    br = b_re.astype(F32)
    bi = b_im.astype(F32)
    bbr = fr[..., None] * br - fi[..., None] * bi
    bbi = fr[..., None] * bi + fi[..., None] * br
    jj = jnp.arange(t_len + 1, dtype=F32)[None, None, :]
    pmag = jnp.exp((lr * step)[..., None] * jj)
    pang = (li * step)[..., None] * jj
    pw_r = pmag * jnp.cos(pang)
    pw_i = pmag * jnp.sin(pang)
    cr = c_re.astype(F32)
    ci = c_im.astype(F32)
    pr_t = jnp.transpose(pw_r, (0, 2, 1))[:, :, None, :]
    pi_t = jnp.transpose(pw_i, (0, 2, 1))[:, :, None, :]
    ca_r = cr[:, None] * pr_t - ci[:, None] * pi_t
    ca_i = cr[:, None] * pi_t + ci[:, None] * pr_t
    kmat = (jnp.einsum('gjcp,gpd->gjcd', ca_r[:, :t_len], bbr, precision=hp)
            - jnp.einsum('gjcp,gpd->gjcd', ca_i[:, :t_len], bbi, precision=hp))
    s_idx = np.arange(t_len)[:, None]
    t_idx = np.arange(t_len)[None, :]
    lag = t_idx - s_idx
    kg = kmat[:, np.clip(lag, 0, t_len - 1)]
    kg = jnp.where(jnp.asarray(lag >= 0)[None, :, :, None, None], kg, 0.0)
    toep = jnp.transpose(kg, (0, 1, 4, 2, 3)).reshape(g_n, SSM_TC, SSM_TC)
    d_r = jnp.transpose(ca_r[:, 1:], (0, 3, 1, 2)).reshape(g_n, p_n, SSM_TC)
    d_i = jnp.transpose(ca_i[:, 1:], (0, 3, 1, 2)).reshape(g_n, p_n, SSM_TC)
    rev_r = pw_r[:, :, t_len - 1::-1][:, :, :t_len]
    rev_i = pw_i[:, :, t_len - 1::-1][:, :, :t_len]
    e_r = rev_r[..., None] * bbr[:, :, None, :] - rev_i[..., None] * bbi[:, :, None, :]
    e_i = rev_r[..., None] * bbi[:, :, None, :] + rev_i[..., None] * bbr[:, :, None, :]
    e_r = jnp.transpose(e_r, (0, 2, 3, 1)).reshape(g_n, SSM_TC, p_n)
    e_i = jnp.transpose(e_i, (0, 2, 3, 1)).reshape(g_n, SSM_TC, p_n)
    z_in = jnp.zeros((SSM_PAIRS, SSM_TC, p_n), F32)
    e_r2 = e_r.reshape(SSM_PAIRS, 2, SSM_TC, p_n)
    e_i2 = e_i.reshape(SSM_PAIRS, 2, SSM_TC, p_n)
    min_top = jnp.concatenate([e_r2[:, 0], z_in, e_i2[:, 0], z_in], axis=-1)
    min_bot = jnp.concatenate([z_in, e_r2[:, 1], z_in, e_i2[:, 1]], axis=-1)
    m_in = jnp.concatenate([min_top, min_bot], axis=1)
    z_out = jnp.zeros((SSM_PAIRS, p_n, SSM_TC), F32)
    d_r2 = d_r.reshape(SSM_PAIRS, 2, p_n, SSM_TC)
    d_i2 = d_i.reshape(SSM_PAIRS, 2, p_n, SSM_TC)
    m_out = jnp.concatenate([
        jnp.concatenate([d_r2[:, 0], z_out], axis=-1),
        jnp.concatenate([z_out, d_r2[:, 1]], axis=-1),
        jnp.concatenate([-d_i2[:, 0], z_out], axis=-1),
        jnp.concatenate([z_out, -d_i2[:, 1]], axis=-1)], axis=1)
    at_r = pw_r[:, :, t_len].reshape(1, SSM_STATES)
    at_i = pw_i[:, :, t_len].reshape(1, SSM_STATES)
    return toep.astype(BF16), m_in.astype(BF16), m_out.astype(BF16), at_r, at_i


def _s5_x_kernel(u_ref, min_ref, xr_ref, xi_ref):
    up = jnp.concatenate([u_ref[0], u_ref[1]], axis=1)
    x = jnp.dot(up, min_ref[0], preferred_element_type=F32)
    xr_ref[...] = x[:, :LANES]
    xi_ref[...] = x[:, LANES:]


def _s5_x(u_g, m_in):
    nc = SSM_NC
    return pl.pallas_call(
        _s5_x_kernel,
        out_shape=[jax.ShapeDtypeStruct((nc, SSM_STATES), F32)] * 2,
        grid=(SSM_PAIRS,),
        in_specs=[
            pl.BlockSpec((2, nc, SSM_TC), lambda p: (p, 0, 0)),
            pl.BlockSpec((1, 2 * SSM_TC, 256), lambda p: (p, 0, 0)),
        ],
        out_specs=[pl.BlockSpec((nc, LANES), lambda p: (0, p))] * 2,
        compiler_params=_cparams(("parallel",)),
        name="s5_x",
    )(u_g, m_in)


def _s5_scan_kernel(xr_ref, xi_ref, ar_ref, ai_ref, hr_ref, hi_ref, *, nc):
    ar = ar_ref[...]
    ai = ai_ref[...]

    def body(n, carry):
        hr, hi = carry
        row = pl.ds(n, 1)
        hr_ref[row, :] = hr
        hi_ref[row, :] = hi
        xr = xr_ref[row, :]
        xi = xi_ref[row, :]
        return (ar * hr - ai * hi + xr, ar * hi + ai * hr + xi)

    zero = jnp.zeros_like(ar)
    lax.fori_loop(0, nc, body, (zero, zero), unroll=8)


def _s5_scan(xr, xi, at_r, at_i, lb=1024):
    nc, ns = xr.shape
    blk = pl.BlockSpec((nc, lb), lambda i: (0, i))
    tab = pl.BlockSpec((1, lb), lambda i: (0, i))
    return pl.pallas_call(
        functools.partial(_s5_scan_kernel, nc=nc),
        out_shape=[jax.ShapeDtypeStruct((nc, ns), F32)] * 2,
        grid=(ns // lb,),
        in_specs=[blk, blk, tab, tab],
        out_specs=[blk, blk],
        compiler_params=_cparams(("parallel",)),
        name="s5_scan",
    )(xr, xi, at_r, at_i)


def _s5_y_kernel(u_ref, hr_ref, hi_ref, toep_ref, mout_ref, y_ref):
    hcat = jnp.concatenate([hr_ref[...], hi_ref[...]], axis=1).astype(BF16)
    carry = jnp.dot(hcat, mout_ref[0], preferred_element_type=F32)
    for k in range(2):
        y = jnp.dot(u_ref[k], toep_ref[k], preferred_element_type=F32)
        y_ref[k] = y + carry[:, k * SSM_TC:(k + 1) * SSM_TC]


def _s5_y(u_g, hr, hi, toep, m_out):
    nc = SSM_NC
    hblk = pl.BlockSpec((nc, LANES), lambda p: (0, p))
    return pl.pallas_call(
        _s5_y_kernel,
        out_shape=jax.ShapeDtypeStruct((SSM_GROUPS, nc, SSM_TC), F32),
        grid=(SSM_PAIRS,),
        in_specs=[
            pl.BlockSpec((2, nc, SSM_TC), lambda p: (p, 0, 0)),
            hblk, hblk,
            pl.BlockSpec((2, SSM_TC, SSM_TC), lambda p: (p, 0, 0)),
            pl.BlockSpec((1, 256, 2 * SSM_TC), lambda p: (p, 0, 0)),
        ],
        out_specs=pl.BlockSpec((2, nc, SSM_TC), lambda p: (p, 0, 0)),
        compiler_params=_cparams(("parallel",)),
        name="s5_y",
    )(u_g, hr, hi, toep, m_out)


def _s5_glu_kernel(y_ref, u_ref, d_ref, w_ref, b_ref, o_ref):
    y = y_ref[...] + d_ref[...] * u_ref[...].astype(F32)
    z = jax.nn.gelu(y)
    gate = jnp.dot(z.astype(BF16), w_ref[...], preferred_element_type=F32) + b_ref[...]
    o_ref[...] = (z * jax.nn.sigmoid(gate)).astype(o_ref.dtype)


def _s5_glu(y, proj, d_skip, w_glu, b_glu, tm=512):
    s = y.shape[0]
    w = SSM_WIDTH
    return pl.pallas_call(
        _s5_glu_kernel,
        out_shape=jax.ShapeDtypeStruct((s, w), BF16),
        grid=(s // tm,),
        in_specs=[
            pl.BlockSpec((tm, w), lambda i: (i, 0)),
            pl.BlockSpec((tm, w), lambda i: (i, COL_U // w)),
            pl.BlockSpec((1, w), lambda i: (0, 0)),
            pl.BlockSpec((w, w), lambda i: (0, 0)),
            pl.BlockSpec((1, w), lambda i: (0, 0)),
        ],
        out_specs=pl.BlockSpec((tm, w), lambda i: (i, 0)),
        compiler_params=_cparams(("parallel",)),
        name="s5_glu",
    )(y, proj, d_skip, w_glu, b_glu)


def _s5_branch(proj, prep, d_skip, w_glu, b_glu):
    toep, m_in, m_out, at_r, at_i = prep
    u = proj[:, COL_U:COL_U + SSM_WIDTH]
    u_g = u.reshape(SSM_NC, SSM_T, SSM_GROUPS, SSM_GROUP).transpose(2, 0, 1, 3).reshape(
        SSM_GROUPS, SSM_NC, SSM_TC)
    xr, xi = _s5_x(u_g, m_in)
    hr, hi = _s5_scan(xr, xi, at_r, at_i)
    y_g = _s5_y(u_g, hr, hi, toep, m_out)
    y = y_g.reshape(SSM_GROUPS, SSM_NC, SSM_T, SSM_GROUP).transpose(1, 2, 0, 3).reshape(SEQ, SSM_WIDTH)
    return _s5_glu(y, proj, d_skip, w_glu, b_glu)


def _rope(x, cos, sin_signed):
    w = x.shape[1]
    reps = w // LANES
    cosw = jnp.concatenate([cos] * reps, axis=1) if reps > 1 else cos
    sinw = jnp.concatenate([sin_signed] * reps, axis=1) if reps > 1 else sin_signed
    lane = lax.broadcasted_iota(jnp.int32, x.shape, 1) % HEAD_DIM
    swapped = jnp.where(lane < HEAD_DIM // 2,
                        pltpu.roll(x, w - HEAD_DIM // 2, axis=1),
                        pltpu.roll(x, HEAD_DIM // 2, axis=1))
    return x * cosw + swapped * sinw


def _swa_kernel(sink_ref, q_ref, kc_ref, kp_ref, vc_ref, vp_ref, cc_ref, sc_ref, cp_ref, sp_ref, o_ref):
    n = pl.program_id(0)
    blk = ATT_BLOCK
    q = _rope(q_ref[...].astype(F32), cc_ref[...], sc_ref[...]).astype(BF16)
    kc = _rope(kc_ref[...].astype(F32), cc_ref[...], sc_ref[...]).astype(BF16)
    kp = _rope(kp_ref[...].astype(F32), cp_ref[...], sp_ref[...]).astype(BF16)
    k_all = jnp.concatenate([kp, kc], axis=0)
    v_all = jnp.concatenate([vp_ref[...], vc_ref[...]], axis=0)
    qi = lax.broadcasted_iota(jnp.int32, (blk, 2 * blk), 0)
    kj = lax.broadcasted_iota(jnp.int32, (blk, 2 * blk), 1)
    diff = qi + blk - kj
    valid = (diff >= 0) & (diff < WINDOW) & ((kj >= blk) | (n > 0))
    scale = HEAD_DIM ** -0.5
    outs = []
    for h in range(ATT_KV_HEADS):
        kh = k_all[:, h * HEAD_DIM:(h + 1) * HEAD_DIM]
        vh = v_all[:, h * HEAD_DIM:(h + 1) * HEAD_DIM]
        for g in range(ATT_GROUP):
            hd = h * ATT_GROUP + g
            qh = q[:, hd * HEAD_DIM:(hd + 1) * HEAD_DIM]
            s = lax.dot_general(qh, kh, (((1,), (1,)), ((), ())), preferred_element_type=F32) * scale
            s = jnp.where(valid, s, NEG_BIG)
            sk = sink_ref[hd]
            m = jnp.maximum(jnp.max(s, axis=-1, keepdims=True), sk)
            p = jnp.exp(s - m)
            den = jnp.sum(p, axis=-1, keepdims=True) + jnp.exp(sk - m)
            o = jnp.dot(p.astype(BF16), vh, preferred_element_type=F32)
            outs.append(o / den)
    o_ref[...] = jnp.concatenate(outs, axis=1).astype(o_ref.dtype)


def _swa(proj, sinks, cos_t, sin_t):
    s = proj.shape[0]
    blk = ATT_BLOCK
    nb = s // blk
    prev = lambda n: jnp.maximum(n - 1, 0)
    tab_c = pl.BlockSpec((blk, LANES), lambda n: (n, 0))
    tab_p = pl.BlockSpec((blk, LANES), lambda n: (prev(n), 0))
    return pl.pallas_call(
        _swa_kernel,
        out_shape=jax.ShapeDtypeStruct((s, ATT_Q_WIDTH), BF16),
        grid=(nb,),
        in_specs=[
            pl.BlockSpec(memory_space=pltpu.SMEM),
            pl.BlockSpec((blk, ATT_Q_WIDTH), lambda n: (n, COL_AQ // ATT_Q_WIDTH)),
            pl.BlockSpec((blk, ATT_KV_WIDTH), lambda n: (n, COL_AK // ATT_KV_WIDTH)),
            pl.BlockSpec((blk, ATT_KV_WIDTH), lambda n: (prev(n), COL_AK // ATT_KV_WIDTH)),
            pl.BlockSpec((blk, ATT_KV_WIDTH), lambda n: (n, COL_AV // ATT_KV_WIDTH)),
            pl.BlockSpec((blk, ATT_KV_WIDTH), lambda n: (prev(n), COL_AV // ATT_KV_WIDTH)),
            tab_c, tab_c, tab_p, tab_p,
        ],
        out_specs=pl.BlockSpec((blk, ATT_Q_WIDTH), lambda n: (n, 0)),
        compiler_params=_cparams(("parallel",)),
        name="swa",
    )(sinks, proj, proj, proj, proj, proj, cos_t, sin_t, cos_t, sin_t)


def _log_sigmoid(x):
    return jnp.minimum(x, 0.0) - jnp.log(1.0 + jnp.exp(-jnp.abs(x)))


def _gla_kernel(q_ref, k_ref, v_ref, glr_ref, gout_ref, wg_ref, bg_ref, ng_ref, o_ref, st_ref, *, rb):
    @pl.when(pl.program_id(0) == 0)
    def _():
        st_ref[...] = jnp.zeros_like(st_ref)

    ch = GLA_CHUNK
    ncb = rb // ch
    logit = jnp.dot(glr_ref[...], wg_ref[...], preferred_element_type=F32) + bg_ref[...]
    log_a = _log_sigmoid(logit) * (1.0 / GLA_TAU)
    ri = lax.broadcasted_iota(jnp.int32, (rb, rb), 0)
    ci = lax.broadcasted_iota(jnp.int32, (rb, rb), 1)
    tri = ((ci <= ri) & ((ri // ch) == (ci // ch))).astype(BF16)
    a_hi = log_a.astype(BF16)
    a_lo = (log_a - a_hi.astype(F32)).astype(BF16)
    bcum = (jnp.dot(tri, a_hi, preferred_element_type=F32)
            + jnp.dot(tri, a_lo, preferred_element_type=F32))
    q = q_ref[...].astype(F32) * (GLA_DK ** -0.5)
    k = k_ref[...].astype(F32)
    q_in = (q * jnp.exp(bcum)).astype(BF16)
    k_in = (k * jnp.exp(-bcum)).astype(BF16)
    causal = (lax.broadcasted_iota(jnp.int32, (ch, ch), 1)
              <= lax.broadcasted_iota(jnp.int32, (ch, ch), 0))
    for c in range(ncb):
        r0 = c * ch
        b_last = bcum[r0 + ch - 1:r0 + ch, :]
        k_dec = (k[r0:r0 + ch] * jnp.exp(b_last - bcum[r0:r0 + ch])).astype(BF16)
        decay = jnp.exp(b_last)
        for h in range(GLA_HEADS):
            ks = slice(h * GLA_DK, (h + 1) * GLA_DK)
            vs = slice(h * GLA_DV, (h + 1) * GLA_DV)
            qh = q_in[r0:r0 + ch, ks]
            kh = k_in[r0:r0 + ch, ks]
            vh = v_ref[r0:r0 + ch, vs]
            st = st_ref[h]
            att = lax.dot_general(qh, kh, (((1,), (1,)), ((), ())), preferred_element_type=F32)
            att = jnp.where(causal, att, 0.0).astype(BF16)
            o = jnp.dot(att, vh, preferred_element_type=F32)
            o = o + lax.dot_general(qh, st.astype(BF16), (((1,), (1,)), ((), ())),
                                    preferred_element_type=F32)
            upd = lax.dot_general(vh, k_dec[:, ks], (((0,), (0,)), ((), ())),
                                  preferred_element_type=F32)
            st_ref[h] = st * decay[:, ks] + upd
            o = o * lax.rsqrt(jnp.mean(o * o, axis=-1, keepdims=True) + EPS)
            o = o * ng_ref[:, vs]
            go = gout_ref[r0:r0 + ch, vs].astype(F32)
            o = o * (go * jax.nn.sigmoid(go))
            o_ref[r0:r0 + ch, vs] = o.astype(o_ref.dtype)


def _gla(proj, w_gate, b_gate, norm_g, rb=256):
    s = proj.shape[0]
    return pl.pallas_call(
        functools.partial(_gla_kernel, rb=rb),
        out_shape=jax.ShapeDtypeStruct((s, GLA_V_WIDTH), BF16),
        grid=(s // rb,),
        in_specs=[
            pl.BlockSpec((rb, GLA_QK_WIDTH), lambda i: (i, COL_GQ // GLA_QK_WIDTH)),
            pl.BlockSpec((rb, GLA_QK_WIDTH), lambda i: (i, COL_GK // GLA_QK_WIDTH)),
            pl.BlockSpec((rb, GLA_V_WIDTH), lambda i: (i, COL_GV // GLA_V_WIDTH)),
            pl.BlockSpec((rb, LANES), lambda i: (i, COL_GLR // LANES)),
            pl.BlockSpec((rb, GLA_V_WIDTH), lambda i: (i, COL_GOUT // GLA_V_WIDTH)),
            pl.BlockSpec((LANES, GLA_QK_WIDTH), lambda i: (0, 0)),
            pl.BlockSpec((1, GLA_QK_WIDTH), lambda i: (0, 0)),
            pl.BlockSpec((1, GLA_V_WIDTH), lambda i: (0, 0)),
        ],
        out_specs=pl.BlockSpec((rb, GLA_V_WIDTH), lambda i: (i, 0)),
        scratch_shapes=[pltpu.VMEM((GLA_HEADS, GLA_DV, GLA_DK), F32)],
        compiler_params=_cparams(("arbitrary",)),
        name="gla",
    )(proj, proj, proj, proj, proj, w_gate, b_gate, norm_g)


def _mix_kernel(x_ref, ys_ref, ya_ref, yg_ref, gs_ref, ga_ref, gg_ref,
                ws_ref, wa_ref, wg_ref, wo_ref, o_ref):
    @pl.when(pl.program_id(1) == 0)
    def _():
        o_ref[...] = x_ref[...]

    def branch(y_ref, w_ref, gate_ref):
        proj = jnp.dot(y_ref[...], w_ref[...], preferred_element_type=F32)
        return jax.nn.sigmoid(gate_ref[...].astype(F32)) * proj

    m = branch(ys_ref, ws_ref, gs_ref) + branch(ya_ref, wa_ref, ga_ref) + branch(yg_ref, wg_ref, gg_ref)
    o_ref[...] += jnp.dot(m.astype(BF16), wo_ref[...], preferred_element_type=F32)


def _mix(x, proj, y_ssm, y_att, y_gla, w_bs, w_ba, w_bg, w_out, tm=512, tn=512):
    s, d = x.shape
    bw = y_ssm.shape[1]
    yblk = pl.BlockSpec((tm, bw), lambda i, j: (i, 0))
    wblk = pl.BlockSpec((bw, tn), lambda i, j: (0, j))

    def gate_blk(k):
        base = (COL_MERGE + k * d) // tn
        return pl.BlockSpec((tm, tn), lambda i, j: (i, base + j))

    return pl.pallas_call(
        _mix_kernel,
        out_shape=jax.ShapeDtypeStruct((s, d), F32),
        grid=(s // tm, d // tn),
        in_specs=[
            pl.BlockSpec((tm, d), lambda i, j: (i, 0)),
            yblk, yblk, yblk,
            gate_blk(0), gate_blk(1), gate_blk(2),
            wblk, wblk, wblk,
            pl.BlockSpec((tn, d), lambda i, j: (j, 0)),
        ],
        out_specs=pl.BlockSpec((tm, d), lambda i, j: (i, 0)),
        compiler_params=_cparams(("parallel", "arbitrary")),
        name="mix",
    )(x, y_ssm, y_att, y_gla, proj, proj, proj, w_bs, w_ba, w_bg, w_out)


def _ffn_kernel(x_ref, g_ref, w1_ref, w2_ref, fg_ref, o_ref, h_ref, *, tm, nf, final_norm):
    f = pl.program_id(1)

    @pl.when(f == 0)
    def _():
        _rms_rows(x_ref, g_ref, h_ref, tm)
        o_ref[...] = x_ref[...]

    a = jnp.dot(h_ref[...], w1_ref[...], preferred_element_type=F32)
    a = jnp.square(jnp.maximum(a, 0.0)).astype(BF16)
    o_ref[...] += jnp.dot(a, w2_ref[...], preferred_element_type=F32)

    if final_norm:
        @pl.when(f == nf - 1)
        def _():
            _rms_rows(o_ref, fg_ref, o_ref, tm)


def _ffn(x, g, w1, w2, fg, final_norm, tm=1024, tf=512):
    s, d = x.shape
    dff = w1.shape[1]
    nf = dff // tf
    return pl.pallas_call(
        functools.partial(_ffn_kernel, tm=tm, nf=nf, final_norm=final_norm),
        out_shape=jax.ShapeDtypeStruct((s, d), F32),
        grid=(s // tm, nf),
        in_specs=[
            pl.BlockSpec((tm, d), lambda i, f: (i, 0)),
            pl.BlockSpec((1, d), lambda i, f: (0, 0)),
            pl.BlockSpec((d, tf), lambda i, f: (0, f)),
            pl.BlockSpec((tf, d), lambda i, f: (f, 0)),
            pl.BlockSpec((1, d), lambda i, f: (0, 0)),
        ],
        out_specs=pl.BlockSpec((tm, d), lambda i, f: (i, 0)),
        scratch_shapes=[pltpu.VMEM((tm, d), BF16)],
        compiler_params=_cparams(("parallel", "arbitrary")),
        name="ffn",
    )(x, g, w1, w2, fg)


def _pack_w_in(w):
    sizes = [SSM_WIDTH, ATT_Q_WIDTH, ATT_KV_WIDTH, ATT_KV_WIDTH, GLA_QK_WIDTH, GLA_QK_WIDTH,
             GLA_V_WIDTH, GLA_GATE_RANK, GLA_V_WIDTH, 3 * D_MODEL]
    offs = np.concatenate([[0], np.cumsum(sizes)])
    u, aq, ak, av, gq, gk, gv, glr, gout, merge = [w[:, offs[i]:offs[i + 1]] for i in range(len(sizes))]
    pad = jnp.zeros((w.shape[0], GLR_PAD - GLA_GATE_RANK), w.dtype)
    return jnp.concatenate([u, aq, gv, gout, merge, gq, gk, ak, av, glr, pad], axis=1).astype(BF16)


def _rope_tables():
    half = HEAD_DIM // 2
    inv = ROPE_THETA ** (-jnp.arange(half, dtype=F32) / half)
    ang = jnp.arange(SEQ, dtype=jnp.int32).astype(F32)[:, None] * inv[None, :]
    cos = jnp.cos(ang)
    sin = jnp.sin(ang)
    cos_t = jnp.concatenate([cos, cos, cos, cos], axis=1)
    sin_t = jnp.concatenate([-sin, sin, -sin, sin], axis=1)
    return cos_t, sin_t


def kernel(x, norm1_g, w_in, ssm_lam_re, ssm_lam_im, ssm_log_step, ssm_b_re, ssm_b_im, ssm_c_re, ssm_c_im, ssm_d, ssm_w_glu, ssm_b_glu, att_sinks, gla_w_gate, gla_b_gate, gla_norm_g, w_branch_ssm, w_branch_att, w_branch_gla, w_out, norm2_g, w_ff1, w_ff2, final_norm_g):
    assert x.shape == (1, SEQ, D_MODEL)
    xs = x.reshape(SEQ, D_MODEL).astype(F32)
    cos_t, sin_t = _rope_tables()
    fg = final_norm_g.reshape(1, D_MODEL).astype(F32)
    for l in range(DEPTH):
        proj = _inproj(xs, norm1_g[l].reshape(1, D_MODEL).astype(F32), _pack_w_in(w_in[l]))
        prep = _s5_prepare(ssm_lam_re[l], ssm_lam_im[l], ssm_log_step[l], ssm_b_re[l], ssm_b_im[l],
                           ssm_c_re[l], ssm_c_im[l])
        y_ssm = _s5_branch(proj, prep, ssm_d[l].reshape(1, SSM_WIDTH).astype(F32),
                           ssm_w_glu[l].astype(BF16), ssm_b_glu[l].reshape(1, SSM_WIDTH).astype(F32))
        y_att = _swa(proj, att_sinks[l].astype(F32), cos_t, sin_t)
        wg_pad = jnp.concatenate(
            [gla_w_gate[l], jnp.zeros((LANES - GLA_GATE_RANK, GLA_QK_WIDTH), gla_w_gate.dtype)],
            axis=0).astype(BF16)
        y_gla = _gla(proj, wg_pad, gla_b_gate[l].reshape(1, GLA_QK_WIDTH).astype(F32),
                     gla_norm_g[l].reshape(1, GLA_V_WIDTH).astype(F32))
        xs = _mix(xs, proj, y_ssm, y_att, y_gla, w_branch_ssm[l].astype(BF16),
                  w_branch_att[l].astype(BF16), w_branch_gla[l].astype(BF16), w_out[l].astype(BF16))
        xs = _ffn(xs, norm2_g[l].reshape(1, D_MODEL).astype(F32), w_ff1[l].astype(BF16),
                  w_ff2[l].astype(BF16), fg, final_norm=(l == DEPTH - 1))
    return xs.reshape(1, SEQ, D_MODEL)
```

```python
import functools

import jax
import jax.numpy as jnp
import numpy as np
from jax import lax
from jax.experimental import pallas as pl
from jax.experimental.pallas import tpu as pltpu

F32 = jnp.float32
BF16 = jnp.bfloat16

D_MODEL = 2048
SEQ = 16384
DEPTH = 2
SSM_WIDTH = 1024
SSM_GROUP = 16
SSM_GROUPS = 64
SSM_STATE = 64
HEAD_DIM = 64
ATT_HEADS = 16
ATT_KV_HEADS = 4
ATT_GROUP = 4
ATT_Q_WIDTH = 1024
ATT_KV_WIDTH = 256
WINDOW = 128
ATT_BLOCK = 128
ROPE_THETA = 10000.0
GLA_HEADS = 4
GLA_V_WIDTH = 1024
GLA_DV = 256
GLA_DK = 128
GLA_QK_WIDTH = 512
GLA_GATE_RANK = 16
GLA_TAU = 16.0
GLA_CHUNK = 64
D_FF = 8192
EPS = 1e-6

LANES = 128
VMEM_LIMIT = 56 * 1024 * 1024

COL_MERGE = 0
COL_AQ = 6144
COL_GV = 7168
COL_GOUT = 8192
COL_GQ = 9216
COL_GK = 9728
COL_AK = 10240
COL_AV = 10496
COL_GLR = 10752
GLR_PAD = 512
PROJ_WIDTH = 11264
INPROJ_TN = 1024

SSM_T = 16
SSM_NC = SEQ // SSM_T
SSM_TC = SSM_T * SSM_GROUP
SSM_PAIRS = SSM_GROUPS // 2
SSM_STATES = SSM_GROUPS * SSM_STATE
SSM_OCT = LANES // SSM_GROUP
SSM_NOCT = SSM_GROUPS // SSM_OCT
NEG_BIG = -1e30


def _cparams(sem, vmem=VMEM_LIMIT):
    return pltpu.CompilerParams(dimension_semantics=sem, vmem_limit_bytes=vmem)


def _rms_rows(x_ref, g_ref, dst_ref, rows, chunk=128):
    g = g_ref[...]

    def body(r, c):
        sl = pl.ds(pl.multiple_of(r * chunk, chunk), chunk)
        xv = x_ref[sl, :].astype(F32)
        ms = jnp.mean(xv * xv, axis=-1, keepdims=True)
        dst_ref[sl, :] = (xv * lax.rsqrt(ms + EPS) * g).astype(dst_ref.dtype)
        return c

    lax.fori_loop(0, rows // chunk, body, 0)


def _inproj_kernel(x_ref, g_ref, w_ref, o_ref, uv_ref, h_ref, scr_ref, *, tm):
    j = pl.program_id(1)

    @pl.when(j == 0)
    def _():
        _rms_rows(x_ref, g_ref, h_ref, tm)
        res = jnp.dot(h_ref[...], w_ref[...], preferred_element_type=F32)
        nt = res.shape[1] // LANES
        for k in range(nt):
            scr_ref[k] = res[:, k * LANES:(k + 1) * LANES]
        for t in range(SSM_T):
            for k in range(nt):
                uv_ref[t, :, k * LANES:(k + 1) * LANES] = scr_ref[
                    k, pl.ds(t, tm // SSM_T, stride=SSM_T), :].astype(uv_ref.dtype)

    @pl.when(j > 0)
    def _():
        o_ref[...] = jnp.dot(h_ref[...], w_ref[...], preferred_element_type=F32).astype(o_ref.dtype)


def _inproj(x, g, w, layer, tm=1024):
    s, d = x.shape
    tn = INPROJ_TN
    n = w.shape[2]
    return pl.pallas_call(
        functools.partial(_inproj_kernel, tm=tm),
        out_shape=[jax.ShapeDtypeStruct((s, n - tn), BF16),
                   jax.ShapeDtypeStruct((SSM_T, s // SSM_T, SSM_WIDTH), BF16)],
        grid=(s // tm, n // tn),
        in_specs=[
            pl.BlockSpec((tm, d), lambda i, j: (i, 0)),
            pl.BlockSpec((None, 1, d), lambda i, j: (layer, 0, 0)),
            pl.BlockSpec((None, d, tn), lambda i, j: (layer, 0, j)),
        ],
        out_specs=[pl.BlockSpec((tm, tn), lambda i, j: (i, jnp.maximum(j - 1, 0))),
                   pl.BlockSpec((SSM_T, tm // SSM_T, SSM_WIDTH), lambda i, j: (0, i, 0))],
        scratch_shapes=[pltpu.VMEM((tm, d), BF16), pltpu.VMEM((tn // LANES, tm, LANES), F32)],
        compiler_params=_cparams(("parallel", "arbitrary")),
        name="inproj",
    )(x, g, w)


def _s5_prepare(lam_re, lam_im, log_step, b_re, b_im, c_re, c_im):
    hp = lax.Precision.HIGHEST
    t_len = SSM_T
    g_n, p_n = SSM_GROUPS, SSM_STATE
    step = jnp.exp(log_step.astype(F32))[:, None]
    lr = lam_re.astype(F32)
    li = lam_im.astype(F32)
    mag = jnp.exp(lr * step)
    ar = mag * jnp.cos(li * step)
    ai = mag * jnp.sin(li * step)
    den = lr * lr + li * li
    fr = ((ar - 1.0) * lr + ai * li) / den
    fi = (ai * lr - (ar - 1.0) * li) / den
    br = b_re.astype(F32)
    bi = b_im.astype(F32)
    bbr = fr[..., None] * br - fi[..., None] * bi
    bbi = fr[..., None] * bi + fi[..., None] * br
    jj = jnp.arange(t_len + 1, dtype=F32)[None, None, :]
    pmag = jnp.exp((lr * step)[..., None] * jj)
    pang = (li * step)[..., None] * jj
    pw_r = pmag * jnp.cos(pang)
    pw_i = pmag * jnp.sin(pang)
    cr = c_re.astype(F32)
    ci = c_im.astype(F32)
    pr_t = jnp.transpose(pw_r, (0, 2, 1))[:, :, None, :]
    pi_t = jnp.transpose(pw_i, (0, 2, 1))[:, :, None, :]
    ca_r = cr[:, None] * pr_t - ci[:, None] * pi_t
    ca_i = cr[:, None] * pi_t + ci[:, None] * pr_t
    kmat = (jnp.einsum('gjcp,gpd->gjcd', ca_r[:, :t_len], bbr, precision=hp)
            - jnp.einsum('gjcp,gpd->gjcd', ca_i[:, :t_len], bbi, precision=hp))
    s_idx = np.arange(t_len)[None, :, None]
    t_idx = np.arange(t_len)[None, None, :]
    sel = (t_idx - s_idx == np.arange(t_len)[:, None, None]).astype(np.float32)
    toep = jnp.einsum('jst,gjcd->gsdtc', jnp.asarray(sel), kmat, precision=hp).reshape(g_n, SSM_TC, SSM_TC)
    d_r = jnp.transpose(ca_r[:, 1:], (0, 3, 1, 2)).reshape(g_n, p_n, SSM_TC)
    d_i = jnp.transpose(ca_i[:, 1:], (0, 3, 1, 2)).reshape(g_n, p_n, SSM_TC)
    rev_r = pw_r[:, :, t_len - 1::-1][:, :, :t_len]
    rev_i = pw_i[:, :, t_len - 1::-1][:, :, :t_len]
    e_r = rev_r[..., None] * bbr[:, :, None, :] - rev_i[..., None] * bbi[:, :, None, :]
    e_i = rev_r[..., None] * bbi[:, :, None, :] + rev_i[..., None] * bbr[:, :, None, :]
    e_r = jnp.transpose(e_r, (0, 2, 3, 1)).reshape(g_n, SSM_TC, p_n)
    e_i = jnp.transpose(e_i, (0, 2, 3, 1)).reshape(g_n, SSM_TC, p_n)
    z_in = jnp.zeros((SSM_PAIRS, SSM_TC, p_n), F32)
    e_r2 = e_r.reshape(SSM_PAIRS, 2, SSM_TC, p_n)
    e_i2 = e_i.reshape(SSM_PAIRS, 2, SSM_TC, p_n)
    min_top = jnp.concatenate([e_r2[:, 0], z_in, e_i2[:, 0], z_in], axis=-1)
    min_bot = jnp.concatenate([z_in, e_r2[:, 1], z_in, e_i2[:, 1]], axis=-1)
    m_in = jnp.concatenate([min_top, min_bot], axis=1)
    z_out = jnp.zeros((SSM_PAIRS, p_n, SSM_TC), F32)
    d_r2 = d_r.reshape(SSM_PAIRS, 2, p_n, SSM_TC)
    d_i2 = d_i.reshape(SSM_PAIRS, 2, p_n, SSM_TC)
    m_out = jnp.concatenate([
        jnp.concatenate([d_r2[:, 0], z_out], axis=-1),
        jnp.concatenate([z_out, d_r2[:, 1]], axis=-1),
        jnp.concatenate([-d_i2[:, 0], z_out], axis=-1),
        jnp.concatenate([z_out, -d_i2[:, 1]], axis=-1)], axis=1)
    at_r = pw_r[:, :, t_len].reshape(1, SSM_STATES)
    at_i = pw_i[:, :, t_len].reshape(1, SSM_STATES)
    return toep.astype(BF16), m_in.astype(BF16), m_out.astype(BF16), at_r, at_i


def _granule_transpose(src):
    grp = lax.broadcasted_iota(jnp.int32, src[0].shape, 1) // SSM_GROUP
    out = []
    for b in range(SSM_OCT):
        acc = None
        for a in range(SSM_OCT):
            r = src[a]
            shift = ((a - b) * SSM_GROUP) % LANES
            if shift:
                r = pltpu.roll(r, shift, axis=1)
            acc = r if acc is None else jnp.where(grp == a, r, acc)
        out.append(acc)
    return out


S5_RELAYOUT_ROWS = 64


def _s5_x_kernel(uv_ref, min_ref, u_ref, xr_ref, xi_ref):
    rb = S5_RELAYOUT_ROWS

    def body(r, c):
        rows = pl.ds(pl.multiple_of(r * rb, rb), rb)
        for half in range(2):
            src = [pltpu.bitcast(uv_ref[half * SSM_OCT + tp, rows, :], jnp.int32) for tp in range(SSM_OCT)]
            out = _granule_transpose(src)
            for g in range(SSM_OCT):
                u_ref[g, rows, half * LANES:(half + 1) * LANES] = pltpu.bitcast(out[g], BF16)
        return c

    lax.fori_loop(0, SSM_NC // rb, body, 0)
    for pr in range(SSM_OCT // 2):
        up = jnp.concatenate([u_ref[2 * pr], u_ref[2 * pr + 1]], axis=1)
        x = jnp.dot(up, min_ref[pr], preferred_element_type=F32)
        xr_ref[:, pr * LANES:(pr + 1) * LANES] = x[:, :LANES]
        xi_ref[:, pr * LANES:(pr + 1) * LANES] = x[:, LANES:]


def _s5_x(uv, m_in, layer):
    nc = SSM_NC
    ppo = SSM_OCT // 2
    sw = ppo * LANES
    return pl.pallas_call(
        _s5_x_kernel,
        out_shape=[jax.ShapeDtypeStruct((SSM_GROUPS, nc, SSM_TC), BF16),
                   jax.ShapeDtypeStruct((nc, SSM_STATES), F32),
                   jax.ShapeDtypeStruct((nc, SSM_STATES), F32)],
        grid=(SSM_NOCT,),
        in_specs=[
            pl.BlockSpec((SSM_T, nc, LANES), lambda k: (0, 0, k)),
            pl.BlockSpec((None, ppo, 2 * SSM_TC, 256), lambda k: (layer, k, 0, 0)),
        ],
        out_specs=[pl.BlockSpec((SSM_OCT, nc, SSM_TC), lambda k: (k, 0, 0)),
                   pl.BlockSpec((nc, sw), lambda k: (0, k)),
                   pl.BlockSpec((nc, sw), lambda k: (0, k))],
        compiler_params=_cparams(("parallel",)),
        name="s5_x",
    )(uv, m_in)


def _s5_scan_kernel(xr_ref, xi_ref, ar_ref, ai_ref, hr_ref, hi_ref, *, nc):
    ar = ar_ref[...]
    ai = ai_ref[...]

    def body(n, carry):
        hr, hi = carry
        row = pl.ds(n, 1)
        hr_ref[row, :] = hr
        hi_ref[row, :] = hi
        xr = xr_ref[row, :]
        xi = xi_ref[row, :]
        return (ar * hr - ai * hi + xr, ar * hi + ai * hr + xi)

    zero = jnp.zeros_like(ar)
    lax.fori_loop(0, nc, body, (zero, zero), unroll=8)


def _s5_scan(xr, xi, at_r, at_i, layer, lb=1024):
    nc, ns = xr.shape
    blk = pl.BlockSpec((nc, lb), lambda i: (0, i))
    tab = pl.BlockSpec((None, 1, lb), lambda i: (layer, 0, i))
    return pl.pallas_call(
        functools.partial(_s5_scan_kernel, nc=nc),
        out_shape=[jax.ShapeDtypeStruct((nc, ns), F32)] * 2,
        grid=(ns // lb,),
        in_specs=[blk, blk, tab, tab],
        out_specs=[blk, blk],
        compiler_params=_cparams(("parallel",)),
        name="s5_scan",
    )(xr, xi, at_r, at_i)


def _s5_y_kernel(u_ref, hr_ref, hi_ref, toep_ref, mout_ref, yv_ref, y_scr):
    for pr in range(SSM_OCT // 2):
        hcat = jnp.concatenate([hr_ref[:, pr * LANES:(pr + 1) * LANES],
                                hi_ref[:, pr * LANES:(pr + 1) * LANES]], axis=1).astype(BF16)
        carry = jnp.dot(hcat, mout_ref[pr], preferred_element_type=F32)
        for k in range(2):
            g = 2 * pr + k
            y = jnp.dot(u_ref[g], toep_ref[g], preferred_element_type=F32)
            y_scr[g] = (y + carry[:, k * SSM_TC:(k + 1) * SSM_TC]).astype(BF16)

    rb = S5_RELAYOUT_ROWS

    def body(r, c):
        rows = pl.ds(pl.multiple_of(r * rb, rb), rb)
        for half in range(2):
            src = [pltpu.bitcast(y_scr[g, rows, half * LANES:(half + 1) * LANES], jnp.int32)
                   for g in range(SSM_OCT)]
            out = _granule_transpose(src)
            for tp in range(SSM_OCT):
                yv_ref[half * SSM_OCT + tp, rows, :] = pltpu.bitcast(out[tp], BF16)
        return c

    lax.fori_loop(0, SSM_NC // rb, body, 0)


def _s5_y(u_g, hr, hi, toep, m_out, layer):
    nc = SSM_NC
    ppo = SSM_OCT // 2
    hblk = pl.BlockSpec((nc, ppo * LANES), lambda k: (0, k))
    return pl.pallas_call(
        _s5_y_kernel,
        out_shape=jax.ShapeDtypeStruct((SSM_T, nc, SSM_WIDTH), BF16),
        grid=(SSM_NOCT,),
        in_specs=[
            pl.BlockSpec((SSM_OCT, nc, SSM_TC), lambda k: (k, 0, 0)),
            hblk, hblk,
            pl.BlockSpec((None, SSM_OCT, SSM_TC, SSM_TC), lambda k: (layer, k, 0, 0)),
            pl.BlockSpec((None, ppo, 256, 2 * SSM_TC), lambda k: (layer, k, 0, 0)),
        ],
        out_specs=pl.BlockSpec((SSM_T, nc, LANES), lambda k: (0, 0, k)),
        scratch_shapes=[pltpu.VMEM((SSM_OCT, nc, SSM_TC), BF16)],
        compiler_params=_cparams(("parallel",)),
        name="s5_y",
    )(u_g, hr, hi, toep, m_out)


def _s5_glu_kernel(yv_ref, uv_ref, d_ref, w_ref, b_ref, o_ref, z_scr, zb_scr, g_scr, nat_scr, *, ncb):
    d = d_ref[...]
    for t in range(SSM_T):
        z = jax.nn.gelu(yv_ref[t].astype(F32) + d * uv_ref[t].astype(F32))
        z_scr[t * ncb:(t + 1) * ncb, :] = z
        zb_scr[t * ncb:(t + 1) * ncb, :] = z.astype(BF16)
    g_scr[...] = jnp.dot(zb_scr[...], w_ref[...], preferred_element_type=F32)
    b = b_ref[...]
    nt = SSM_WIDTH // LANES
    for t in range(SSM_T):
        rows = slice(t * ncb, (t + 1) * ncb)
        out = z_scr[rows, :] * jax.nn.sigmoid(g_scr[rows, :] + b)
        for k in range(nt):
            nat_scr[k, pl.ds(t, ncb, stride=SSM_T), :] = out[:, k * LANES:(k + 1) * LANES]
    for k in range(nt):
        o_ref[:, k * LANES:(k + 1) * LANES] = nat_scr[k].astype(o_ref.dtype)


def _s5_glu(yv, uv, d_skip, w_glu, b_glu, layer, ncb=64):
    w = SSM_WIDTH
    tm = ncb * SSM_T
    vblk = pl.BlockSpec((SSM_T, ncb, w), lambda i: (0, i, 0))
    vec = pl.BlockSpec((None, 1, w), lambda i: (layer, 0, 0))
    return pl.pallas_call(
        functools.partial(_s5_glu_kernel, ncb=ncb),
        out_shape=jax.ShapeDtypeStruct((SEQ, w), BF16),
        grid=(SSM_NC // ncb,),
        in_specs=[vblk, vblk, vec,
                  pl.BlockSpec((None, w, w), lambda i: (layer, 0, 0)),
                  vec],
        out_specs=pl.BlockSpec((tm, w), lambda i: (i, 0)),
        scratch_shapes=[pltpu.VMEM((tm, w), F32), pltpu.VMEM((tm, w), BF16), pltpu.VMEM((tm, w), F32),
                        pltpu.VMEM((w // LANES, tm, LANES), F32)],
        compiler_params=_cparams(("parallel",)),
        name="s5_glu",
    )(yv, uv, d_skip, w_glu, b_glu)


def _s5_branch(uv, prep, d_skip, w_glu, b_glu, layer):
    toep, m_in, m_out, at_r, at_i = prep
    u_g, xr, xi = _s5_x(uv, m_in, layer)
    hr, hi = _s5_scan(xr, xi, at_r, at_i, layer)
    yv = _s5_y(u_g, hr, hi, toep, m_out, layer)
    return _s5_glu(yv, uv, d_skip, w_glu, b_glu, layer)


def _rope(x, cos, sin_signed):
    w = x.shape[1]
    reps = w // LANES
    cosw = jnp.concatenate([cos] * reps, axis=1) if reps > 1 else cos
    sinw = jnp.concatenate([sin_signed] * reps, axis=1) if reps > 1 else sin_signed
    lane = lax.broadcasted_iota(jnp.int32, x.shape, 1) % HEAD_DIM
    swapped = jnp.where(lane < HEAD_DIM // 2,
                        pltpu.roll(x, w - HEAD_DIM // 2, axis=1),
                        pltpu.roll(x, HEAD_DIM // 2, axis=1))
    return x * cosw + swapped * sinw


def _swa_kernel(sink_ref, q_ref, kc_ref, kp_ref, vc_ref, vp_ref, cc_ref, sc_ref, cp_ref, sp_ref, o_ref):
    n = pl.program_id(0)
    blk = ATT_BLOCK
    gq = ATT_GROUP
    cos_c = cc_ref[...]
    sin_c = sc_ref[...]
    kc = _rope(kc_ref[...].astype(F32), cos_c, sin_c).astype(BF16)
    kp = _rope(kp_ref[...].astype(F32), cp_ref[...], sp_ref[...]).astype(BF16)
    k_all = jnp.concatenate([kp, kc], axis=0)
    v_all = jnp.concatenate([vp_ref[...], vc_ref[...]], axis=0)
    row = lax.broadcasted_iota(jnp.int32, (gq * blk, blk), 0) % blk
    col = lax.broadcasted_iota(jnp.int32, (gq * blk, blk), 1)
    cur = col <= row
    prev_bias = jnp.where(n > 0, 0.0, NEG_BIG)
    scale = HEAD_DIM ** -0.5
    for h in range(ATT_KV_HEADS):
        qw = gq * HEAD_DIM
        qh = _rope(q_ref[:, h * qw:(h + 1) * qw].astype(F32) * scale, cos_c, sin_c).astype(BF16)
        q4 = jnp.concatenate([qh[:, g * HEAD_DIM:(g + 1) * HEAD_DIM] for g in range(gq)], axis=0)
        kh = k_all[:, h * HEAD_DIM:(h + 1) * HEAD_DIM]
        vh = v_all[:, h * HEAD_DIM:(h + 1) * HEAD_DIM]
        s2 = lax.dot_general(q4, kh, (((1,), (1,)), ((), ())), preferred_element_type=F32)
        s = jnp.where(cur, s2[:, blk:], s2[:, :blk] + prev_bias)
        sk = jnp.concatenate([jnp.full((blk, 1), sink_ref[h * gq + g], F32) for g in range(gq)], axis=0)
        m = jnp.maximum(jnp.max(s, axis=-1, keepdims=True), sk)
        p = jnp.exp(s - m)
        den = jnp.sum(p, axis=-1, keepdims=True) + jnp.exp(sk - m)
        p2 = jnp.concatenate([jnp.where(cur, 0.0, p), jnp.where(cur, p, 0.0)], axis=1).astype(BF16)
        o4 = jnp.dot(p2, vh, preferred_element_type=F32) / den
        o_ref[:, h * qw:(h + 1) * qw] = jnp.concatenate(
            [o4[g * blk:(g + 1) * blk] for g in range(gq)], axis=1).astype(o_ref.dtype)


def _swa(proj, sinks, cos_t, sin_t):
    s = proj.shape[0]
    blk = ATT_BLOCK
    nb = s // blk
    prev = lambda n: jnp.maximum(n - 1, 0)
    tab_c = pl.BlockSpec((blk, LANES), lambda n: (n, 0))
    tab_p = pl.BlockSpec((blk, LANES), lambda n: (prev(n), 0))
    return pl.pallas_call(
        _swa_kernel,
        out_shape=jax.ShapeDtypeStruct((s, ATT_Q_WIDTH), BF16),
        grid=(nb,),
        in_specs=[
            pl.BlockSpec(memory_space=pltpu.SMEM),
            pl.BlockSpec((blk, ATT_Q_WIDTH), lambda n: (n, COL_AQ // ATT_Q_WIDTH)),
            pl.BlockSpec((blk, ATT_KV_WIDTH), lambda n: (n, COL_AK // ATT_KV_WIDTH)),
            pl.BlockSpec((blk, ATT_KV_WIDTH), lambda n: (prev(n), COL_AK // ATT_KV_WIDTH)),
            pl.BlockSpec((blk, ATT_KV_WIDTH), lambda n: (n, COL_AV // ATT_KV_WIDTH)),
            pl.BlockSpec((blk, ATT_KV_WIDTH), lambda n: (prev(n), COL_AV // ATT_KV_WIDTH)),
            tab_c, tab_c, tab_p, tab_p,
        ],
        out_specs=pl.BlockSpec((blk, ATT_Q_WIDTH), lambda n: (n, 0)),
        compiler_params=_cparams(("parallel",)),
        name="swa",
    )(sinks, proj, proj, proj, proj, proj, cos_t, sin_t, cos_t, sin_t)


def _log_sigmoid(x):
    return jnp.minimum(x, 0.0) - jnp.log(1.0 + jnp.exp(-jnp.abs(x)))


def _gla_kernel(q_ref, k_ref, v_ref, glr_ref, gout_ref, wg_ref, bg_ref, ng_ref, o_ref, st_ref, *, rb):
    @pl.when(pl.program_id(0) == 0)
    def _():
        st_ref[...] = jnp.zeros_like(st_ref)

    ch = GLA_CHUNK
    ncb = rb // ch
    logit = jnp.dot(glr_ref[...], wg_ref[...], preferred_element_type=F32) + bg_ref[...]
    log_a = _log_sigmoid(logit) * (1.0 / GLA_TAU)
    ri = lax.broadcasted_iota(jnp.int32, (rb, rb), 0)
    ci = lax.broadcasted_iota(jnp.int32, (rb, rb), 1)
    tri = ((ci <= ri) & ((ri // ch) == (ci // ch))).astype(BF16)
    a_hi = log_a.astype(BF16)
    a_lo = (log_a - a_hi.astype(F32)).astype(BF16)
    bcum = (jnp.dot(tri, a_hi, preferred_element_type=F32)
            + jnp.dot(tri, a_lo, preferred_element_type=F32))
    q = q_ref[...].astype(F32) * (GLA_DK ** -0.5)
    k = k_ref[...].astype(F32)
    q_in = (q * jnp.exp(bcum)).astype(BF16)
    k_in = (k * jnp.exp(-bcum)).astype(BF16)
    causal = (lax.broadcasted_iota(jnp.int32, (ch, ch), 1)
              <= lax.broadcasted_iota(jnp.int32, (ch, ch), 0))
    for c in range(ncb):
        r0 = c * ch
        b_last = bcum[r0 + ch - 1:r0 + ch, :]
        k_dec = (k[r0:r0 + ch] * jnp.exp(b_last - bcum[r0:r0 + ch])).astype(BF16)
        decay = jnp.exp(b_last)
        for h in range(GLA_HEADS):
            ks = slice(h * GLA_DK, (h + 1) * GLA_DK)
            vs = slice(h * GLA_DV, (h + 1) * GLA_DV)
            qh = q_in[r0:r0 + ch, ks]
            kh = k_in[r0:r0 + ch, ks]
            vh = v_ref[r0:r0 + ch, vs]
            st = st_ref[h]
            att = lax.dot_general(qh, kh, (((1,), (1,)), ((), ())), preferred_element_type=F32)
            att = jnp.where(causal, att, 0.0).astype(BF16)
            o = jnp.dot(att, vh, preferred_element_type=F32)
            o = o + lax.dot_general(qh, st.astype(BF16), (((1,), (1,)), ((), ())),
                                    preferred_element_type=F32)
            upd = lax.dot_general(vh, k_dec[:, ks], (((0,), (0,)), ((), ())),
                                  preferred_element_type=F32)
            st_ref[h] = st * decay[:, ks] + upd
            o = o * lax.rsqrt(jnp.mean(o * o, axis=-1, keepdims=True) + EPS)
            o = o * ng_ref[:, vs]
            go = gout_ref[r0:r0 + ch, vs].astype(F32)
            o = o * (go * jax.nn.sigmoid(go))
            o_ref[r0:r0 + ch, vs] = o.astype(o_ref.dtype)


def _gla(proj, w_gate, b_gate, norm_g, layer, rb=256):
    s = proj.shape[0]
    return pl.pallas_call(
        functools.partial(_gla_kernel, rb=rb),
        out_shape=jax.ShapeDtypeStruct((s, GLA_V_WIDTH), BF16),
        grid=(s // rb,),
        in_specs=[
            pl.BlockSpec((rb, GLA_QK_WIDTH), lambda i: (i, COL_GQ // GLA_QK_WIDTH)),
            pl.BlockSpec((rb, GLA_QK_WIDTH), lambda i: (i, COL_GK // GLA_QK_WIDTH)),
            pl.BlockSpec((rb, GLA_V_WIDTH), lambda i: (i, COL_GV // GLA_V_WIDTH)),
            pl.BlockSpec((rb, LANES), lambda i: (i, COL_GLR // LANES)),
            pl.BlockSpec((rb, GLA_V_WIDTH), lambda i: (i, COL_GOUT // GLA_V_WIDTH)),
            pl.BlockSpec((None, LANES, GLA_QK_WIDTH), lambda i: (layer, 0, 0)),
            pl.BlockSpec((None, 1, GLA_QK_WIDTH), lambda i: (layer, 0, 0)),
            pl.BlockSpec((None, 1, GLA_V_WIDTH), lambda i: (layer, 0, 0)),
        ],
        out_specs=pl.BlockSpec((rb, GLA_V_WIDTH), lambda i: (i, 0)),
        scratch_shapes=[pltpu.VMEM((GLA_HEADS, GLA_DV, GLA_DK), F32)],
        compiler_params=_cparams(("arbitrary",)),
        name="gla",
    )(proj, proj, proj, proj, proj, w_gate, b_gate, norm_g)


MIX_TN = 512


def _mix_kernel(x_ref, ys_ref, ya_ref, yg_ref, gate_ref, ws_ref, wa_ref, wg_ref, wo_ref, o_ref, m_scr):
    d = D_MODEL
    for j in range(d // MIX_TN):
        cols = slice(j * MIX_TN, (j + 1) * MIX_TN)
        m = None
        for k, (y_ref, w_ref) in enumerate(((ys_ref, ws_ref), (ya_ref, wa_ref), (yg_ref, wg_ref))):
            proj = jnp.dot(y_ref[...], w_ref[:, cols], preferred_element_type=F32)
            gate = gate_ref[:, k * d + j * MIX_TN:k * d + (j + 1) * MIX_TN].astype(F32)
            term = jax.nn.sigmoid(gate) * proj
            m = term if m is None else m + term
        m_scr[:, cols] = m.astype(BF16)
    o_ref[...] = x_ref[...] + jnp.dot(m_scr[...], wo_ref[...], preferred_element_type=F32)


def _mix(x, proj, y_ssm, y_att, y_gla, w_bs, w_ba, w_bg, w_out, layer, tm=256):
    s, d = x.shape
    bw = y_ssm.shape[1]
    yblk = pl.BlockSpec((tm, bw), lambda i: (i, 0))
    wblk = pl.BlockSpec((None, bw, d), lambda i: (layer, 0, 0), pipeline_mode=pl.Buffered(1))
    return pl.pallas_call(
        _mix_kernel,
        out_shape=jax.ShapeDtypeStruct((s, d), F32),
        grid=(s // tm,),
        in_specs=[
            pl.BlockSpec((tm, d), lambda i: (i, 0)),
            yblk, yblk, yblk,
            pl.BlockSpec((tm, 3 * d), lambda i: (i, COL_MERGE // (3 * d))),
            wblk, wblk, wblk,
            pl.BlockSpec((None, d, d), lambda i: (layer, 0, 0), pipeline_mode=pl.Buffered(1)),
        ],
        out_specs=pl.BlockSpec((tm, d), lambda i: (i, 0)),
        scratch_shapes=[pltpu.VMEM((tm, d), BF16)],
        compiler_params=_cparams(("parallel",)),
        name="mix",
    )(x, y_ssm, y_att, y_gla, proj, w_bs, w_ba, w_bg, w_out)


def _ffn_kernel(x_ref, g_ref, w1_ref, w2_ref, fg_ref, o_ref, h_ref, *, tm, nf, final_norm):
    f = pl.program_id(1)

    @pl.when(f == 0)
    def _():
        _rms_rows(x_ref, g_ref, h_ref, tm)
        o_ref[...] = x_ref[...]

    a = jnp.dot(h_ref[...], w1_ref[...], preferred_element_type=F32)
    a = jnp.square(jnp.maximum(a, 0.0)).astype(BF16)
    o_ref[...] += jnp.dot(a, w2_ref[...], preferred_element_type=F32)

    if final_norm:
        @pl.when(f == nf - 1)
        def _():
            _rms_rows(o_ref, fg_ref, o_ref, tm)


def _ffn(x, g, w1, w2, fg, layer, final_norm, tm=1024, tf=512):
    s, d = x.shape
    dff = w1.shape[2]
    nf = dff // tf
    return pl.pallas_call(
        functools.partial(_ffn_kernel, tm=tm, nf=nf, final_norm=final_norm),
        out_shape=jax.ShapeDtypeStruct((s, d), F32),
        grid=(s // tm, nf),
        in_specs=[
            pl.BlockSpec((tm, d), lambda i, f: (i, 0)),
            pl.BlockSpec((None, 1, d), lambda i, f: (layer, 0, 0)),
            pl.BlockSpec((None, d, tf), lambda i, f: (layer, 0, f)),
            pl.BlockSpec((None, tf, d), lambda i, f: (layer, f, 0)),
            pl.BlockSpec((1, d), lambda i, f: (0, 0)),
        ],
        out_specs=pl.BlockSpec((tm, d), lambda i, f: (i, 0)),
        scratch_shapes=[pltpu.VMEM((tm, d), BF16)],
        compiler_params=_cparams(("parallel", "arbitrary")),
        name="ffn",
    )(x, g, w1, w2, fg)


def _pack_w_in(w):
    sizes = [SSM_WIDTH, ATT_Q_WIDTH, ATT_KV_WIDTH, ATT_KV_WIDTH, GLA_QK_WIDTH, GLA_QK_WIDTH,
             GLA_V_WIDTH, GLA_GATE_RANK, GLA_V_WIDTH, 3 * D_MODEL]
    offs = np.concatenate([[0], np.cumsum(sizes)])
    u, aq, ak, av, gq, gk, gv, glr, gout, merge = [
        w[:, :, offs[i]:offs[i + 1]].astype(BF16) for i in range(len(sizes))]
    pad = jnp.zeros(w.shape[:2] + (GLR_PAD - GLA_GATE_RANK,), BF16)
    return jnp.concatenate([u, merge, aq, gv, gout, gq, gk, ak, av, glr, pad], axis=2)


def _rope_tables():
    half = HEAD_DIM // 2
    inv = ROPE_THETA ** (-jnp.arange(half, dtype=F32) / half)
    ang = jnp.arange(SEQ, dtype=jnp.int32).astype(F32)[:, None] * inv[None, :]
    cos = jnp.cos(ang)
    sin = jnp.sin(ang)
    cos_t = jnp.concatenate([cos, cos, cos, cos], axis=1)
    sin_t = jnp.concatenate([-sin, sin, -sin, sin], axis=1)
    return cos_t, sin_t


def kernel(x, norm1_g, w_in, ssm_lam_re, ssm_lam_im, ssm_log_step, ssm_b_re, ssm_b_im, ssm_c_re, ssm_c_im, ssm_d, ssm_w_glu, ssm_b_glu, att_sinks, gla_w_gate, gla_b_gate, gla_norm_g, w_branch_ssm, w_branch_att, w_branch_gla, w_out, norm2_g, w_ff1, w_ff2, final_norm_g):
    assert x.shape == (1, SEQ, D_MODEL)
    nl = DEPTH
    xs = x.reshape(SEQ, D_MODEL).astype(F32)
    cos_t, sin_t = _rope_tables()
    fg = final_norm_g.reshape(1, D_MODEL).astype(F32)
    w_in_p = _pack_w_in(w_in)
    n1 = norm1_g.reshape(nl, 1, D_MODEL).astype(F32)
    n2 = norm2_g.reshape(nl, 1, D_MODEL).astype(F32)
    prep = jax.vmap(_s5_prepare)(ssm_lam_re, ssm_lam_im, ssm_log_step, ssm_b_re, ssm_b_im, ssm_c_re, ssm_c_im)
    d_skip = ssm_d.reshape(nl, 1, SSM_WIDTH).astype(F32)
    w_glu = ssm_w_glu.astype(BF16)
    b_glu = ssm_b_glu.reshape(nl, 1, SSM_WIDTH).astype(F32)
    sinks = att_sinks.astype(F32)
    wg_pad = jnp.concatenate(
        [gla_w_gate.astype(BF16), jnp.zeros((nl, LANES - GLA_GATE_RANK, GLA_QK_WIDTH), BF16)], axis=1)
    bg = gla_b_gate.reshape(nl, 1, GLA_QK_WIDTH).astype(F32)
    ng = gla_norm_g.reshape(nl, 1, GLA_V_WIDTH).astype(F32)
    w_bs = w_branch_ssm.astype(BF16)
    w_ba = w_branch_att.astype(BF16)
    w_bg = w_branch_gla.astype(BF16)
    w_o = w_out.astype(BF16)
    w1 = w_ff1.astype(BF16)
    w2 = w_ff2.astype(BF16)
    for l in range(nl):
        proj, uv = _inproj(xs, n1, w_in_p, l)
        y_ssm = _s5_branch(uv, prep, d_skip, w_glu, b_glu, l)
        y_att = _swa(proj, sinks[l], cos_t, sin_t)
        y_gla = _gla(proj, wg_pad, bg, ng, l)
        xs = _mix(xs, proj, y_ssm, y_att, y_gla, w_bs, w_ba, w_bg, w_o, l)
        xs = _ffn(xs, n2, w1, w2, fg, l, final_norm=(l == nl - 1))
    return xs.reshape(1, SEQ, D_MODEL)
```

```python
import functools

import jax
import jax.numpy as jnp
import numpy as np
from jax import lax
from jax.experimental import pallas as pl
from jax.experimental.pallas import tpu as pltpu

F32 = jnp.float32
BF16 = jnp.bfloat16

D_MODEL = 2048
SEQ = 16384
DEPTH = 2
SSM_WIDTH = 1024
SSM_GROUP = 16
SSM_GROUPS = 64
SSM_STATE = 64
HEAD_DIM = 64
ATT_HEADS = 16
ATT_KV_HEADS = 4
ATT_GROUP = 4
ATT_Q_WIDTH = 1024
ATT_KV_WIDTH = 256
WINDOW = 128
ATT_BLOCK = 128
ROPE_THETA = 10000.0
GLA_HEADS = 4
GLA_V_WIDTH = 1024
GLA_DV = 256
GLA_DK = 128
GLA_QK_WIDTH = 512
GLA_GATE_RANK = 16
GLA_TAU = 16.0
GLA_CHUNK = 64
D_FF = 8192
EPS = 1e-6

LANES = 128
VMEM_LIMIT = 56 * 1024 * 1024

COL_MERGE = 0
COL_AQ = 6144
COL_GV = 7168
COL_GOUT = 8192
COL_GQ = 9216
COL_GK = 9728
COL_AK = 10240
COL_AV = 10496
COL_GLR = 10752
GLR_PAD = 512
PROJ_WIDTH = 11264
INPROJ_TN = 1024

SSM_T = 16
SSM_NC = SEQ // SSM_T
SSM_TC = SSM_T * SSM_GROUP
SSM_PAIRS = SSM_GROUPS // 2
SSM_STATES = SSM_GROUPS * SSM_STATE
SSM_OCT = LANES // SSM_GROUP
SSM_NOCT = SSM_GROUPS // SSM_OCT
NEG_BIG = -1e30


def _cparams(sem, vmem=VMEM_LIMIT):
    return pltpu.CompilerParams(dimension_semantics=sem, vmem_limit_bytes=vmem)


def _rms_rows(x_ref, g_ref, dst_ref, rows, chunk=128):
    g = g_ref[...]

    def body(r, c):
        sl = pl.ds(pl.multiple_of(r * chunk, chunk), chunk)
        xv = x_ref[sl, :].astype(F32)
        ms = jnp.mean(xv * xv, axis=-1, keepdims=True)
        dst_ref[sl, :] = (xv * lax.rsqrt(ms + EPS) * g).astype(dst_ref.dtype)
        return c

    lax.fori_loop(0, rows // chunk, body, 0)


def _inproj_kernel(x_ref, g_ref, w_ref, o_ref, uv_ref, h_ref, scr_ref, *, tm):
    j = pl.program_id(1)

    @pl.when(j == 0)
    def _():
        _rms_rows(x_ref, g_ref, h_ref, tm)
        res = jnp.dot(h_ref[...], w_ref[...], preferred_element_type=F32)
        nt = res.shape[1] // LANES
        for k in range(nt):
            scr_ref[k] = res[:, k * LANES:(k + 1) * LANES]
        for t in range(SSM_T):
            for k in range(nt):
                uv_ref[t, :, k * LANES:(k + 1) * LANES] = scr_ref[
                    k, pl.ds(t, tm // SSM_T, stride=SSM_T), :].astype(uv_ref.dtype)

    @pl.when(j > 0)
    def _():
        o_ref[...] = jnp.dot(h_ref[...], w_ref[...], preferred_element_type=F32).astype(o_ref.dtype)


def _inproj(x, g, w, layer, tm=1024):
    s, d = x.shape
    tn = INPROJ_TN
    n = w.shape[2]
    return pl.pallas_call(
        functools.partial(_inproj_kernel, tm=tm),
        out_shape=[jax.ShapeDtypeStruct((s, n - tn), BF16),
                   jax.ShapeDtypeStruct((SSM_T, s // SSM_T, SSM_WIDTH), BF16)],
        grid=(s // tm, n // tn),
        in_specs=[
            pl.BlockSpec((tm, d), lambda i, j: (i, 0)),
            pl.BlockSpec((None, 1, d), lambda i, j: (layer, 0, 0)),
            pl.BlockSpec((None, d, tn), lambda i, j: (layer, 0, j)),
        ],
        out_specs=[pl.BlockSpec((tm, tn), lambda i, j: (i, jnp.maximum(j - 1, 0))),
                   pl.BlockSpec((SSM_T, tm // SSM_T, SSM_WIDTH), lambda i, j: (0, i, 0))],
        scratch_shapes=[pltpu.VMEM((tm, d), BF16), pltpu.VMEM((tn // LANES, tm, LANES), F32)],
        compiler_params=_cparams(("parallel", "arbitrary")),
        name="inproj",
    )(x, g, w)


def _s5_prepare(lam_re, lam_im, log_step, b_re, b_im, c_re, c_im):
    hp = lax.Precision.HIGHEST
    t_len = SSM_T
    g_n, p_n = SSM_GROUPS, SSM_STATE
    step = jnp.exp(log_step.astype(F32))[:, None]
    lr = lam_re.astype(F32)
    li = lam_im.astype(F32)
    mag = jnp.exp(lr * step)
    ar = mag * jnp.cos(li * step)
    ai = mag * jnp.sin(li * step)
    den = lr * lr + li * li
    fr = ((ar - 1.0) * lr + ai * li) / den
    fi = (ai * lr - (ar - 1.0) * li) / den
    br = b_re.astype(F32)
    bi = b_im.astype(F32)
    bbr = fr[..., None] * br - fi[..., None] * bi
    bbi = fr[..., None] * bi + fi[..., None] * br
    jj = jnp.arange(t_len + 1, dtype=F32)[None, None, :]
    pmag = jnp.exp((lr * step)[..., None] * jj)
    pang = (li * step)[..., None] * jj
    pw_r = pmag * jnp.cos(pang)
    pw_i = pmag * jnp.sin(pang)
    cr = c_re.astype(F32)
    ci = c_im.astype(F32)
    pr_t = jnp.transpose(pw_r, (0, 2, 1))[:, :, None, :]
    pi_t = jnp.transpose(pw_i, (0, 2, 1))[:, :, None, :]
    ca_r = cr[:, None] * pr_t - ci[:, None] * pi_t
    ca_i = cr[:, None] * pi_t + ci[:, None] * pr_t
    kmat = (jnp.einsum('gjcp,gpd->gjcd', ca_r[:, :t_len], bbr, precision=hp)
            - jnp.einsum('gjcp,gpd->gjcd', ca_i[:, :t_len], bbi, precision=hp))
    s_idx = np.arange(t_len)[None, :, None]
    t_idx = np.arange(t_len)[None, None, :]
    sel = (t_idx - s_idx == np.arange(t_len)[:, None, None]).astype(np.float32)
    toep = jnp.einsum('jst,gjcd->gsdtc', jnp.asarray(sel), kmat, precision=hp).reshape(g_n, SSM_TC, SSM_TC)
    d_r = jnp.transpose(ca_r[:, 1:], (0, 3, 1, 2)).reshape(g_n, p_n, SSM_TC)
    d_i = jnp.transpose(ca_i[:, 1:], (0, 3, 1, 2)).reshape(g_n, p_n, SSM_TC)
    rev_r = pw_r[:, :, t_len - 1::-1][:, :, :t_len]
    rev_i = pw_i[:, :, t_len - 1::-1][:, :, :t_len]
    e_r = rev_r[..., None] * bbr[:, :, None, :] - rev_i[..., None] * bbi[:, :, None, :]
    e_i = rev_r[..., None] * bbi[:, :, None, :] + rev_i[..., None] * bbr[:, :, None, :]
    e_r = jnp.transpose(e_r, (0, 2, 3, 1)).reshape(g_n, SSM_TC, p_n)
    e_i = jnp.transpose(e_i, (0, 2, 3, 1)).reshape(g_n, SSM_TC, p_n)
    z_in = jnp.zeros((SSM_PAIRS, SSM_TC, p_n), F32)
    e_r2 = e_r.reshape(SSM_PAIRS, 2, SSM_TC, p_n)
    e_i2 = e_i.reshape(SSM_PAIRS, 2, SSM_TC, p_n)
    min_top = jnp.concatenate([e_r2[:, 0], z_in, e_i2[:, 0], z_in], axis=-1)
    min_bot = jnp.concatenate([z_in, e_r2[:, 1], z_in, e_i2[:, 1]], axis=-1)
    m_in = jnp.concatenate([min_top, min_bot], axis=1)
    z_out = jnp.zeros((SSM_PAIRS, p_n, SSM_TC), F32)
    d_r2 = d_r.reshape(SSM_PAIRS, 2, p_n, SSM_TC)
    d_i2 = d_i.reshape(SSM_PAIRS, 2, p_n, SSM_TC)
    m_out = jnp.concatenate([
        jnp.concatenate([d_r2[:, 0], z_out], axis=-1),
        jnp.concatenate([z_out, d_r2[:, 1]], axis=-1),
        jnp.concatenate([-d_i2[:, 0], z_out], axis=-1),
        jnp.concatenate([z_out, -d_i2[:, 1]], axis=-1)], axis=1)
    at_r = pw_r[:, :, t_len].reshape(1, SSM_STATES)
    at_i = pw_i[:, :, t_len].reshape(1, SSM_STATES)
    return toep.astype(BF16), m_in.astype(BF16), m_out.astype(BF16), at_r, at_i


def _granule_transpose(src):
    x = list(src)
    grp = lax.broadcasted_iota(jnp.int32, x[0].shape, 1) // SSM_GROUP
    stride = SSM_OCT // 2
    while stride:
        upper = (grp & stride) != 0
        for a in range(SSM_OCT):
            if a & stride:
                continue
            lo, hi = x[a], x[a + stride]
            x[a] = jnp.where(upper, pltpu.roll(hi, stride * SSM_GROUP, axis=1), lo)
            x[a + stride] = jnp.where(upper, hi, pltpu.roll(lo, LANES - stride * SSM_GROUP, axis=1))
        stride //= 2
    return x


S5_RELAYOUT_ROWS = 64


def _s5_x_kernel(uv_ref, min_ref, u_ref, xr_ref, xi_ref):
    rb = S5_RELAYOUT_ROWS

    def body(r, c):
        rows = pl.ds(pl.multiple_of(r * rb, rb), rb)
        for half in range(2):
            src = [uv_ref[half * SSM_OCT + tp, rows, :].astype(F32) for tp in range(SSM_OCT)]
            out = _granule_transpose(src)
            for g in range(SSM_OCT):
                u_ref[g, rows, half * LANES:(half + 1) * LANES] = out[g].astype(BF16)
        return c

    lax.fori_loop(0, SSM_NC // rb, body, 0)
    for pr in range(SSM_OCT // 2):
        up = jnp.concatenate([u_ref[2 * pr], u_ref[2 * pr + 1]], axis=1)
        x = jnp.dot(up, min_ref[pr], preferred_element_type=F32)
        xr_ref[:, pr * LANES:(pr + 1) * LANES] = x[:, :LANES]
        xi_ref[:, pr * LANES:(pr + 1) * LANES] = x[:, LANES:]


def _s5_x(uv, m_in, layer):
    nc = SSM_NC
    ppo = SSM_OCT // 2
    sw = ppo * LANES
    return pl.pallas_call(
        _s5_x_kernel,
        out_shape=[jax.ShapeDtypeStruct((SSM_GROUPS, nc, SSM_TC), BF16),
                   jax.ShapeDtypeStruct((nc, SSM_STATES), F32),
                   jax.ShapeDtypeStruct((nc, SSM_STATES), F32)],
        grid=(SSM_NOCT,),
        in_specs=[
            pl.BlockSpec((SSM_T, nc, LANES), lambda k: (0, 0, k)),
            pl.BlockSpec((None, ppo, 2 * SSM_TC, 256), lambda k: (layer, k, 0, 0)),
        ],
        out_specs=[pl.BlockSpec((SSM_OCT, nc, SSM_TC), lambda k: (k, 0, 0)),
                   pl.BlockSpec((nc, sw), lambda k: (0, k)),
                   pl.BlockSpec((nc, sw), lambda k: (0, k))],
        compiler_params=_cparams(("parallel",)),
        name="s5_x",
    )(uv, m_in)


def _s5_scan_kernel(xr_ref, xi_ref, ar_ref, ai_ref, hr_ref, hi_ref, *, nc):
    ar = ar_ref[...]
    ai = ai_ref[...]

    def body(n, carry):
        hr, hi = carry
        row = pl.ds(n, 1)
        hr_ref[row, :] = hr
        hi_ref[row, :] = hi
        xr = xr_ref[row, :]
        xi = xi_ref[row, :]
        return (ar * hr - ai * hi + xr, ar * hi + ai * hr + xi)

    zero = jnp.zeros_like(ar)
    lax.fori_loop(0, nc, body, (zero, zero), unroll=8)


def _s5_scan(xr, xi, at_r, at_i, layer, lb=1024):
    nc, ns = xr.shape
    blk = pl.BlockSpec((nc, lb), lambda i: (0, i))
    tab = pl.BlockSpec((None, 1, lb), lambda i: (layer, 0, i))
    return pl.pallas_call(
        functools.partial(_s5_scan_kernel, nc=nc),
        out_shape=[jax.ShapeDtypeStruct((nc, ns), F32)] * 2,
        grid=(ns // lb,),
        in_specs=[blk, blk, tab, tab],
        out_specs=[blk, blk],
        compiler_params=_cparams(("parallel",)),
        name="s5_scan",
    )(xr, xi, at_r, at_i)


def _s5_y_kernel(u_ref, hr_ref, hi_ref, toep_ref, mout_ref, yv_ref, y_scr):
    for pr in range(SSM_OCT // 2):
        hcat = jnp.concatenate([hr_ref[:, pr * LANES:(pr + 1) * LANES],
                                hi_ref[:, pr * LANES:(pr + 1) * LANES]], axis=1).astype(BF16)
        carry = jnp.dot(hcat, mout_ref[pr], preferred_element_type=F32)
        for k in range(2):
            g = 2 * pr + k
            y = jnp.dot(u_ref[g], toep_ref[g], preferred_element_type=F32)
            y_scr[g] = y + carry[:, k * SSM_TC:(k + 1) * SSM_TC]

    rb = S5_RELAYOUT_ROWS

    def body(r, c):
        rows = pl.ds(pl.multiple_of(r * rb, rb), rb)
        for half in range(2):
            src = [y_scr[g, rows, half * LANES:(half + 1) * LANES] for g in range(SSM_OCT)]
            out = _granule_transpose(src)
            for tp in range(SSM_OCT):
                yv_ref[half * SSM_OCT + tp, rows, :] = out[tp].astype(BF16)
        return c

    lax.fori_loop(0, SSM_NC // rb, body, 0)


def _s5_y(u_g, hr, hi, toep, m_out, layer):
    nc = SSM_NC
    ppo = SSM_OCT // 2
    hblk = pl.BlockSpec((nc, ppo * LANES), lambda k: (0, k))
    return pl.pallas_call(
        _s5_y_kernel,
        out_shape=jax.ShapeDtypeStruct((SSM_T, nc, SSM_WIDTH), BF16),
        grid=(SSM_NOCT,),
        in_specs=[
            pl.BlockSpec((SSM_OCT, nc, SSM_TC), lambda k: (k, 0, 0)),
            hblk, hblk,
            pl.BlockSpec((None, SSM_OCT, SSM_TC, SSM_TC), lambda k: (layer, k, 0, 0)),
            pl.BlockSpec((None, ppo, 256, 2 * SSM_TC), lambda k: (layer, k, 0, 0)),
        ],
        out_specs=pl.BlockSpec((SSM_T, nc, LANES), lambda k: (0, 0, k)),
        scratch_shapes=[pltpu.VMEM((SSM_OCT, nc, SSM_TC), F32)],
        compiler_params=_cparams(("parallel",)),
        name="s5_y",
    )(u_g, hr, hi, toep, m_out)


def _s5_glu_kernel(yv_ref, uv_ref, d_ref, w_ref, b_ref, o_ref, z_scr, zb_scr, g_scr, nat_scr, *, ncb):
    d = d_ref[...]
    for t in range(SSM_T):
        z = jax.nn.gelu(yv_ref[t].astype(F32) + d * uv_ref[t].astype(F32))
        z_scr[t * ncb:(t + 1) * ncb, :] = z
        zb_scr[t * ncb:(t + 1) * ncb, :] = z.astype(BF16)
    g_scr[...] = jnp.dot(zb_scr[...], w_ref[...], preferred_element_type=F32)
    b = b_ref[...]
    nt = SSM_WIDTH // LANES
    for t in range(SSM_T):
        rows = slice(t * ncb, (t + 1) * ncb)
        out = z_scr[rows, :] * jax.nn.sigmoid(g_scr[rows, :] + b)
        for k in range(nt):
            nat_scr[k, pl.ds(t, ncb, stride=SSM_T), :] = out[:, k * LANES:(k + 1) * LANES]
    for k in range(nt):
        o_ref[:, k * LANES:(k + 1) * LANES] = nat_scr[k].astype(o_ref.dtype)


def _s5_glu(yv, uv, d_skip, w_glu, b_glu, layer, ncb=64):
    w = SSM_WIDTH
    tm = ncb * SSM_T
    vblk = pl.BlockSpec((SSM_T, ncb, w), lambda i: (0, i, 0))
    vec = pl.BlockSpec((None, 1, w), lambda i: (layer, 0, 0))
    return pl.pallas_call(
        functools.partial(_s5_glu_kernel, ncb=ncb),
        out_shape=jax.ShapeDtypeStruct((SEQ, w), BF16),
        grid=(SSM_NC // ncb,),
        in_specs=[vblk, vblk, vec,
                  pl.BlockSpec((None, w, w), lambda i: (layer, 0, 0)),
                  vec],
        out_specs=pl.BlockSpec((tm, w), lambda i: (i, 0)),
        scratch_shapes=[pltpu.VMEM((tm, w), F32), pltpu.VMEM((tm, w), BF16), pltpu.VMEM((tm, w), F32),
                        pltpu.VMEM((w // LANES, tm, LANES), F32)],
        compiler_params=_cparams(("parallel",)),
        name="s5_glu",
    )(yv, uv, d_skip, w_glu, b_glu)


def _s5_branch(uv, prep, d_skip, w_glu, b_glu, layer):
    toep, m_in, m_out, at_r, at_i = prep
    u_g, xr, xi = _s5_x(uv, m_in, layer)
    hr, hi = _s5_scan(xr, xi, at_r, at_i, layer)
    yv = _s5_y(u_g, hr, hi, toep, m_out, layer)
    return _s5_glu(yv, uv, d_skip, w_glu, b_glu, layer)


def _rope(x, cos, sin_signed):
    w = x.shape[1]
    reps = w // LANES
    cosw = jnp.concatenate([cos] * reps, axis=1) if reps > 1 else cos
    sinw = jnp.concatenate([sin_signed] * reps, axis=1) if reps > 1 else sin_signed
    lane = lax.broadcasted_iota(jnp.int32, x.shape, 1) % HEAD_DIM
    swapped = jnp.where(lane < HEAD_DIM // 2,
                        pltpu.roll(x, w - HEAD_DIM // 2, axis=1),
                        pltpu.roll(x, HEAD_DIM // 2, axis=1))
    return x * cosw + swapped * sinw


def _swa_kernel(sink_ref, q_ref, kc_ref, kp_ref, vc_ref, vp_ref, cc_ref, sc_ref, cp_ref, sp_ref, o_ref,
                s_scr, p_scr):
    n = pl.program_id(0)
    blk = ATT_BLOCK
    hpt = LANES // HEAD_DIM
    cos_c = cc_ref[...]
    sin_c = sc_ref[...]
    kc = _rope(kc_ref[...].astype(F32), cos_c, sin_c)
    kp = _rope(kp_ref[...].astype(F32), cp_ref[...], sp_ref[...])
    k_all = jnp.concatenate([kp, kc], axis=0)
    v_all = jnp.concatenate([vp_ref[...], vc_ref[...]], axis=0).astype(F32)
    lo = lax.broadcasted_iota(jnp.int32, (2 * blk, LANES), 1) < HEAD_DIM
    ones = jnp.ones((2 * blk, LANES), BF16)

    def lane_padded(tile, half_in):
        other = pltpu.roll(tile, HEAD_DIM, axis=1)
        at_lo, at_hi = (tile, other) if half_in == 0 else (other, tile)
        return jnp.where(lo, at_lo, 0.0).astype(BF16), jnp.where(lo, 0.0, at_hi).astype(BF16)

    row = lax.broadcasted_iota(jnp.int32, (blk, blk), 0)
    col = lax.broadcasted_iota(jnp.int32, (blk, blk), 1)
    cur = col <= row
    sink_slot = (col == 0).astype(F32)
    key_row = lax.broadcasted_iota(jnp.int32, (2 * blk, LANES), 0)
    prev_bias = jnp.where(n > 0, 0.0, NEG_BIG)
    scale = HEAD_DIM ** -0.5
    tiles_per_kv = ATT_GROUP // hpt

    for tile in range(ATT_HEADS // hpt):
        h = tile // tiles_per_kv
        if tile % tiles_per_kv == 0:
            k_pad = lane_padded(k_all[:, (h // hpt) * LANES:(h // hpt + 1) * LANES], h % hpt)
        lanes = slice(tile * LANES, (tile + 1) * LANES)
        qt = _rope(q_ref[:, lanes].astype(F32) * scale, cos_c, sin_c).astype(BF16)
        for e in range(hpt):
            s_scr[tile * hpt + e] = lax.dot_general(qt, k_pad[e], (((1,), (1,)), ((), ())),
                                                    preferred_element_type=F32)

    for hd in range(ATT_HEADS):
        sk = sink_ref[hd]
        s = jnp.where(cur, s_scr[hd, :, blk:], s_scr[hd, :, :blk] + prev_bias)
        m = jnp.maximum(jnp.max(s, axis=-1, keepdims=True), sk)
        p = jnp.exp(s - m)
        p_scr[hd, :, :blk] = jnp.where(cur, jnp.exp(sk - m) * sink_slot, p).astype(BF16)
        p_scr[hd, :, blk:] = jnp.where(cur, p, 0.0).astype(BF16)

    for tile in range(ATT_HEADS // hpt):
        h = tile // tiles_per_kv
        if tile % tiles_per_kv == 0:
            vt = v_all[:, (h // hpt) * LANES:(h // hpt + 1) * LANES]
            vt = jnp.where(key_row == 0, 0.0, vt)
            v_ext = [jnp.concatenate([vp, ones], axis=1) for vp in lane_padded(vt, h % hpt)]
        acc = None
        for e in range(hpt):
            oe = jnp.dot(p_scr[tile * hpt + e], v_ext[e], preferred_element_type=F32)
            term = oe[:, :LANES] / oe[:, LANES:]
            acc = term if acc is None else acc + term
        o_ref[:, tile * LANES:(tile + 1) * LANES] = acc.astype(o_ref.dtype)


def _swa(proj, sinks, cos_t, sin_t):
    s = proj.shape[0]
    blk = ATT_BLOCK
    nb = s // blk
    prev = lambda n: jnp.maximum(n - 1, 0)
    tab_c = pl.BlockSpec((blk, LANES), lambda n: (n, 0))
    tab_p = pl.BlockSpec((blk, LANES), lambda n: (prev(n), 0))
    return pl.pallas_call(
        _swa_kernel,
        out_shape=jax.ShapeDtypeStruct((s, ATT_Q_WIDTH), BF16),
        grid=(nb,),
        in_specs=[
            pl.BlockSpec(memory_space=pltpu.SMEM),
            pl.BlockSpec((blk, ATT_Q_WIDTH), lambda n: (n, COL_AQ // ATT_Q_WIDTH)),
            pl.BlockSpec((blk, ATT_KV_WIDTH), lambda n: (n, COL_AK // ATT_KV_WIDTH)),
            pl.BlockSpec((blk, ATT_KV_WIDTH), lambda n: (prev(n), COL_AK // ATT_KV_WIDTH)),
            pl.BlockSpec((blk, ATT_KV_WIDTH), lambda n: (n, COL_AV // ATT_KV_WIDTH)),
            pl.BlockSpec((blk, ATT_KV_WIDTH), lambda n: (prev(n), COL_AV // ATT_KV_WIDTH)),
            tab_c, tab_c, tab_p, tab_p,
        ],
        out_specs=pl.BlockSpec((blk, ATT_Q_WIDTH), lambda n: (n, 0)),
        scratch_shapes=[pltpu.VMEM((ATT_HEADS, blk, 2 * blk), F32), pltpu.VMEM((ATT_HEADS, blk, 2 * blk), BF16)],
        compiler_params=_cparams(("parallel",)),
        name="swa",
    )(sinks, proj, proj, proj, proj, proj, cos_t, sin_t, cos_t, sin_t)


def _log_sigmoid(x):
    return jnp.minimum(x, 0.0) - jnp.log(1.0 + jnp.exp(-jnp.abs(x)))


def _gla_kernel(q_ref, k_ref, v_ref, glr_ref, gout_ref, wg_ref, bg_ref, ng_ref, tri_ref, o_ref,
                st_ref, qin_scr, att_scr, upd_scr, sb_scr, *, rb):
    @pl.when(pl.program_id(0) == 0)
    def _():
        st_ref[...] = jnp.zeros_like(st_ref)

    ch = GLA_CHUNK
    ncb = rb // ch
    nh = GLA_HEADS
    logit = jnp.dot(glr_ref[...], wg_ref[...], preferred_element_type=F32) + bg_ref[...]
    log_a = _log_sigmoid(logit) * (1.0 / GLA_TAU)
    tri = tri_ref[...]
    a_hi = log_a.astype(BF16)
    a_lo = (log_a - a_hi.astype(F32)).astype(BF16)
    bcum = (jnp.dot(tri, a_hi, preferred_element_type=F32)
            + jnp.dot(tri, a_lo, preferred_element_type=F32))
    qin_scr[...] = (q_ref[...].astype(F32) * (GLA_DK ** -0.5) * jnp.exp(bcum)).astype(BF16)
    k = k_ref[...].astype(F32)
    k_in = (k * jnp.exp(-bcum)).astype(BF16)
    causal = (lax.broadcasted_iota(jnp.int32, (ch, ch), 1)
              <= lax.broadcasted_iota(jnp.int32, (ch, ch), 0))

    decays = []
    for c in range(ncb):
        rows = slice(c * ch, (c + 1) * ch)
        b_last = bcum[(c + 1) * ch - 1:(c + 1) * ch, :]
        k_dec = (k[rows] * jnp.exp(b_last - bcum[rows])).astype(BF16)
        decays.append(jnp.exp(b_last))
        for h in range(nh):
            ks = slice(h * GLA_DK, (h + 1) * GLA_DK)
            att = lax.dot_general(qin_scr[rows, ks], k_in[rows, ks], (((1,), (1,)), ((), ())),
                                  preferred_element_type=F32)
            att_scr[c * nh + h] = jnp.where(causal, att, 0.0).astype(BF16)
            upd_scr[c * nh + h] = lax.dot_general(v_ref[rows, h * GLA_DV:(h + 1) * GLA_DV], k_dec[:, ks],
                                                  (((0,), (0,)), ((), ())), preferred_element_type=F32)

    for h in range(nh):
        ks = slice(h * GLA_DK, (h + 1) * GLA_DK)
        st = st_ref[h]
        for c in range(ncb):
            sb_scr[c * nh + h] = st.astype(BF16)
            st = st * decays[c][:, ks] + upd_scr[c * nh + h]
        st_ref[h] = st

    for c in range(ncb):
        rows = slice(c * ch, (c + 1) * ch)
        for h in range(nh):
            ks = slice(h * GLA_DK, (h + 1) * GLA_DK)
            vs = slice(h * GLA_DV, (h + 1) * GLA_DV)
            o = jnp.dot(att_scr[c * nh + h], v_ref[rows, vs], preferred_element_type=F32)
            o = o + lax.dot_general(qin_scr[rows, ks], sb_scr[c * nh + h], (((1,), (1,)), ((), ())),
                                    preferred_element_type=F32)
            o = o * lax.rsqrt(jnp.mean(o * o, axis=-1, keepdims=True) + EPS)
            o = o * ng_ref[:, vs]
            go = gout_ref[rows, vs].astype(F32)
            o = o * (go * jax.nn.sigmoid(go))
            o_ref[rows, vs] = o.astype(o_ref.dtype)


def _gla(proj, w_gate, b_gate, norm_g, layer, rb=256):
    s = proj.shape[0]
    ch = GLA_CHUNK
    idx = np.arange(rb)
    tri = jnp.asarray((idx[None, :] <= idx[:, None]) & (idx[None, :] // ch == idx[:, None] // ch), BF16)
    nblk = (rb // ch) * GLA_HEADS
    return pl.pallas_call(
        functools.partial(_gla_kernel, rb=rb),
        out_shape=jax.ShapeDtypeStruct((s, GLA_V_WIDTH), BF16),
        grid=(s // rb,),
        in_specs=[
            pl.BlockSpec((rb, GLA_QK_WIDTH), lambda i: (i, COL_GQ // GLA_QK_WIDTH)),
            pl.BlockSpec((rb, GLA_QK_WIDTH), lambda i: (i, COL_GK // GLA_QK_WIDTH)),
            pl.BlockSpec((rb, GLA_V_WIDTH), lambda i: (i, COL_GV // GLA_V_WIDTH)),
            pl.BlockSpec((rb, LANES), lambda i: (i, COL_GLR // LANES)),
            pl.BlockSpec((rb, GLA_V_WIDTH), lambda i: (i, COL_GOUT // GLA_V_WIDTH)),
            pl.BlockSpec((None, LANES, GLA_QK_WIDTH), lambda i: (layer, 0, 0)),
            pl.BlockSpec((None, 1, GLA_QK_WIDTH), lambda i: (layer, 0, 0)),
            pl.BlockSpec((None, 1, GLA_V_WIDTH), lambda i: (layer, 0, 0)),
            pl.BlockSpec((rb, rb), lambda i: (0, 0)),
        ],
        out_specs=pl.BlockSpec((rb, GLA_V_WIDTH), lambda i: (i, 0)),
        scratch_shapes=[pltpu.VMEM((GLA_HEADS, GLA_DV, GLA_DK), F32),
                        pltpu.VMEM((rb, GLA_QK_WIDTH), BF16),
                        pltpu.VMEM((nblk, ch, ch), BF16),
                        pltpu.VMEM((nblk, GLA_DV, GLA_DK), F32),
                        pltpu.VMEM((nblk, GLA_DV, GLA_DK), BF16)],
        compiler_params=_cparams(("arbitrary",)),
        name="gla",
    )(proj, proj, proj, proj, proj, w_gate, b_gate, norm_g, tri)


MIX_TN = 512


def _mix_kernel(x_ref, ys_ref, ya_ref, yg_ref, gate_ref, ws_ref, wa_ref, wg_ref, wo_ref, o_ref, m_scr):
    d = D_MODEL
    for j in range(d // MIX_TN):
        cols = slice(j * MIX_TN, (j + 1) * MIX_TN)
        m = None
        for k, (y_ref, w_ref) in enumerate(((ys_ref, ws_ref), (ya_ref, wa_ref), (yg_ref, wg_ref))):
            proj = jnp.dot(y_ref[...], w_ref[:, cols], preferred_element_type=F32)
            gate = gate_ref[:, k * d + j * MIX_TN:k * d + (j + 1) * MIX_TN].astype(F32)
            term = jax.nn.sigmoid(gate) * proj
            m = term if m is None else m + term
        m_scr[:, cols] = m.astype(BF16)
    o_ref[...] = x_ref[...] + jnp.dot(m_scr[...], wo_ref[...], preferred_element_type=F32)


def _mix(x, proj, y_ssm, y_att, y_gla, w_bs, w_ba, w_bg, w_out, layer, tm=256):
    s, d = x.shape
    bw = y_ssm.shape[1]
    yblk = pl.BlockSpec((tm, bw), lambda i: (i, 0))
    wblk = pl.BlockSpec((None, bw, d), lambda i: (layer, 0, 0), pipeline_mode=pl.Buffered(1))
    return pl.pallas_call(
        _mix_kernel,
        out_shape=jax.ShapeDtypeStruct((s, d), F32),
        grid=(s // tm,),
        in_specs=[
            pl.BlockSpec((tm, d), lambda i: (i, 0)),
            yblk, yblk, yblk,
            pl.BlockSpec((tm, 3 * d), lambda i: (i, COL_MERGE // (3 * d))),
            wblk, wblk, wblk,
            pl.BlockSpec((None, d, d), lambda i: (layer, 0, 0), pipeline_mode=pl.Buffered(1)),
        ],
        out_specs=pl.BlockSpec((tm, d), lambda i: (i, 0)),
        scratch_shapes=[pltpu.VMEM((tm, d), BF16)],
        compiler_params=_cparams(("parallel",)),
        name="mix",
    )(x, y_ssm, y_att, y_gla, proj, w_bs, w_ba, w_bg, w_out)


def _ffn_kernel(x_ref, g_ref, w1_ref, w2_ref, fg_ref, o_ref, h_ref, *, tm, nf, final_norm):
    f = pl.program_id(1)

    @pl.when(f == 0)
    def _():
        _rms_rows(x_ref, g_ref, h_ref, tm)
        o_ref[...] = x_ref[...]

    a = jnp.dot(h_ref[...], w1_ref[...], preferred_element_type=F32)
    a = jnp.square(jnp.maximum(a, 0.0)).astype(BF16)
    o_ref[...] += jnp.dot(a, w2_ref[...], preferred_element_type=F32)

    if final_norm:
        @pl.when(f == nf - 1)
        def _():
            _rms_rows(o_ref, fg_ref, o_ref, tm)


def _ffn(x, g, w1, w2, fg, layer, final_norm, tm=1024, tf=512):
    s, d = x.shape
    dff = w1.shape[2]
    nf = dff // tf
    return pl.pallas_call(
        functools.partial(_ffn_kernel, tm=tm, nf=nf, final_norm=final_norm),
        out_shape=jax.ShapeDtypeStruct((s, d), F32),
        grid=(s // tm, nf),
        in_specs=[
            pl.BlockSpec((tm, d), lambda i, f: (i, 0)),
            pl.BlockSpec((None, 1, d), lambda i, f: (layer, 0, 0)),
            pl.BlockSpec((None, d, tf), lambda i, f: (layer, 0, f)),
            pl.BlockSpec((None, tf, d), lambda i, f: (layer, f, 0)),
            pl.BlockSpec((1, d), lambda i, f: (0, 0)),
        ],
        out_specs=pl.BlockSpec((tm, d), lambda i, f: (i, 0)),
        scratch_shapes=[pltpu.VMEM((tm, d), BF16)],
        compiler_params=_cparams(("parallel", "arbitrary")),
        name="ffn",
    )(x, g, w1, w2, fg)


PACK_TN = 512
PACK_COPY, PACK_SHIFT, PACK_GLR = 0, 1, 2


def _pack_tables():
    sizes = [SSM_WIDTH, ATT_Q_WIDTH, ATT_KV_WIDTH, ATT_KV_WIDTH, GLA_QK_WIDTH, GLA_QK_WIDTH,
             GLA_V_WIDTH, GLA_GATE_RANK, GLA_V_WIDTH, 3 * D_MODEL]
    o_u, o_aq, o_ak, o_av, o_gq, o_gk, o_gv, o_glr, o_gout, o_merge = np.concatenate([[0], np.cumsum(sizes)])[:-1]
    assert o_av == o_ak + ATT_KV_WIDTH
    pieces = [(o_u, SSM_WIDTH), (o_merge, 3 * D_MODEL), (o_aq, ATT_Q_WIDTH), (o_gv, GLA_V_WIDTH),
              (o_gout, GLA_V_WIDTH), (o_gq, GLA_QK_WIDTH), (o_gk, GLA_QK_WIDTH), (o_ak, 2 * ATT_KV_WIDTH)]
    a_idx, b_idx, mode = [], [], []
    for src, width in pieces:
        shift = src % PACK_TN
        assert shift in (0, GLA_GATE_RANK) and width % PACK_TN == 0
        for k in range(width // PACK_TN):
            a = (src - shift) // PACK_TN + k
            a_idx.append(a)
            b_idx.append((a + 1) * (PACK_TN // LANES))
            mode.append(PACK_SHIFT if shift else PACK_COPY)
    assert o_glr % PACK_TN == 0
    a_idx.append(o_glr // PACK_TN)
    b_idx.append((o_glr // PACK_TN + 1) * (PACK_TN // LANES))
    mode.append(PACK_GLR)
    assert len(mode) * PACK_TN == INPROJ_TN + PROJ_WIDTH
    return [jnp.asarray(np.array(t, np.int32)) for t in (a_idx, b_idx, mode)]


def _pack_kernel(a_idx, b_idx, mode, a_ref, b_ref, o_ref, *, rows, chunk=256):
    md = mode[pl.program_id(1)]

    def for_chunks(fn):
        def body(r, c):
            sl = pl.ds(pl.multiple_of(r * chunk, chunk), chunk)
            o_ref[sl, :] = fn(sl).astype(o_ref.dtype)
            return c
        lax.fori_loop(0, rows // chunk, body, 0)

    @pl.when(md == PACK_COPY)
    def _():
        for_chunks(lambda sl: a_ref[sl, :])

    @pl.when(md == PACK_SHIFT)
    def _():
        def shifted(sl):
            full = jnp.concatenate([a_ref[sl, :], b_ref[sl, :]], axis=1)
            return pltpu.roll(full, full.shape[1] - GLA_GATE_RANK, axis=1)[:, :PACK_TN]
        for_chunks(shifted)

    @pl.when(md == PACK_GLR)
    def _():
        lane = lax.broadcasted_iota(jnp.int32, (chunk, PACK_TN), 1)
        for_chunks(lambda sl: jnp.where(lane < GLA_GATE_RANK, a_ref[sl, :], 0.0))


def _pack_w_in(w):
    nl, d, _ = w.shape
    a_idx, b_idx, mode = _pack_tables()
    nt = (INPROJ_TN + PROJ_WIDTH) // PACK_TN
    return pl.pallas_call(
        functools.partial(_pack_kernel, rows=d),
        out_shape=jax.ShapeDtypeStruct((nl, d, nt * PACK_TN), BF16),
        grid_spec=pltpu.PrefetchScalarGridSpec(
            num_scalar_prefetch=3,
            grid=(nl, nt),
            in_specs=[
                pl.BlockSpec((None, d, PACK_TN), lambda l, j, a, b, m: (l, 0, a[j])),
                pl.BlockSpec((None, d, LANES), lambda l, j, a, b, m: (l, 0, b[j])),
            ],
            out_specs=pl.BlockSpec((None, d, PACK_TN), lambda l, j, a, b, m: (l, 0, j)),
        ),
        compiler_params=_cparams(("parallel", "parallel")),
        name="pack_w_in",
    )(a_idx, b_idx, mode, w, w)


def _rope_tables():
    half = HEAD_DIM // 2
    inv = ROPE_THETA ** (-jnp.arange(half, dtype=F32) / half)
    ang = jnp.arange(SEQ, dtype=jnp.int32).astype(F32)[:, None] * inv[None, :]
    cos = jnp.cos(ang)
    sin = jnp.sin(ang)
    cos_t = jnp.concatenate([cos, cos, cos, cos], axis=1)
    sin_t = jnp.concatenate([-sin, sin, -sin, sin], axis=1)
    return cos_t, sin_t


def kernel(x, norm1_g, w_in, ssm_lam_re, ssm_lam_im, ssm_log_step, ssm_b_re, ssm_b_im, ssm_c_re, ssm_c_im, ssm_d, ssm_w_glu, ssm_b_glu, att_sinks, gla_w_gate, gla_b_gate, gla_norm_g, w_branch_ssm, w_branch_att, w_branch_gla, w_out, norm2_g, w_ff1, w_ff2, final_norm_g):
    assert x.shape == (1, SEQ, D_MODEL)
    nl = DEPTH
    xs = x.reshape(SEQ, D_MODEL).astype(F32)
    cos_t, sin_t = _rope_tables()
    fg = final_norm_g.reshape(1, D_MODEL).astype(F32)
    w_in_p = _pack_w_in(w_in)
    n1 = norm1_g.reshape(nl, 1, D_MODEL).astype(F32)
    n2 = norm2_g.reshape(nl, 1, D_MODEL).astype(F32)
    prep = jax.vmap(_s5_prepare)(ssm_lam_re, ssm_lam_im, ssm_log_step, ssm_b_re, ssm_b_im, ssm_c_re, ssm_c_im)
    d_skip = ssm_d.reshape(nl, 1, SSM_WIDTH).astype(F32)
    w_glu = ssm_w_glu.astype(BF16)
    b_glu = ssm_b_glu.reshape(nl, 1, SSM_WIDTH).astype(F32)
    sinks = att_sinks.astype(F32)
    wg_pad = jnp.concatenate(
        [gla_w_gate.astype(BF16), jnp.zeros((nl, LANES - GLA_GATE_RANK, GLA_QK_WIDTH), BF16)], axis=1)
    bg = gla_b_gate.reshape(nl, 1, GLA_QK_WIDTH).astype(F32)
    ng = gla_norm_g.reshape(nl, 1, GLA_V_WIDTH).astype(F32)
    w_bs = w_branch_ssm.astype(BF16)
    w_ba = w_branch_att.astype(BF16)
    w_bg = w_branch_gla.astype(BF16)
    w_o = w_out.astype(BF16)
    w1 = w_ff1.astype(BF16)
    w2 = w_ff2.astype(BF16)
    for l in range(nl):
        proj, uv = _inproj(xs, n1, w_in_p, l)
        y_ssm = _s5_branch(uv, prep, d_skip, w_glu, b_glu, l)
        y_att = _swa(proj, sinks[l], cos_t, sin_t)
        y_gla = _gla(proj, wg_pad, bg, ng, l)
        xs = _mix(xs, proj, y_ssm, y_att, y_gla, w_bs, w_ba, w_bg, w_o, l)
        xs = _ffn(xs, n2, w1, w2, fg, l, final_norm=(l == nl - 1))
    return xs.reshape(1, SEQ, D_MODEL)
```

```python
import functools

import jax
import jax.numpy as jnp
import numpy as np
from jax import lax
from jax.experimental import pallas as pl
from jax.experimental.pallas import tpu as pltpu

F32 = jnp.float32
BF16 = jnp.bfloat16

D_MODEL = 2048
SEQ = 16384
DEPTH = 2
SSM_WIDTH = 1024
SSM_GROUP = 16
SSM_GROUPS = 64
SSM_STATE = 64
HEAD_DIM = 64
ATT_HEADS = 16
ATT_KV_HEADS = 4
ATT_GROUP = 4
ATT_Q_WIDTH = 1024
ATT_KV_WIDTH = 256
WINDOW = 128
ATT_BLOCK = 128
ROPE_THETA = 10000.0
GLA_HEADS = 4
GLA_V_WIDTH = 1024
GLA_DV = 256
GLA_DK = 128
GLA_QK_WIDTH = 512
GLA_GATE_RANK = 16
GLA_TAU = 16.0
GLA_CHUNK = 64
D_FF = 8192
EPS = 1e-6

LANES = 128
VMEM_LIMIT = 56 * 1024 * 1024

COL_MERGE = 0
COL_AQ = 6144
COL_GV = 7168
COL_GOUT = 8192
COL_GQ = 9216
COL_GK = 9728
COL_AK = 10240
COL_AV = 10496
COL_GLR = 10752
GLR_PAD = 512
PROJ_WIDTH = 11264
INPROJ_TN = 1024

SSM_T = 16
SSM_NC = SEQ // SSM_T
SSM_TC = SSM_T * SSM_GROUP
SSM_PAIRS = SSM_GROUPS // 2
SSM_STATES = SSM_GROUPS * SSM_STATE
SSM_OCT = LANES // SSM_GROUP
SSM_NOCT = SSM_GROUPS // SSM_OCT
NEG_BIG = -1e30


def _cparams(sem, vmem=VMEM_LIMIT):
    return pltpu.CompilerParams(dimension_semantics=sem, vmem_limit_bytes=vmem)


def _rms_rows(x_ref, g_ref, dst_ref, rows, chunk=128):
    g = g_ref[...]

    def body(r, c):
        sl = pl.ds(pl.multiple_of(r * chunk, chunk), chunk)
        xv = x_ref[sl, :].astype(F32)
        ms = jnp.mean(xv * xv, axis=-1, keepdims=True)
        dst_ref[sl, :] = (xv * lax.rsqrt(ms + EPS) * g).astype(dst_ref.dtype)
        return c

    lax.fori_loop(0, rows // chunk, body, 0)


def _inproj_kernel(x_ref, g_ref, w_ref, o_ref, uv_ref, h_ref, scr_ref, *, tm):
    j = pl.program_id(1)

    def project():
        return lax.dot_general(h_ref[...], w_ref[...], (((1,), (1,)), ((), ())), preferred_element_type=F32)

    @pl.when(j == 0)
    def _():
        _rms_rows(x_ref, g_ref, h_ref, tm)
        res = project()
        nt = res.shape[1] // LANES
        for k in range(nt):
            scr_ref[k] = res[:, k * LANES:(k + 1) * LANES]
        for t in range(SSM_T):
            for k in range(nt):
                uv_ref[t, :, k * LANES:(k + 1) * LANES] = scr_ref[
                    k, pl.ds(t, tm // SSM_T, stride=SSM_T), :].astype(uv_ref.dtype)

    @pl.when(j > 0)
    def _():
        o_ref[...] = project().astype(o_ref.dtype)


def _inproj(x, g, w, layer, tm=1024):
    s, d = x.shape
    tn = INPROJ_TN
    n = w.shape[1]
    return pl.pallas_call(
        functools.partial(_inproj_kernel, tm=tm),
        out_shape=[jax.ShapeDtypeStruct((s, n - tn), BF16),
                   jax.ShapeDtypeStruct((SSM_T, s // SSM_T, SSM_WIDTH), BF16)],
        grid=(s // tm, n // tn),
        in_specs=[
            pl.BlockSpec((tm, d), lambda i, j: (i, 0)),
            pl.BlockSpec((None, 1, d), lambda i, j: (layer, 0, 0)),
            pl.BlockSpec((None, tn, d), lambda i, j: (layer, j, 0)),
        ],
        out_specs=[pl.BlockSpec((tm, tn), lambda i, j: (i, jnp.maximum(j - 1, 0))),
                   pl.BlockSpec((SSM_T, tm // SSM_T, SSM_WIDTH), lambda i, j: (0, i, 0))],
        scratch_shapes=[pltpu.VMEM((tm, d), BF16), pltpu.VMEM((tn // LANES, tm, LANES), F32)],
        compiler_params=_cparams(("parallel", "arbitrary")),
        name="inproj",
    )(x, g, w)


def _s5_prepare(lam_re, lam_im, log_step, b_re, b_im, c_re, c_im):
    hp = lax.Precision.HIGHEST
    t_len = SSM_T
    g_n, p_n = SSM_GROUPS, SSM_STATE
    step = jnp.exp(log_step.astype(F32))[:, None]
    lr = lam_re.astype(F32)
    li = lam_im.astype(F32)
    mag = jnp.exp(lr * step)
    ar = mag * jnp.cos(li * step)
    ai = mag * jnp.sin(li * step)
    den = lr * lr + li * li
    fr = ((ar - 1.0) * lr + ai * li) / den
    fi = (ai * lr - (ar - 1.0) * li) / den
    br = b_re.astype(F32)
    bi = b_im.astype(F32)
    bbr = fr[..., None] * br - fi[..., None] * bi
    bbi = fr[..., None] * bi + fi[..., None] * br
    jj = jnp.arange(t_len + 1, dtype=F32)[None, None, :]
    pmag = jnp.exp((lr * step)[..., None] * jj)
    pang = (li * step)[..., None] * jj
    pw_r = pmag * jnp.cos(pang)
    pw_i = pmag * jnp.sin(pang)
    cr = c_re.astype(F32)
    ci = c_im.astype(F32)
    pr_t = jnp.transpose(pw_r, (0, 2, 1))[:, :, None, :]
    pi_t = jnp.transpose(pw_i, (0, 2, 1))[:, :, None, :]
    ca_r = cr[:, None] * pr_t - ci[:, None] * pi_t
    ca_i = cr[:, None] * pi_t + ci[:, None] * pr_t
    kmat = (jnp.einsum('gjcp,gpd->gjcd', ca_r[:, :t_len], bbr, precision=hp)
            - jnp.einsum('gjcp,gpd->gjcd', ca_i[:, :t_len], bbi, precision=hp))
    s_idx = np.arange(t_len)[None, :, None]
    t_idx = np.arange(t_len)[None, None, :]
    sel = (t_idx - s_idx == np.arange(t_len)[:, None, None]).astype(np.float32)
    toep = jnp.einsum('jst,gjcd->gsdtc', jnp.asarray(sel), kmat, precision=hp).reshape(g_n, SSM_TC, SSM_TC)
    d_r = jnp.transpose(ca_r[:, 1:], (0, 3, 1, 2)).reshape(g_n, p_n, SSM_TC)
    d_i = jnp.transpose(ca_i[:, 1:], (0, 3, 1, 2)).reshape(g_n, p_n, SSM_TC)
    rev_r = pw_r[:, :, t_len - 1::-1][:, :, :t_len]
    rev_i = pw_i[:, :, t_len - 1::-1][:, :, :t_len]
    e_r = rev_r[..., None] * bbr[:, :, None, :] - rev_i[..., None] * bbi[:, :, None, :]
    e_i = rev_r[..., None] * bbi[:, :, None, :] + rev_i[..., None] * bbr[:, :, None, :]
    e_r = jnp.transpose(e_r, (0, 2, 3, 1)).reshape(g_n, SSM_TC, p_n)
    e_i = jnp.transpose(e_i, (0, 2, 3, 1)).reshape(g_n, SSM_TC, p_n)
    z_in = jnp.zeros((SSM_PAIRS, SSM_TC, p_n), F32)
    e_r2 = e_r.reshape(SSM_PAIRS, 2, SSM_TC, p_n)
    e_i2 = e_i.reshape(SSM_PAIRS, 2, SSM_TC, p_n)
    min_top = jnp.concatenate([e_r2[:, 0], z_in, e_i2[:, 0], z_in], axis=-1)
    min_bot = jnp.concatenate([z_in, e_r2[:, 1], z_in, e_i2[:, 1]], axis=-1)
    m_in = jnp.concatenate([min_top, min_bot], axis=1)
    z_out = jnp.zeros((SSM_PAIRS, p_n, SSM_TC), F32)
    d_r2 = d_r.reshape(SSM_PAIRS, 2, p_n, SSM_TC)
    d_i2 = d_i.reshape(SSM_PAIRS, 2, p_n, SSM_TC)
    m_out = jnp.concatenate([
        jnp.concatenate([d_r2[:, 0], z_out], axis=-1),
        jnp.concatenate([z_out, d_r2[:, 1]], axis=-1),
        jnp.concatenate([-d_i2[:, 0], z_out], axis=-1),
        jnp.concatenate([z_out, -d_i2[:, 1]], axis=-1)], axis=1)
    at_r = pw_r[:, :, t_len].reshape(1, SSM_STATES)
    at_i = pw_i[:, :, t_len].reshape(1, SSM_STATES)
    return toep.astype(BF16), m_in.astype(BF16), m_out.astype(BF16), at_r, at_i


def _granule_transpose(src):
    x = list(src)
    grp = lax.broadcasted_iota(jnp.int32, x[0].shape, 1) // SSM_GROUP
    stride = SSM_OCT // 2
    while stride:
        upper = (grp & stride) != 0
        for a in range(SSM_OCT):
            if a & stride:
                continue
            lo, hi = x[a], x[a + stride]
            x[a] = jnp.where(upper, pltpu.roll(hi, stride * SSM_GROUP, axis=1), lo)
            x[a + stride] = jnp.where(upper, hi, pltpu.roll(lo, LANES - stride * SSM_GROUP, axis=1))
        stride //= 2
    return x


S5_RELAYOUT_ROWS = 64
S5_RELAYOUT_UNROLL = 1


def _s5_x_kernel(uv_ref, min_ref, u_ref, xr_ref, xi_ref):
    rb = S5_RELAYOUT_ROWS

    def body(r, c):
        rows = pl.ds(pl.multiple_of(r * rb, rb), rb)
        for half in range(2):
            src = [uv_ref[half * SSM_OCT + tp, rows, :].astype(F32) for tp in range(SSM_OCT)]
            out = _granule_transpose(src)
            for g in range(SSM_OCT):
                u_ref[g, rows, half * LANES:(half + 1) * LANES] = out[g].astype(BF16)
        return c

    lax.fori_loop(0, SSM_NC // rb, body, 0, unroll=S5_RELAYOUT_UNROLL)
    for pr in range(SSM_OCT // 2):
        up = jnp.concatenate([u_ref[2 * pr], u_ref[2 * pr + 1]], axis=1)
        x = jnp.dot(up, min_ref[pr], preferred_element_type=F32)
        xr_ref[:, pr * LANES:(pr + 1) * LANES] = x[:, :LANES]
        xi_ref[:, pr * LANES:(pr + 1) * LANES] = x[:, LANES:]


def _s5_x(uv, m_in, layer):
    nc = SSM_NC
    ppo = SSM_OCT // 2
    sw = ppo * LANES
    return pl.pallas_call(
        _s5_x_kernel,
        out_shape=[jax.ShapeDtypeStruct((SSM_GROUPS, nc, SSM_TC), BF16),
                   jax.ShapeDtypeStruct((nc, SSM_STATES), F32),
                   jax.ShapeDtypeStruct((nc, SSM_STATES), F32)],
        grid=(SSM_NOCT,),
        in_specs=[
            pl.BlockSpec((SSM_T, nc, LANES), lambda k: (0, 0, k)),
            pl.BlockSpec((None, ppo, 2 * SSM_TC, 256), lambda k: (layer, k, 0, 0)),
        ],
        out_specs=[pl.BlockSpec((SSM_OCT, nc, SSM_TC), lambda k: (k, 0, 0)),
                   pl.BlockSpec((nc, sw), lambda k: (0, k)),
                   pl.BlockSpec((nc, sw), lambda k: (0, k))],
        compiler_params=_cparams(("parallel",)),
        name="s5_x",
    )(uv, m_in)


def _s5_scan_kernel(xr_ref, xi_ref, ar_ref, ai_ref, hr_ref, hi_ref, *, nc):
    ar = ar_ref[...]
    ai = ai_ref[...]

    def body(n, carry):
        hr, hi = carry
        row = pl.ds(n, 1)
        hr_ref[row, :] = hr
        hi_ref[row, :] = hi
        xr = xr_ref[row, :]
        xi = xi_ref[row, :]
        return (ar * hr - ai * hi + xr, ar * hi + ai * hr + xi)

    zero = jnp.zeros_like(ar)
    lax.fori_loop(0, nc, body, (zero, zero), unroll=8)


def _s5_scan(xr, xi, at_r, at_i, layer, lb=1024):
    nc, ns = xr.shape
    blk = pl.BlockSpec((nc, lb), lambda i: (0, i))
    tab = pl.BlockSpec((None, 1, lb), lambda i: (layer, 0, i))
    return pl.pallas_call(
        functools.partial(_s5_scan_kernel, nc=nc),
        out_shape=[jax.ShapeDtypeStruct((nc, ns), F32)] * 2,
        grid=(ns // lb,),
        in_specs=[blk, blk, tab, tab],
        out_specs=[blk, blk],
        compiler_params=_cparams(("parallel",)),
        name="s5_scan",
    )(xr, xi, at_r, at_i)


def _s5_y_kernel(u_ref, hr_ref, hi_ref, toep_ref, mout_ref, yv_ref, y_scr):
    for pr in range(SSM_OCT // 2):
        hcat = jnp.concatenate([hr_ref[:, pr * LANES:(pr + 1) * LANES],
                                hi_ref[:, pr * LANES:(pr + 1) * LANES]], axis=1).astype(BF16)
        carry = jnp.dot(hcat, mout_ref[pr], preferred_element_type=F32)
        for k in range(2):
            g = 2 * pr + k
            y = jnp.dot(u_ref[g], toep_ref[g], preferred_element_type=F32)
            y_scr[g] = y + carry[:, k * SSM_TC:(k + 1) * SSM_TC]

    rb = S5_RELAYOUT_ROWS

    def body(r, c):
        rows = pl.ds(pl.multiple_of(r * rb, rb), rb)
        for half in range(2):
            src = [y_scr[g, rows, half * LANES:(half + 1) * LANES] for g in range(SSM_OCT)]
            out = _granule_transpose(src)
            for tp in range(SSM_OCT):
                yv_ref[half * SSM_OCT + tp, rows, :] = out[tp].astype(BF16)
        return c

    lax.fori_loop(0, SSM_NC // rb, body, 0, unroll=S5_RELAYOUT_UNROLL)


def _s5_y(u_g, hr, hi, toep, m_out, layer):
    nc = SSM_NC
    ppo = SSM_OCT // 2
    hblk = pl.BlockSpec((nc, ppo * LANES), lambda k: (0, k))
    return pl.pallas_call(
        _s5_y_kernel,
        out_shape=jax.ShapeDtypeStruct((SSM_T, nc, SSM_WIDTH), BF16),
        grid=(SSM_NOCT,),
        in_specs=[
            pl.BlockSpec((SSM_OCT, nc, SSM_TC), lambda k: (k, 0, 0)),
            hblk, hblk,
            pl.BlockSpec((None, SSM_OCT, SSM_TC, SSM_TC), lambda k: (layer, k, 0, 0)),
            pl.BlockSpec((None, ppo, 256, 2 * SSM_TC), lambda k: (layer, k, 0, 0)),
        ],
        out_specs=pl.BlockSpec((SSM_T, nc, LANES), lambda k: (0, 0, k)),
        scratch_shapes=[pltpu.VMEM((SSM_OCT, nc, SSM_TC), F32)],
        compiler_params=_cparams(("parallel",)),
        name="s5_y",
    )(u_g, hr, hi, toep, m_out)


def _s5_glu_kernel(yv_ref, uv_ref, d_ref, w_ref, b_ref, o_ref, z_scr, zb_scr, g_scr, nat_scr, *, ncb):
    d = d_ref[...]
    for t in range(SSM_T):
        z = jax.nn.gelu(yv_ref[t].astype(F32) + d * uv_ref[t].astype(F32))
        z_scr[t * ncb:(t + 1) * ncb, :] = z
        zb_scr[t * ncb:(t + 1) * ncb, :] = z.astype(BF16)
    g_scr[...] = jnp.dot(zb_scr[...], w_ref[...], preferred_element_type=F32)
    b = b_ref[...]
    nt = SSM_WIDTH // LANES
    for t in range(SSM_T):
        rows = slice(t * ncb, (t + 1) * ncb)
        out = z_scr[rows, :] * jax.nn.sigmoid(g_scr[rows, :] + b)
        for k in range(nt):
            nat_scr[k, pl.ds(t, ncb, stride=SSM_T), :] = out[:, k * LANES:(k + 1) * LANES]
    for k in range(nt):
        o_ref[:, k * LANES:(k + 1) * LANES] = nat_scr[k].astype(o_ref.dtype)


def _s5_glu(yv, uv, d_skip, w_glu, b_glu, layer, ncb=64):
    w = SSM_WIDTH
    tm = ncb * SSM_T
    vblk = pl.BlockSpec((SSM_T, ncb, w), lambda i: (0, i, 0))
    vec = pl.BlockSpec((None, 1, w), lambda i: (layer, 0, 0))
    return pl.pallas_call(
        functools.partial(_s5_glu_kernel, ncb=ncb),
        out_shape=jax.ShapeDtypeStruct((SEQ, w), BF16),
        grid=(SSM_NC // ncb,),
        in_specs=[vblk, vblk, vec,
                  pl.BlockSpec((None, w, w), lambda i: (layer, 0, 0)),
                  vec],
        out_specs=pl.BlockSpec((tm, w), lambda i: (i, 0)),
        scratch_shapes=[pltpu.VMEM((tm, w), F32), pltpu.VMEM((tm, w), BF16), pltpu.VMEM((tm, w), F32),
                        pltpu.VMEM((w // LANES, tm, LANES), F32)],
        compiler_params=_cparams(("parallel",)),
        name="s5_glu",
    )(yv, uv, d_skip, w_glu, b_glu)


def _s5_branch(uv, prep, d_skip, w_glu, b_glu, layer):
    toep, m_in, m_out, at_r, at_i = prep
    u_g, xr, xi = _s5_x(uv, m_in, layer)
    hr, hi = _s5_scan(xr, xi, at_r, at_i, layer)
    yv = _s5_y(u_g, hr, hi, toep, m_out, layer)
    return _s5_glu(yv, uv, d_skip, w_glu, b_glu, layer)


def _rope(x, cos, sin_signed):
    w = x.shape[1]
    reps = w // LANES
    cosw = jnp.concatenate([cos] * reps, axis=1) if reps > 1 else cos
    sinw = jnp.concatenate([sin_signed] * reps, axis=1) if reps > 1 else sin_signed
    lane = lax.broadcasted_iota(jnp.int32, x.shape, 1) % HEAD_DIM
    swapped = jnp.where(lane < HEAD_DIM // 2,
                        pltpu.roll(x, w - HEAD_DIM // 2, axis=1),
                        pltpu.roll(x, HEAD_DIM // 2, axis=1))
    return x * cosw + swapped * sinw


def _swa_kernel(sink_ref, q_ref, kc_ref, kp_ref, vc_ref, vp_ref, cc_ref, sc_ref, cp_ref, sp_ref, o_ref,
                s_scr, p_scr):
    n = pl.program_id(0)
    blk = ATT_BLOCK
    hpt = LANES // HEAD_DIM
    cos_c = cc_ref[...]
    sin_c = sc_ref[...]
    kc = _rope(kc_ref[...].astype(F32), cos_c, sin_c)
    kp = _rope(kp_ref[...].astype(F32), cp_ref[...], sp_ref[...])
    k_all = jnp.concatenate([kp, kc], axis=0)
    v_all = jnp.concatenate([vp_ref[...], vc_ref[...]], axis=0).astype(F32)
    lo = lax.broadcasted_iota(jnp.int32, (2 * blk, LANES), 1) < HEAD_DIM
    ones = jnp.ones((2 * blk, LANES), BF16)

    def lane_padded(tile, half_in):
        other = pltpu.roll(tile, HEAD_DIM, axis=1)
        at_lo, at_hi = (tile, other) if half_in == 0 else (other, tile)
        return jnp.where(lo, at_lo, 0.0).astype(BF16), jnp.where(lo, 0.0, at_hi).astype(BF16)

    row = lax.broadcasted_iota(jnp.int32, (blk, blk), 0)
    col = lax.broadcasted_iota(jnp.int32, (blk, blk), 1)
    cur = col <= row
    sink_slot = (col == 0).astype(F32)
    key_row = lax.broadcasted_iota(jnp.int32, (2 * blk, LANES), 0)
    prev_bias = jnp.where(n > 0, 0.0, NEG_BIG)
    scale = HEAD_DIM ** -0.5
    tiles_per_kv = ATT_GROUP // hpt

    for tile in range(ATT_HEADS // hpt):
        h = tile // tiles_per_kv
        if tile % tiles_per_kv == 0:
            k_pad = lane_padded(k_all[:, (h // hpt) * LANES:(h // hpt + 1) * LANES], h % hpt)
        lanes = slice(tile * LANES, (tile + 1) * LANES)
        qt = _rope(q_ref[:, lanes].astype(F32) * scale, cos_c, sin_c).astype(BF16)
        for e in range(hpt):
            s_scr[tile * hpt + e] = lax.dot_general(qt, k_pad[e], (((1,), (1,)), ((), ())),
                                                    preferred_element_type=F32)

    for hd in range(ATT_HEADS):
        sk = sink_ref[hd]
        s = jnp.where(cur, s_scr[hd, :, blk:], s_scr[hd, :, :blk] + prev_bias)
        m = jnp.maximum(jnp.max(s, axis=-1, keepdims=True), sk)
        p = jnp.exp(s - m)
        p_scr[hd, :, :blk] = jnp.where(cur, jnp.exp(sk - m) * sink_slot, p).astype(BF16)
        p_scr[hd, :, blk:] = jnp.where(cur, p, 0.0).astype(BF16)

    for tile in range(ATT_HEADS // hpt):
        h = tile // tiles_per_kv
        if tile % tiles_per_kv == 0:
            vt = v_all[:, (h // hpt) * LANES:(h // hpt + 1) * LANES]
            vt = jnp.where(key_row == 0, 0.0, vt)
            v_ext = [jnp.concatenate([vp, ones], axis=1) for vp in lane_padded(vt, h % hpt)]
        acc = None
        for e in range(hpt):
            oe = jnp.dot(p_scr[tile * hpt + e], v_ext[e], preferred_element_type=F32)
            term = oe[:, :LANES] / oe[:, LANES:]
            acc = term if acc is None else acc + term
        o_ref[:, tile * LANES:(tile + 1) * LANES] = acc.astype(o_ref.dtype)


def _swa(proj, sinks, cos_t, sin_t):
    s = proj.shape[0]
    blk = ATT_BLOCK
    nb = s // blk
    prev = lambda n: jnp.maximum(n - 1, 0)
    tab_c = pl.BlockSpec((blk, LANES), lambda n: (n, 0))
    tab_p = pl.BlockSpec((blk, LANES), lambda n: (prev(n), 0))
    return pl.pallas_call(
        _swa_kernel,
        out_shape=jax.ShapeDtypeStruct((s, ATT_Q_WIDTH), BF16),
        grid=(nb,),
        in_specs=[
            pl.BlockSpec(memory_space=pltpu.SMEM),
            pl.BlockSpec((blk, ATT_Q_WIDTH), lambda n: (n, COL_AQ // ATT_Q_WIDTH)),
            pl.BlockSpec((blk, ATT_KV_WIDTH), lambda n: (n, COL_AK // ATT_KV_WIDTH)),
            pl.BlockSpec((blk, ATT_KV_WIDTH), lambda n: (prev(n), COL_AK // ATT_KV_WIDTH)),
            pl.BlockSpec((blk, ATT_KV_WIDTH), lambda n: (n, COL_AV // ATT_KV_WIDTH)),
            pl.BlockSpec((blk, ATT_KV_WIDTH), lambda n: (prev(n), COL_AV // ATT_KV_WIDTH)),
            tab_c, tab_c, tab_p, tab_p,
        ],
        out_specs=pl.BlockSpec((blk, ATT_Q_WIDTH), lambda n: (n, 0)),
        scratch_shapes=[pltpu.VMEM((ATT_HEADS, blk, 2 * blk), F32), pltpu.VMEM((ATT_HEADS, blk, 2 * blk), BF16)],
        compiler_params=_cparams(("parallel",)),
        name="swa",
    )(sinks, proj, proj, proj, proj, proj, cos_t, sin_t, cos_t, sin_t)


def _log_sigmoid(x):
    return jnp.minimum(x, 0.0) - jnp.log(1.0 + jnp.exp(-jnp.abs(x)))


def _gla_kernel(q_ref, k_ref, v_ref, glr_ref, gout_ref, wg_ref, bg_ref, ng_ref, tri_ref, o_ref,
                st_ref, qin_scr, att_scr, upd_scr, sb_scr, *, rb):
    @pl.when(pl.program_id(0) == 0)
    def _():
        st_ref[...] = jnp.zeros_like(st_ref)

    ch = GLA_CHUNK
    ncb = rb // ch
    nh = GLA_HEADS
    logit = jnp.dot(glr_ref[...], wg_ref[...], preferred_element_type=F32) + bg_ref[...]
    log_a = _log_sigmoid(logit) * (1.0 / GLA_TAU)
    tri = tri_ref[...]
    a_hi = log_a.astype(BF16)
    a_lo = (log_a - a_hi.astype(F32)).astype(BF16)
    bcum = (jnp.dot(tri, a_hi, preferred_element_type=F32)
            + jnp.dot(tri, a_lo, preferred_element_type=F32))
    qin_scr[...] = (q_ref[...].astype(F32) * (GLA_DK ** -0.5) * jnp.exp(bcum)).astype(BF16)
    k = k_ref[...].astype(F32)
    k_in = (k * jnp.exp(-bcum)).astype(BF16)
    causal = (lax.broadcasted_iota(jnp.int32, (ch, ch), 1)
              <= lax.broadcasted_iota(jnp.int32, (ch, ch), 0))

    decays = []
    for c in range(ncb):
        rows = slice(c * ch, (c + 1) * ch)
        b_last = bcum[(c + 1) * ch - 1:(c + 1) * ch, :]
        k_dec = (k[rows] * jnp.exp(b_last - bcum[rows])).astype(BF16)
        decays.append(jnp.exp(b_last))
        for h in range(nh):
            ks = slice(h * GLA_DK, (h + 1) * GLA_DK)
            att = lax.dot_general(qin_scr[rows, ks], k_in[rows, ks], (((1,), (1,)), ((), ())),
                                  preferred_element_type=F32)
            att_scr[c * nh + h] = jnp.where(causal, att, 0.0).astype(BF16)
            upd_scr[c * nh + h] = lax.dot_general(v_ref[rows, h * GLA_DV:(h + 1) * GLA_DV], k_dec[:, ks],
                                                  (((0,), (0,)), ((), ())), preferred_element_type=F32)

    for h in range(nh):
        ks = slice(h * GLA_DK, (h + 1) * GLA_DK)
        st = st_ref[h]
        for c in range(ncb):
            sb_scr[c * nh + h] = st.astype(BF16)
            st = st * decays[c][:, ks] + upd_scr[c * nh + h]
        st_ref[h] = st

    for c in range(ncb):
        rows = slice(c * ch, (c + 1) * ch)
        for h in range(nh):
            ks = slice(h * GLA_DK, (h + 1) * GLA_DK)
            vs = slice(h * GLA_DV, (h + 1) * GLA_DV)
            o = jnp.dot(att_scr[c * nh + h], v_ref[rows, vs], preferred_element_type=F32)
            o = o + lax.dot_general(qin_scr[rows, ks], sb_scr[c * nh + h], (((1,), (1,)), ((), ())),
                                    preferred_element_type=F32)
            o = o * lax.rsqrt(jnp.mean(o * o, axis=-1, keepdims=True) + EPS)
            o = o * ng_ref[:, vs]
            go = gout_ref[rows, vs].astype(F32)
            o = o * (go * jax.nn.sigmoid(go))
            o_ref[rows, vs] = o.astype(o_ref.dtype)


def _gla(proj, w_gate, b_gate, norm_g, layer, rb=256):
    s = proj.shape[0]
    ch = GLA_CHUNK
    idx = np.arange(rb)
    tri = jnp.asarray((idx[None, :] <= idx[:, None]) & (idx[None, :] // ch == idx[:, None] // ch), BF16)
    nblk = (rb // ch) * GLA_HEADS
    return pl.pallas_call(
        functools.partial(_gla_kernel, rb=rb),
        out_shape=jax.ShapeDtypeStruct((s, GLA_V_WIDTH), BF16),
        grid=(s // rb,),
        in_specs=[
            pl.BlockSpec((rb, GLA_QK_WIDTH), lambda i: (i, COL_GQ // GLA_QK_WIDTH)),
            pl.BlockSpec((rb, GLA_QK_WIDTH), lambda i: (i, COL_GK // GLA_QK_WIDTH)),
            pl.BlockSpec((rb, GLA_V_WIDTH), lambda i: (i, COL_GV // GLA_V_WIDTH)),
            pl.BlockSpec((rb, LANES), lambda i: (i, COL_GLR // LANES)),
            pl.BlockSpec((rb, GLA_V_WIDTH), lambda i: (i, COL_GOUT // GLA_V_WIDTH)),
            pl.BlockSpec((None, LANES, GLA_QK_WIDTH), lambda i: (layer, 0, 0)),
            pl.BlockSpec((None, 1, GLA_QK_WIDTH), lambda i: (layer, 0, 0)),
            pl.BlockSpec((None, 1, GLA_V_WIDTH), lambda i: (layer, 0, 0)),
            pl.BlockSpec((rb, rb), lambda i: (0, 0)),
        ],
        out_specs=pl.BlockSpec((rb, GLA_V_WIDTH), lambda i: (i, 0)),
        scratch_shapes=[pltpu.VMEM((GLA_HEADS, GLA_DV, GLA_DK), F32),
                        pltpu.VMEM((rb, GLA_QK_WIDTH), BF16),
                        pltpu.VMEM((nblk, ch, ch), BF16),
                        pltpu.VMEM((nblk, GLA_DV, GLA_DK), F32),
                        pltpu.VMEM((nblk, GLA_DV, GLA_DK), BF16)],
        compiler_params=_cparams(("arbitrary",)),
        name="gla",
    )(proj, proj, proj, proj, proj, w_gate, b_gate, norm_g, tri)


MIX_TN = 512


def _mix_kernel(x_ref, ys_ref, ya_ref, yg_ref, gate_ref, ws_ref, wa_ref, wg_ref, wo_ref, o_ref, m_scr):
    d = D_MODEL
    for j in range(d // MIX_TN):
        cols = slice(j * MIX_TN, (j + 1) * MIX_TN)
        m = None
        for k, (y_ref, w_ref) in enumerate(((ys_ref, ws_ref), (ya_ref, wa_ref), (yg_ref, wg_ref))):
            proj = jnp.dot(y_ref[...], w_ref[:, cols], preferred_element_type=F32)
            gate = gate_ref[:, k * d + j * MIX_TN:k * d + (j + 1) * MIX_TN].astype(F32)
            term = jax.nn.sigmoid(gate) * proj
            m = term if m is None else m + term
        m_scr[:, cols] = m.astype(BF16)
    o_ref[...] = x_ref[...] + jnp.dot(m_scr[...], wo_ref[...], preferred_element_type=F32)


def _mix(x, proj, y_ssm, y_att, y_gla, w_bs, w_ba, w_bg, w_out, layer, tm=256):
    s, d = x.shape
    bw = y_ssm.shape[1]
    yblk = pl.BlockSpec((tm, bw), lambda i: (i, 0))
    wblk = pl.BlockSpec((None, bw, d), lambda i: (layer, 0, 0), pipeline_mode=pl.Buffered(1))
    return pl.pallas_call(
        _mix_kernel,
        out_shape=jax.ShapeDtypeStruct((s, d), F32),
        grid=(s // tm,),
        in_specs=[
            pl.BlockSpec((tm, d), lambda i: (i, 0)),
            yblk, yblk, yblk,
            pl.BlockSpec((tm, 3 * d), lambda i: (i, COL_MERGE // (3 * d))),
            wblk, wblk, wblk,
            pl.BlockSpec((None, d, d), lambda i: (layer, 0, 0), pipeline_mode=pl.Buffered(1)),
        ],
        out_specs=pl.BlockSpec((tm, d), lambda i: (i, 0)),
        scratch_shapes=[pltpu.VMEM((tm, d), BF16)],
        compiler_params=_cparams(("parallel",)),
        name="mix",
    )(x, y_ssm, y_att, y_gla, proj, w_bs, w_ba, w_bg, w_out)


def _ffn_kernel(x_ref, g_ref, w1_ref, w2_ref, fg_ref, o_ref, h_ref, *, tm, nf, final_norm):
    f = pl.program_id(1)

    @pl.when(f == 0)
    def _():
        _rms_rows(x_ref, g_ref, h_ref, tm)
        o_ref[...] = x_ref[...]

    a = jnp.dot(h_ref[...], w1_ref[...], preferred_element_type=F32)
    a = jnp.square(jnp.maximum(a, 0.0)).astype(BF16)
    o_ref[...] += jnp.dot(a, w2_ref[...], preferred_element_type=F32)

    if final_norm:
        @pl.when(f == nf - 1)
        def _():
            _rms_rows(o_ref, fg_ref, o_ref, tm)


def _ffn(x, g, w1, w2, fg, layer, final_norm, tm=1024, tf=512):
    s, d = x.shape
    dff = w1.shape[2]
    nf = dff // tf
    return pl.pallas_call(
        functools.partial(_ffn_kernel, tm=tm, nf=nf, final_norm=final_norm),
        out_shape=jax.ShapeDtypeStruct((s, d), F32),
        grid=(s // tm, nf),
        in_specs=[
            pl.BlockSpec((tm, d), lambda i, f: (i, 0)),
            pl.BlockSpec((None, 1, d), lambda i, f: (layer, 0, 0)),
            pl.BlockSpec((None, d, tf), lambda i, f: (layer, 0, f)),
            pl.BlockSpec((None, tf, d), lambda i, f: (layer, f, 0)),
            pl.BlockSpec((1, d), lambda i, f: (0, 0)),
        ],
        out_specs=pl.BlockSpec((tm, d), lambda i, f: (i, 0)),
        scratch_shapes=[pltpu.VMEM((tm, d), BF16)],
        compiler_params=_cparams(("parallel", "arbitrary")),
        name="ffn",
    )(x, g, w1, w2, fg)


PACK_TN = 512
PACK_ALIGN = 16
PACK_COPY, PACK_GLR = 0, 1


def _pack_tables():
    sizes = [SSM_WIDTH, ATT_Q_WIDTH, ATT_KV_WIDTH, ATT_KV_WIDTH, GLA_QK_WIDTH, GLA_QK_WIDTH,
             GLA_V_WIDTH, GLA_GATE_RANK, GLA_V_WIDTH, 3 * D_MODEL]
    o_u, o_aq, o_ak, o_av, o_gq, o_gk, o_gv, o_glr, o_gout, o_merge = np.concatenate([[0], np.cumsum(sizes)])[:-1]
    assert o_av == o_ak + ATT_KV_WIDTH
    pieces = [(o_u, SSM_WIDTH), (o_merge, 3 * D_MODEL), (o_aq, ATT_Q_WIDTH), (o_gv, GLA_V_WIDTH),
              (o_gout, GLA_V_WIDTH), (o_gq, GLA_QK_WIDTH), (o_gk, GLA_QK_WIDTH), (o_ak, 2 * ATT_KV_WIDTH)]
    src, mode = [], []
    for start, width in pieces:
        assert start % PACK_ALIGN == 0 and width % PACK_TN == 0
        for k in range(width // PACK_TN):
            src.append((start + k * PACK_TN) // PACK_ALIGN)
            mode.append(PACK_COPY)
    assert o_glr % PACK_ALIGN == 0 and o_glr + PACK_TN <= sum(sizes)
    src.append(o_glr // PACK_ALIGN)
    mode.append(PACK_GLR)
    assert len(mode) * PACK_TN == INPROJ_TN + PROJ_WIDTH
    return [jnp.asarray(np.array(t, np.int32)) for t in (src, mode)]


def _pack_kernel(src, mode, a_ref, o_ref, *, chunk=128):
    md = mode[pl.program_id(1)]

    def for_chunks(fn):
        def body(r, c):
            sl = pl.ds(pl.multiple_of(r * chunk, chunk), chunk)
            o_ref[sl, :] = fn(r, a_ref[0, sl, :]).astype(o_ref.dtype)
            return c
        lax.fori_loop(0, PACK_TN // chunk, body, 0)

    @pl.when(md == PACK_COPY)
    def _():
        for_chunks(lambda r, a: a)

    @pl.when(md == PACK_GLR)
    def _():
        row = lax.broadcasted_iota(jnp.int32, (chunk, a_ref.shape[2]), 0)
        for_chunks(lambda r, a: jnp.where(row + r * chunk < GLA_GATE_RANK, a, 0.0))


def _pack_w_in(w):
    nl, d, _ = w.shape
    wt = jnp.swapaxes(w, 1, 2)
    src, mode = _pack_tables()
    nt = (INPROJ_TN + PROJ_WIDTH) // PACK_TN
    return pl.pallas_call(
        _pack_kernel,
        out_shape=jax.ShapeDtypeStruct((nl, nt * PACK_TN, d), BF16),
        grid_spec=pltpu.PrefetchScalarGridSpec(
            num_scalar_prefetch=2,
            grid=(nl, nt),
            in_specs=[
                pl.BlockSpec((pl.Element(1), pl.Element(PACK_TN), pl.Element(d)),
                             lambda l, j, s, m: (l, pl.multiple_of(s[j] * PACK_ALIGN, PACK_ALIGN), 0)),
            ],
            out_specs=pl.BlockSpec((None, PACK_TN, d), lambda l, j, s, m: (l, j, 0)),
        ),
        compiler_params=_cparams(("parallel", "parallel")),
        name="pack_w_in",
    )(src, mode, wt)


def _rope_tables():
    half = HEAD_DIM // 2
    inv = np.float32(ROPE_THETA) ** (-np.arange(half, dtype=np.float32) / np.float32(half))
    ang = (np.arange(SEQ, dtype=np.float32)[:, None] * inv[None, :].astype(np.float32)).astype(np.float32)
    cos = np.cos(ang.astype(np.float64)).astype(np.float32)
    sin = np.sin(ang.astype(np.float64)).astype(np.float32)
    cos_t = np.concatenate([cos, cos, cos, cos], axis=1)
    sin_t = np.concatenate([-sin, sin, -sin, sin], axis=1)
    return jnp.asarray(cos_t), jnp.asarray(sin_t)


def kernel(x, norm1_g, w_in, ssm_lam_re, ssm_lam_im, ssm_log_step, ssm_b_re, ssm_b_im, ssm_c_re, ssm_c_im, ssm_d, ssm_w_glu, ssm_b_glu, att_sinks, gla_w_gate, gla_b_gate, gla_norm_g, w_branch_ssm, w_branch_att, w_branch_gla, w_out, norm2_g, w_ff1, w_ff2, final_norm_g):
    assert x.shape == (1, SEQ, D_MODEL)
    nl = DEPTH
    xs = x.reshape(SEQ, D_MODEL).astype(F32)
    cos_t, sin_t = _rope_tables()
    fg = final_norm_g.reshape(1, D_MODEL).astype(F32)
    w_in_p = _pack_w_in(w_in)
    n1 = norm1_g.reshape(nl, 1, D_MODEL).astype(F32)
    n2 = norm2_g.reshape(nl, 1, D_MODEL).astype(F32)
    prep = jax.vmap(_s5_prepare)(ssm_lam_re, ssm_lam_im, ssm_log_step, ssm_b_re, ssm_b_im, ssm_c_re, ssm_c_im)
    d_skip = ssm_d.reshape(nl, 1, SSM_WIDTH).astype(F32)
    w_glu = ssm_w_glu.astype(BF16)
    b_glu = ssm_b_glu.reshape(nl, 1, SSM_WIDTH).astype(F32)
    sinks = att_sinks.astype(F32)
    wg_pad = jnp.concatenate(
        [gla_w_gate.astype(BF16), jnp.zeros((nl, LANES - GLA_GATE_RANK, GLA_QK_WIDTH), BF16)], axis=1)
    bg = gla_b_gate.reshape(nl, 1, GLA_QK_WIDTH).astype(F32)
    ng = gla_norm_g.reshape(nl, 1, GLA_V_WIDTH).astype(F32)
    w_bs = w_branch_ssm.astype(BF16)
    w_ba = w_branch_att.astype(BF16)
    w_bg = w_branch_gla.astype(BF16)
    w_o = w_out.astype(BF16)
    w1 = w_ff1.astype(BF16)
    w2 = w_ff2.astype(BF16)
    for l in range(nl):
        proj, uv = _inproj(xs, n1, w_in_p, l)
        y_ssm = _s5_branch(uv, prep, d_skip, w_glu, b_glu, l)
        y_att = _swa(proj, sinks[l], cos_t, sin_t)
        y_gla = _gla(proj, wg_pad, bg, ng, l)
        xs = _mix(xs, proj, y_ssm, y_att, y_gla, w_bs, w_ba, w_bg, w_o, l)
        xs = _ffn(xs, n2, w1, w2, fg, l, final_norm=(l == nl - 1))
    return xs.reshape(1, SEQ, D_MODEL)
```

```python
import functools
import math

import jax
import jax.numpy as jnp
import numpy as np
from jax import lax
from jax.experimental import pallas as pl
from jax.experimental.pallas import tpu as pltpu

F32 = jnp.float32
BF16 = jnp.bfloat16

D_MODEL = 2048
SEQ = 16384
DEPTH = 2
SSM_WIDTH = 1024
SSM_GROUP = 16
SSM_GROUPS = 64
SSM_STATE = 64
HEAD_DIM = 64
ATT_HEADS = 16
ATT_KV_HEADS = 4
ATT_GROUP = 4
ATT_Q_WIDTH = 1024
ATT_KV_WIDTH = 256
WINDOW = 128
ATT_BLOCK = 128
ROPE_THETA = 10000.0
GLA_HEADS = 4
GLA_V_WIDTH = 1024
GLA_DV = 256
GLA_DK = 128
GLA_QK_WIDTH = 512
GLA_GATE_RANK = 16
GLA_TAU = 16.0
GLA_CHUNK = 64
D_FF = 8192
EPS = 1e-6

LANES = 128
VMEM_LIMIT = 56 * 1024 * 1024

COL_MERGE = 0
COL_AQ = 6144
COL_GV = 7168
COL_GOUT = 8192
COL_GQ = 9216
COL_GK = 9728
COL_AK = 10240
COL_AV = 10496
COL_GLR = 10752
GLR_PAD = 512
PROJ_WIDTH = 11264
INPROJ_TN = 1024

SSM_T = 16
SSM_NC = SEQ // SSM_T
SSM_TC = SSM_T * SSM_GROUP
SSM_PAIRS = SSM_GROUPS // 2
SSM_STATES = SSM_GROUPS * SSM_STATE
SSM_OCT = LANES // SSM_GROUP
SSM_NOCT = SSM_GROUPS // SSM_OCT
NEG_BIG = -1e30
LOG2E = math.log2(math.e)
ATT_Q_SCALE = HEAD_DIM ** -0.5 * LOG2E


def _cparams(sem, vmem=VMEM_LIMIT):
    return pltpu.CompilerParams(dimension_semantics=sem, vmem_limit_bytes=vmem)


def _rms_rows(x_ref, g_ref, dst_ref, rows, chunk=128):
    g = g_ref[...]

    def body(r, c):
        sl = pl.ds(pl.multiple_of(r * chunk, chunk), chunk)
        xv = x_ref[sl, :].astype(F32)
        ms = jnp.mean(xv * xv, axis=-1, keepdims=True)
        dst_ref[sl, :] = (xv * lax.rsqrt(ms + EPS) * g).astype(dst_ref.dtype)
        return c

    lax.fori_loop(0, rows // chunk, body, 0)


def _rope(x, cos, sin_signed):
    lane = lax.broadcasted_iota(jnp.int32, x.shape, 1) % HEAD_DIM
    swapped = jnp.where(lane < HEAD_DIM // 2,
                        pltpu.roll(x, LANES - HEAD_DIM // 2, axis=1),
                        pltpu.roll(x, HEAD_DIM // 2, axis=1))
    return x * cos + swapped * sin_signed


def _inproj_kernel(x_ref, g_ref, w_ref, o_ref, uv_ref, h_ref, scr_ref, *, tm):
    j = pl.program_id(1)

    def project():
        return lax.dot_general(h_ref[...], w_ref[...], (((1,), (1,)), ((), ())), preferred_element_type=F32)

    @pl.when(j == 0)
    def _():
        _rms_rows(x_ref, g_ref, h_ref, tm)
        res = project()
        nt = res.shape[1] // LANES
        for k in range(nt):
            scr_ref[k] = res[:, k * LANES:(k + 1) * LANES]
        for t in range(SSM_T):
            for k in range(nt):
                uv_ref[t, :, k * LANES:(k + 1) * LANES] = scr_ref[
                    k, pl.ds(t, tm // SSM_T, stride=SSM_T), :].astype(uv_ref.dtype)

    @pl.when(j > 0)
    def _():
        o_ref[...] = project().astype(o_ref.dtype)


def _inproj(x, g, w, layer, tm=1024):
    s, d = x.shape
    tn = INPROJ_TN
    n = w.shape[1]
    return pl.pallas_call(
        functools.partial(_inproj_kernel, tm=tm),
        out_shape=[jax.ShapeDtypeStruct((s, n - tn), BF16),
                   jax.ShapeDtypeStruct((SSM_T, s // SSM_T, SSM_WIDTH), BF16)],
        grid=(s // tm, n // tn),
        in_specs=[
            pl.BlockSpec((tm, d), lambda i, j: (i, 0)),
            pl.BlockSpec((None, 1, d), lambda i, j: (layer, 0, 0)),
            pl.BlockSpec((None, tn, d), lambda i, j: (layer, j, 0)),
        ],
        out_specs=[pl.BlockSpec((tm, tn), lambda i, j: (i, jnp.maximum(j - 1, 0))),
                   pl.BlockSpec((SSM_T, tm // SSM_T, SSM_WIDTH), lambda i, j: (0, i, 0))],
        scratch_shapes=[pltpu.VMEM((tm, d), BF16), pltpu.VMEM((tn // LANES, tm, LANES), F32)],
        compiler_params=_cparams(("parallel", "arbitrary")),
        name="inproj",
    )(x, g, w)


def _s5_prepare(lam_re, lam_im, log_step, b_re, b_im, c_re, c_im):
    hp = lax.Precision.HIGHEST
    t_len = SSM_T
    g_n, p_n = SSM_GROUPS, SSM_STATE
    step = jnp.exp(log_step.astype(F32))[:, None]
    lr = lam_re.astype(F32)
    li = lam_im.astype(F32)
    mag = jnp.exp(lr * step)
    ar = mag * jnp.cos(li * step)
    ai = mag * jnp.sin(li * step)
    den = lr * lr + li * li
    fr = ((ar - 1.0) * lr + ai * li) / den
    fi = (ai * lr - (ar - 1.0) * li) / den
    br = b_re.astype(F32)
    bi = b_im.astype(F32)
    bbr = fr[..., None] * br - fi[..., None] * bi
    bbi = fr[..., None] * bi + fi[..., None] * br
    jj = jnp.arange(t_len + 1, dtype=F32)[None, None, :]
    pmag = jnp.exp((lr * step)[..., None] * jj)
    pang = (li * step)[..., None] * jj
    pw_r = pmag * jnp.cos(pang)
    pw_i = pmag * jnp.sin(pang)
    cr = c_re.astype(F32)
    ci = c_im.astype(F32)
    pr_t = jnp.transpose(pw_r, (0, 2, 1))[:, :, None, :]
    pi_t = jnp.transpose(pw_i, (0, 2, 1))[:, :, None, :]
    ca_r = cr[:, None] * pr_t - ci[:, None] * pi_t
    ca_i = cr[:, None] * pi_t + ci[:, None] * pr_t
    krow = (jnp.einsum('gjcp,gpd->gdjc', ca_r[:, :t_len], bbr, precision=hp)
            - jnp.einsum('gjcp,gpd->gdjc', ca_i[:, :t_len], bbi, precision=hp)).reshape(g_n, SSM_GROUP, SSM_TC)
    d_r = jnp.transpose(ca_r[:, 1:], (0, 3, 1, 2)).reshape(g_n, p_n, SSM_TC)
    d_i = jnp.transpose(ca_i[:, 1:], (0, 3, 1, 2)).reshape(g_n, p_n, SSM_TC)
    rev_r = pw_r[:, :, t_len - 1::-1][:, :, :t_len]
    rev_i = pw_i[:, :, t_len - 1::-1][:, :, :t_len]
    e_r = rev_r[..., None] * bbr[:, :, None, :] - rev_i[..., None] * bbi[:, :, None, :]
    e_i = rev_r[..., None] * bbi[:, :, None, :] + rev_i[..., None] * bbr[:, :, None, :]
    e_r = jnp.transpose(e_r, (0, 2, 3, 1)).reshape(g_n, SSM_TC, p_n)
    e_i = jnp.transpose(e_i, (0, 2, 3, 1)).reshape(g_n, SSM_TC, p_n)
    z_in = jnp.zeros((SSM_PAIRS, SSM_TC, p_n), F32)
    e_r2 = e_r.reshape(SSM_PAIRS, 2, SSM_TC, p_n)
    e_i2 = e_i.reshape(SSM_PAIRS, 2, SSM_TC, p_n)
    min_top = jnp.concatenate([e_r2[:, 0], z_in, e_i2[:, 0], z_in], axis=-1)
    min_bot = jnp.concatenate([z_in, e_r2[:, 1], z_in, e_i2[:, 1]], axis=-1)
    m_in = jnp.concatenate([min_top, min_bot], axis=1)
    z_out = jnp.zeros((SSM_PAIRS, p_n, SSM_TC), F32)
    d_r2 = d_r.reshape(SSM_PAIRS, 2, p_n, SSM_TC)
    d_i2 = d_i.reshape(SSM_PAIRS, 2, p_n, SSM_TC)
    m_out = jnp.concatenate([
        jnp.concatenate([d_r2[:, 0], z_out], axis=-1),
        jnp.concatenate([z_out, d_r2[:, 1]], axis=-1),
        jnp.concatenate([-d_i2[:, 0], z_out], axis=-1),
        jnp.concatenate([z_out, -d_i2[:, 1]], axis=-1)], axis=1)
    at_r = pw_r[:, :, t_len].reshape(1, SSM_STATES)
    at_i = pw_i[:, :, t_len].reshape(1, SSM_STATES)
    return krow, m_in.astype(BF16), m_out.astype(BF16), at_r, at_i


def _s5_toeplitz_kernel(k_ref, o_ref):
    lane = lax.broadcasted_iota(jnp.int32, (SSM_GROUP, SSM_TC), 1)

    def body(g, c):
        kr = k_ref[g]
        for s in range(SSM_T):
            shifted = pltpu.roll(kr, s * SSM_GROUP, axis=1) if s else kr
            o_ref[g, s * SSM_GROUP:(s + 1) * SSM_GROUP, :] = jnp.where(
                lane >= s * SSM_GROUP, shifted, 0.0).astype(o_ref.dtype)
        return c

    lax.fori_loop(0, SSM_GROUPS, body, 0)


def _s5_toeplitz(krow):
    nl = krow.shape[0]
    return pl.pallas_call(
        _s5_toeplitz_kernel,
        out_shape=jax.ShapeDtypeStruct((nl, SSM_GROUPS, SSM_TC, SSM_TC), BF16),
        grid=(nl,),
        in_specs=[pl.BlockSpec((None, SSM_GROUPS, SSM_GROUP, SSM_TC), lambda l: (l, 0, 0, 0))],
        out_specs=pl.BlockSpec((None, SSM_GROUPS, SSM_TC, SSM_TC), lambda l: (l, 0, 0, 0)),
        compiler_params=_cparams(("parallel",)),
        name="s5_toeplitz",
    )(krow)


def _granule_transpose(src):
    x = list(src)
    grp = lax.broadcasted_iota(jnp.int32, x[0].shape, 1) // SSM_GROUP
    stride = SSM_OCT // 2
    while stride:
        upper = (grp & stride) != 0
        for a in range(SSM_OCT):
            if a & stride:
                continue
            lo, hi = x[a], x[a + stride]
            x[a] = jnp.where(upper, pltpu.roll(hi, stride * SSM_GROUP, axis=1), lo)
            x[a + stride] = jnp.where(upper, hi, pltpu.roll(lo, LANES - stride * SSM_GROUP, axis=1))
        stride //= 2
    return x


S5_RELAYOUT_ROWS = 64
S5_RELAYOUT_UNROLL = 1


def _s5_x_kernel(uv_ref, min_ref, u_ref, xr_ref, xi_ref):
    rb = S5_RELAYOUT_ROWS

    def body(r, c):
        rows = pl.ds(pl.multiple_of(r * rb, rb), rb)
        for half in range(2):
            src = [uv_ref[half * SSM_OCT + tp, rows, :].astype(F32) for tp in range(SSM_OCT)]
            out = _granule_transpose(src)
            for g in range(SSM_OCT):
                u_ref[g, rows, half * LANES:(half + 1) * LANES] = out[g].astype(BF16)
        return c

    lax.fori_loop(0, SSM_NC // rb, body, 0, unroll=S5_RELAYOUT_UNROLL)
    for pr in range(SSM_OCT // 2):
        up = jnp.concatenate([u_ref[2 * pr], u_ref[2 * pr + 1]], axis=1)
        x = jnp.dot(up, min_ref[pr], preferred_element_type=F32)
        xr_ref[:, pr * LANES:(pr + 1) * LANES] = x[:, :LANES]
        xi_ref[:, pr * LANES:(pr + 1) * LANES] = x[:, LANES:]


def _s5_x(uv, m_in, layer):
    nc = SSM_NC
    ppo = SSM_OCT // 2
    sw = ppo * LANES
    return pl.pallas_call(
        _s5_x_kernel,
        out_shape=[jax.ShapeDtypeStruct((SSM_GROUPS, nc, SSM_TC), BF16),
                   jax.ShapeDtypeStruct((nc, SSM_STATES), F32),
                   jax.ShapeDtypeStruct((nc, SSM_STATES), F32)],
        grid=(SSM_NOCT,),
        in_specs=[
            pl.BlockSpec((SSM_T, nc, LANES), lambda k: (0, 0, k)),
            pl.BlockSpec((None, ppo, 2 * SSM_TC, 256), lambda k: (layer, k, 0, 0)),
        ],
        out_specs=[pl.BlockSpec((SSM_OCT, nc, SSM_TC), lambda k: (k, 0, 0)),
                   pl.BlockSpec((nc, sw), lambda k: (0, k)),
                   pl.BlockSpec((nc, sw), lambda k: (0, k))],
        compiler_params=_cparams(("parallel",)),
        name="s5_x",
    )(uv, m_in)


def _s5_scan_kernel(xr_ref, xi_ref, ar_ref, ai_ref, hr_ref, hi_ref, *, nc):
    ar = ar_ref[...]
    ai = ai_ref[...]

    def body(n, carry):
        hr, hi = carry
        row = pl.ds(n, 1)
        hr_ref[row, :] = hr
        hi_ref[row, :] = hi
        xr = xr_ref[row, :]
        xi = xi_ref[row, :]
        return (ar * hr - ai * hi + xr, ar * hi + ai * hr + xi)

    zero = jnp.zeros_like(ar)
    lax.fori_loop(0, nc, body, (zero, zero), unroll=8)


def _s5_scan(xr, xi, at_r, at_i, layer, lb=1024):
    nc, ns = xr.shape
    blk = pl.BlockSpec((nc, lb), lambda i: (0, i))
    tab = pl.BlockSpec((None, 1, lb), lambda i: (layer, 0, i))
    return pl.pallas_call(
        functools.partial(_s5_scan_kernel, nc=nc),
        out_shape=[jax.ShapeDtypeStruct((nc, ns), F32)] * 2,
        grid=(ns // lb,),
        in_specs=[blk, blk, tab, tab],
        out_specs=[blk, blk],
        compiler_params=_cparams(("parallel",)),
        name="s5_scan",
    )(xr, xi, at_r, at_i)


def _s5_y_kernel(u_ref, hr_ref, hi_ref, toep_ref, mout_ref, yv_ref, y_scr):
    for pr in range(SSM_OCT // 2):
        hcat = jnp.concatenate([hr_ref[:, pr * LANES:(pr + 1) * LANES],
                                hi_ref[:, pr * LANES:(pr + 1) * LANES]], axis=1).astype(BF16)
        carry = jnp.dot(hcat, mout_ref[pr], preferred_element_type=F32)
        for k in range(2):
            g = 2 * pr + k
            y = jnp.dot(u_ref[g], toep_ref[g], preferred_element_type=F32)
            y_scr[g] = y + carry[:, k * SSM_TC:(k + 1) * SSM_TC]

    rb = S5_RELAYOUT_ROWS

    def body(r, c):
        rows = pl.ds(pl.multiple_of(r * rb, rb), rb)
        for half in range(2):
            src = [y_scr[g, rows, half * LANES:(half + 1) * LANES] for g in range(SSM_OCT)]
            out = _granule_transpose(src)
            for tp in range(SSM_OCT):
                yv_ref[half * SSM_OCT + tp, rows, :] = out[tp].astype(BF16)
        return c

    lax.fori_loop(0, SSM_NC // rb, body, 0, unroll=S5_RELAYOUT_UNROLL)


def _s5_y(u_g, hr, hi, toep, m_out, layer):
    nc = SSM_NC
    ppo = SSM_OCT // 2
    hblk = pl.BlockSpec((nc, ppo * LANES), lambda k: (0, k))
    return pl.pallas_call(
        _s5_y_kernel,
        out_shape=jax.ShapeDtypeStruct((SSM_T, nc, SSM_WIDTH), BF16),
        grid=(SSM_NOCT,),
        in_specs=[
            pl.BlockSpec((SSM_OCT, nc, SSM_TC), lambda k: (k, 0, 0)),
            hblk, hblk,
            pl.BlockSpec((None, SSM_OCT, SSM_TC, SSM_TC), lambda k: (layer, k, 0, 0)),
            pl.BlockSpec((None, ppo, 256, 2 * SSM_TC), lambda k: (layer, k, 0, 0)),
        ],
        out_specs=pl.BlockSpec((SSM_T, nc, LANES), lambda k: (0, 0, k)),
        scratch_shapes=[pltpu.VMEM((SSM_OCT, nc, SSM_TC), F32)],
        compiler_params=_cparams(("parallel",)),
        name="s5_y",
    )(u_g, hr, hi, toep, m_out)


def _s5_glu_kernel(yv_ref, uv_ref, d_ref, w_ref, b_ref, o_ref, z_scr, zb_scr, g_scr, nat_scr, *, ncb):
    d = d_ref[...]
    for t in range(SSM_T):
        z = jax.nn.gelu(yv_ref[t].astype(F32) + d * uv_ref[t].astype(F32))
        z_scr[t * ncb:(t + 1) * ncb, :] = z
        zb_scr[t * ncb:(t + 1) * ncb, :] = z.astype(BF16)
    g_scr[...] = jnp.dot(zb_scr[...], w_ref[...], preferred_element_type=F32)
    b = b_ref[...]
    nt = SSM_WIDTH // LANES
    for t in range(SSM_T):
        rows = slice(t * ncb, (t + 1) * ncb)
        out = z_scr[rows, :] * jax.nn.sigmoid(g_scr[rows, :] + b)
        for k in range(nt):
            nat_scr[k, pl.ds(t, ncb, stride=SSM_T), :] = out[:, k * LANES:(k + 1) * LANES]
    for k in range(nt):
        o_ref[:, k * LANES:(k + 1) * LANES] = nat_scr[k].astype(o_ref.dtype)


def _s5_glu(yv, uv, d_skip, w_glu, b_glu, layer, ncb=64):
    w = SSM_WIDTH
    tm = ncb * SSM_T
    vblk = pl.BlockSpec((SSM_T, ncb, w), lambda i: (0, i, 0))
    vec = pl.BlockSpec((None, 1, w), lambda i: (layer, 0, 0))
    return pl.pallas_call(
        functools.partial(_s5_glu_kernel, ncb=ncb),
        out_shape=jax.ShapeDtypeStruct((SEQ, w), BF16),
        grid=(SSM_NC // ncb,),
        in_specs=[vblk, vblk, vec,
                  pl.BlockSpec((None, w, w), lambda i: (layer, 0, 0)),
                  vec],
        out_specs=pl.BlockSpec((tm, w), lambda i: (i, 0)),
        scratch_shapes=[pltpu.VMEM((tm, w), F32), pltpu.VMEM((tm, w), BF16), pltpu.VMEM((tm, w), F32),
                        pltpu.VMEM((w // LANES, tm, LANES), F32)],
        compiler_params=_cparams(("parallel",)),
        name="s5_glu",
    )(yv, uv, d_skip, w_glu, b_glu)


def _s5_branch(uv, prep, d_skip, w_glu, b_glu, layer):
    toep, m_in, m_out, at_r, at_i = prep
    u_g, xr, xi = _s5_x(uv, m_in, layer)
    hr, hi = _s5_scan(xr, xi, at_r, at_i, layer)
    yv = _s5_y(u_g, hr, hi, toep, m_out, layer)
    return _s5_glu(yv, uv, d_skip, w_glu, b_glu, layer)


def _swa_kernel(sink_ref, q_ref, k_ref, v_ref, cos_ref, sin_ref, o_ref, s_scr, p_scr, kpad_scr, vext_scr):
    n = pl.program_id(0)
    blk = ATT_BLOCK
    hpt = LANES // HEAD_DIM
    nvar = ATT_KV_HEADS * hpt
    tiles_per_kv = ATT_GROUP // hpt
    row = lax.broadcasted_iota(jnp.int32, (blk, blk), 0)
    col = lax.broadcasted_iota(jnp.int32, (blk, blk), 1)

    @pl.when(n == 0)
    def _():
        kpad_scr[...] = jnp.zeros_like(kpad_scr)
        vext_scr[:, :, :LANES] = jnp.zeros((nvar, 2 * blk, LANES), BF16)
        vext_scr[:, :, LANES:] = jnp.ones((nvar, 2 * blk, LANES), BF16)

    @pl.when(n > 0)
    def _():
        for i in range(nvar):
            kpad_scr[i, :blk, :] = kpad_scr[i, blk:, :]
            vext_scr[i, :blk, :LANES] = jnp.where(row == 0, jnp.zeros((), BF16), vext_scr[i, blk:, :LANES])

    lo = lax.broadcasted_iota(jnp.int32, (blk, LANES), 1) < HEAD_DIM

    def lane_padded(tile, half_in):
        other = pltpu.roll(tile, HEAD_DIM, axis=1)
        at_lo, at_hi = (tile, other) if half_in == 0 else (other, tile)
        return jnp.where(lo, at_lo, 0.0).astype(BF16), jnp.where(lo, 0.0, at_hi).astype(BF16)

    cos = cos_ref[...]
    sin = sin_ref[...]
    for kv_tile in range(ATT_KV_HEADS // hpt):
        lanes = slice(kv_tile * LANES, (kv_tile + 1) * LANES)
        kt = _rope(k_ref[:, lanes].astype(F32), cos, sin)
        vt = v_ref[:, lanes].astype(F32)
        for half in range(hpt):
            h = kv_tile * hpt + half
            k_pad = lane_padded(kt, half)
            v_pad = lane_padded(vt, half)
            for e in range(hpt):
                kpad_scr[h * hpt + e, blk:, :] = k_pad[e]
                vext_scr[h * hpt + e, blk:, :LANES] = v_pad[e]

    cur = col <= row
    sink_slot = (col == 0).astype(F32)
    prev_bias = jnp.where(n > 0, 0.0, NEG_BIG)

    for tile in range(ATT_HEADS // hpt):
        qt = _rope(q_ref[:, tile * LANES:(tile + 1) * LANES].astype(F32) * ATT_Q_SCALE, cos, sin).astype(BF16)
        for e in range(hpt):
            var = (tile // tiles_per_kv) * hpt + e
            s_scr[tile * hpt + e] = lax.dot_general(qt, kpad_scr[var], (((1,), (1,)), ((), ())),
                                                    preferred_element_type=F32)

    for hd in range(ATT_HEADS):
        sk = sink_ref[hd] * LOG2E
        s = jnp.where(cur, s_scr[hd, :, blk:], s_scr[hd, :, :blk] + prev_bias)
        m = jnp.maximum(jnp.max(s, axis=-1, keepdims=True), sk)
        p = jnp.exp2(s - m)
        p_scr[hd, :, :blk] = jnp.where(cur, jnp.exp2(sk - m) * sink_slot, p).astype(BF16)
        p_scr[hd, :, blk:] = jnp.where(cur, p, 0.0).astype(BF16)

    for tile in range(ATT_HEADS // hpt):
        acc = None
        for e in range(hpt):
            var = (tile // tiles_per_kv) * hpt + e
            oe = jnp.dot(p_scr[tile * hpt + e], vext_scr[var], preferred_element_type=F32)
            term = oe[:, :LANES] / oe[:, LANES:]
            acc = term if acc is None else acc + term
        o_ref[:, tile * LANES:(tile + 1) * LANES] = acc.astype(o_ref.dtype)


def _swa(proj, sinks, cos_t, sin_t):
    s = proj.shape[0]
    blk = ATT_BLOCK
    nvar = ATT_KV_HEADS * (LANES // HEAD_DIM)
    tab = pl.BlockSpec((blk, LANES), lambda n: (n, 0))
    return pl.pallas_call(
        _swa_kernel,
        out_shape=jax.ShapeDtypeStruct((s, ATT_Q_WIDTH), BF16),
        grid=(s // blk,),
        in_specs=[
            pl.BlockSpec(memory_space=pltpu.SMEM),
            pl.BlockSpec((blk, ATT_Q_WIDTH), lambda n: (n, COL_AQ // ATT_Q_WIDTH)),
            pl.BlockSpec((blk, ATT_KV_WIDTH), lambda n: (n, COL_AK // ATT_KV_WIDTH)),
            pl.BlockSpec((blk, ATT_KV_WIDTH), lambda n: (n, COL_AV // ATT_KV_WIDTH)),
            tab, tab,
        ],
        out_specs=pl.BlockSpec((blk, ATT_Q_WIDTH), lambda n: (n, 0)),
        scratch_shapes=[pltpu.VMEM((ATT_HEADS, blk, 2 * blk), F32), pltpu.VMEM((ATT_HEADS, blk, 2 * blk), BF16),
                        pltpu.VMEM((nvar, 2 * blk, LANES), BF16), pltpu.VMEM((nvar, 2 * blk, 2 * LANES), BF16)],
        compiler_params=_cparams(("arbitrary",)),
        name="swa",
    )(sinks, proj, proj, proj, cos_t, sin_t)


def _log_sigmoid(x):
    return jnp.minimum(x, 0.0) - jnp.log(1.0 + jnp.exp(-jnp.abs(x)))


def _gla_kernel(q_ref, k_ref, v_ref, glr_ref, gout_ref, wg_ref, bg_ref, ng_ref, tri_ref, o_ref,
                st_ref, qin_scr, att_scr, upd_scr, sb_scr, *, rb):
    @pl.when(pl.program_id(0) == 0)
    def _():
        st_ref[...] = jnp.zeros_like(st_ref)

    ch = GLA_CHUNK
    ncb = rb // ch
    nh = GLA_HEADS
    logit = jnp.dot(glr_ref[...], wg_ref[...], preferred_element_type=F32) + bg_ref[...]
    log_a = _log_sigmoid(logit) * (1.0 / GLA_TAU)
    tri = tri_ref[...]
    a_hi = log_a.astype(BF16)
    a_lo = (log_a - a_hi.astype(F32)).astype(BF16)
    bcum = (jnp.dot(tri, a_hi, preferred_element_type=F32)
            + jnp.dot(tri, a_lo, preferred_element_type=F32))
    qin_scr[...] = (q_ref[...].astype(F32) * (GLA_DK ** -0.5) * jnp.exp(bcum)).astype(BF16)
    k = k_ref[...].astype(F32)
    k_in = (k * jnp.exp(-bcum)).astype(BF16)
    causal = (lax.broadcasted_iota(jnp.int32, (ch, ch), 1)
              <= lax.broadcasted_iota(jnp.int32, (ch, ch), 0))

    decays = []
    for c in range(ncb):
        rows = slice(c * ch, (c + 1) * ch)
        b_last = bcum[(c + 1) * ch - 1:(c + 1) * ch, :]
        k_dec = (k[rows] * jnp.exp(b_last - bcum[rows])).astype(BF16)
        decays.append(jnp.exp(b_last))
        for h in range(nh):
            ks = slice(h * GLA_DK, (h + 1) * GLA_DK)
            att = lax.dot_general(qin_scr[rows, ks], k_in[rows, ks], (((1,), (1,)), ((), ())),
                                  preferred_element_type=F32)
            att_scr[c * nh + h] = jnp.where(causal, att, 0.0).astype(BF16)
            upd_scr[c * nh + h] = lax.dot_general(v_ref[rows, h * GLA_DV:(h + 1) * GLA_DV], k_dec[:, ks],
                                                  (((0,), (0,)), ((), ())), preferred_element_type=F32)

    for h in range(nh):
        ks = slice(h * GLA_DK, (h + 1) * GLA_DK)
        st = st_ref[h]
        for c in range(ncb):
            sb_scr[c * nh + h] = st.astype(BF16)
            st = st * decays[c][:, ks] + upd_scr[c * nh + h]
        st_ref[h] = st

    for c in range(ncb):
        rows = slice(c * ch, (c + 1) * ch)
        for h in range(nh):
            ks = slice(h * GLA_DK, (h + 1) * GLA_DK)
            vs = slice(h * GLA_DV, (h + 1) * GLA_DV)
            o = jnp.dot(att_scr[c * nh + h], v_ref[rows, vs], preferred_element_type=F32)
            o = o + lax.dot_general(qin_scr[rows, ks], sb_scr[c * nh + h], (((1,), (1,)), ((), ())),
                                    preferred_element_type=F32)
            o = o * lax.rsqrt(jnp.mean(o * o, axis=-1, keepdims=True) + EPS)
            o = o * ng_ref[:, vs]
            go = gout_ref[rows, vs].astype(F32)
            o = o * (go * jax.nn.sigmoid(go))
            o_ref[rows, vs] = o.astype(o_ref.dtype)


def _gla(proj, w_gate, b_gate, norm_g, layer, rb=256):
    s = proj.shape[0]
    ch = GLA_CHUNK
    idx = np.arange(rb)
    tri = jnp.asarray((idx[None, :] <= idx[:, None]) & (idx[None, :] // ch == idx[:, None] // ch), BF16)
    nblk = (rb // ch) * GLA_HEADS
    return pl.pallas_call(
        functools.partial(_gla_kernel, rb=rb),
        out_shape=jax.ShapeDtypeStruct((s, GLA_V_WIDTH), BF16),
        grid=(s // rb,),
        in_specs=[
            pl.BlockSpec((rb, GLA_QK_WIDTH), lambda i: (i, COL_GQ // GLA_QK_WIDTH)),
            pl.BlockSpec((rb, GLA_QK_WIDTH), lambda i: (i, COL_GK // GLA_QK_WIDTH)),
            pl.BlockSpec((rb, GLA_V_WIDTH), lambda i: (i, COL_GV // GLA_V_WIDTH)),
            pl.BlockSpec((rb, LANES), lambda i: (i, COL_GLR // LANES)),
            pl.BlockSpec((rb, GLA_V_WIDTH), lambda i: (i, COL_GOUT // GLA_V_WIDTH)),
            pl.BlockSpec((None, LANES, GLA_QK_WIDTH), lambda i: (layer, 0, 0)),
            pl.BlockSpec((None, 1, GLA_QK_WIDTH), lambda i: (layer, 0, 0)),
            pl.BlockSpec((None, 1, GLA_V_WIDTH), lambda i: (layer, 0, 0)),
            pl.BlockSpec((rb, rb), lambda i: (0, 0)),
        ],
        out_specs=pl.BlockSpec((rb, GLA_V_WIDTH), lambda i: (i, 0)),
        scratch_shapes=[pltpu.VMEM((GLA_HEADS, GLA_DV, GLA_DK), F32),
                        pltpu.VMEM((rb, GLA_QK_WIDTH), BF16),
                        pltpu.VMEM((nblk, ch, ch), BF16),
                        pltpu.VMEM((nblk, GLA_DV, GLA_DK), F32),
                        pltpu.VMEM((nblk, GLA_DV, GLA_DK), BF16)],
        compiler_params=_cparams(("arbitrary",)),
        name="gla",
    )(proj, proj, proj, proj, proj, w_gate, b_gate, norm_g, tri)


MIX_TN = 512


def _mix_kernel(x_ref, ys_ref, ya_ref, yg_ref, gate_ref, ws_ref, wa_ref, wg_ref, wo_ref, o_ref, m_scr):
    d = D_MODEL
    for j in range(d // MIX_TN):
        cols = slice(j * MIX_TN, (j + 1) * MIX_TN)
        m = None
        for k, (y_ref, w_ref) in enumerate(((ys_ref, ws_ref), (ya_ref, wa_ref), (yg_ref, wg_ref))):
            proj = jnp.dot(y_ref[...], w_ref[:, cols], preferred_element_type=F32)
            gate = gate_ref[:, k * d + j * MIX_TN:k * d + (j + 1) * MIX_TN].astype(F32)
            term = jax.nn.sigmoid(gate) * proj
            m = term if m is None else m + term
        m_scr[:, cols] = m.astype(BF16)
    o_ref[...] = x_ref[...] + jnp.dot(m_scr[...], wo_ref[...], preferred_element_type=F32)


def _mix(x, proj, y_ssm, y_att, y_gla, w_bs, w_ba, w_bg, w_out, layer, tm=256):
    s, d = x.shape
    bw = y_ssm.shape[1]
    yblk = pl.BlockSpec((tm, bw), lambda i: (i, 0))
    wblk = pl.BlockSpec((None, bw, d), lambda i: (layer, 0, 0), pipeline_mode=pl.Buffered(1))
    return pl.pallas_call(
        _mix_kernel,
        out_shape=jax.ShapeDtypeStruct((s, d), F32),
        grid=(s // tm,),
        in_specs=[
            pl.BlockSpec((tm, d), lambda i: (i, 0)),
            yblk, yblk, yblk,
            pl.BlockSpec((tm, 3 * d), lambda i: (i, COL_MERGE // (3 * d))),
            wblk, wblk, wblk,
            pl.BlockSpec((None, d, d), lambda i: (layer, 0, 0), pipeline_mode=pl.Buffered(1)),
        ],
        out_specs=pl.BlockSpec((tm, d), lambda i: (i, 0)),
        scratch_shapes=[pltpu.VMEM((tm, d), BF16)],
        compiler_params=_cparams(("parallel",)),
        name="mix",
    )(x, y_ssm, y_att, y_gla, proj, w_bs, w_ba, w_bg, w_out)


FFN_SLABS = 2


def _ffn_kernel(x_ref, g_ref, w1_ref, w2_ref, fg_ref, o_ref, h_ref, *, tm, nf, final_norm):
    f = pl.program_id(1)

    @pl.when(f == 0)
    def _():
        _rms_rows(x_ref, g_ref, h_ref, tm)
        o_ref[...] = x_ref[...]

    slab = tm // FFN_SLABS
    for r in range(FFN_SLABS):
        rows = slice(r * slab, (r + 1) * slab)
        a = jnp.dot(h_ref[rows, :], w1_ref[...], preferred_element_type=F32)
        a = jnp.square(jnp.maximum(a, 0.0)).astype(BF16)
        o_ref[rows, :] += jnp.dot(a, w2_ref[...], preferred_element_type=F32)

    if final_norm:
        @pl.when(f == nf - 1)
        def _():
            _rms_rows(o_ref, fg_ref, o_ref, tm)


def _ffn(x, g, w1, w2, fg, layer, final_norm, tm=1024, tf=512):
    s, d = x.shape
    dff = w1.shape[2]
    nf = dff // tf
    return pl.pallas_call(
        functools.partial(_ffn_kernel, tm=tm, nf=nf, final_norm=final_norm),
        out_shape=jax.ShapeDtypeStruct((s, d), F32),
        grid=(s // tm, nf),
        in_specs=[
            pl.BlockSpec((tm, d), lambda i, f: (i, 0)),
            pl.BlockSpec((None, 1, d), lambda i, f: (layer, 0, 0)),
            pl.BlockSpec((None, d, tf), lambda i, f: (layer, 0, f)),
            pl.BlockSpec((None, tf, d), lambda i, f: (layer, f, 0)),
            pl.BlockSpec((1, d), lambda i, f: (0, 0)),
        ],
        out_specs=pl.BlockSpec((tm, d), lambda i, f: (i, 0)),
        scratch_shapes=[pltpu.VMEM((tm, d), BF16)],
        compiler_params=_cparams(("parallel", "arbitrary")),
        name="ffn",
    )(x, g, w1, w2, fg)


PACK_TN = 512
PACK_ALIGN = 16
PACK_COPY, PACK_GLR = 0, 1


def _pack_tables():
    sizes = [SSM_WIDTH, ATT_Q_WIDTH, ATT_KV_WIDTH, ATT_KV_WIDTH, GLA_QK_WIDTH, GLA_QK_WIDTH,
             GLA_V_WIDTH, GLA_GATE_RANK, GLA_V_WIDTH, 3 * D_MODEL]
    o_u, o_aq, o_ak, o_av, o_gq, o_gk, o_gv, o_glr, o_gout, o_merge = np.concatenate([[0], np.cumsum(sizes)])[:-1]
    assert o_av == o_ak + ATT_KV_WIDTH
    pieces = [(o_u, SSM_WIDTH), (o_merge, 3 * D_MODEL), (o_aq, ATT_Q_WIDTH), (o_gv, GLA_V_WIDTH),
              (o_gout, GLA_V_WIDTH), (o_gq, GLA_QK_WIDTH), (o_gk, GLA_QK_WIDTH), (o_ak, 2 * ATT_KV_WIDTH)]
    src, mode = [], []
    for start, width in pieces:
        assert start % PACK_ALIGN == 0 and width % PACK_TN == 0
        for k in range(width // PACK_TN):
            src.append((start + k * PACK_TN) // PACK_ALIGN)
            mode.append(PACK_COPY)
    assert o_glr % PACK_ALIGN == 0 and o_glr + PACK_TN <= sum(sizes)
    src.append(o_glr // PACK_ALIGN)
    mode.append(PACK_GLR)
    assert len(mode) * PACK_TN == INPROJ_TN + PROJ_WIDTH
    return [jnp.asarray(np.array(t, np.int32)) for t in (src, mode)]


def _pack_kernel(src, mode, a_ref, o_ref, *, chunk=128):
    md = mode[pl.program_id(1)]

    def for_chunks(fn):
        def body(r, c):
            sl = pl.ds(pl.multiple_of(r * chunk, chunk), chunk)
            o_ref[sl, :] = fn(r, a_ref[0, sl, :]).astype(o_ref.dtype)
            return c
        lax.fori_loop(0, PACK_TN // chunk, body, 0)

    @pl.when(md == PACK_COPY)
    def _():
        for_chunks(lambda r, a: a)

    @pl.when(md == PACK_GLR)
    def _():
        row = lax.broadcasted_iota(jnp.int32, (chunk, a_ref.shape[2]), 0)
        for_chunks(lambda r, a: jnp.where(row + r * chunk < GLA_GATE_RANK, a, 0.0))


def _pack_w_in(w):
    nl, d, _ = w.shape
    wt = jnp.swapaxes(w, 1, 2)
    src, mode = _pack_tables()
    nt = (INPROJ_TN + PROJ_WIDTH) // PACK_TN
    return pl.pallas_call(
        _pack_kernel,
        out_shape=jax.ShapeDtypeStruct((nl, nt * PACK_TN, d), BF16),
        grid_spec=pltpu.PrefetchScalarGridSpec(
            num_scalar_prefetch=2,
            grid=(nl, nt),
            in_specs=[
                pl.BlockSpec((pl.Element(1), pl.Element(PACK_TN), pl.Element(d)),
                             lambda l, j, s, m: (l, pl.multiple_of(s[j] * PACK_ALIGN, PACK_ALIGN), 0)),
            ],
            out_specs=pl.BlockSpec((None, PACK_TN, d), lambda l, j, s, m: (l, j, 0)),
        ),
        compiler_params=_cparams(("parallel", "parallel")),
        name="pack_w_in",
    )(src, mode, wt)


def _rope_tables():
    half = HEAD_DIM // 2
    inv = np.float32(ROPE_THETA) ** (-np.arange(half, dtype=np.float32) / np.float32(half))
    ang = (np.arange(SEQ, dtype=np.float32)[:, None] * inv[None, :].astype(np.float32)).astype(np.float32)
    cos = np.cos(ang.astype(np.float64)).astype(np.float32)
    sin = np.sin(ang.astype(np.float64)).astype(np.float32)
    cos_t = np.concatenate([cos, cos, cos, cos], axis=1)
    sin_t = np.concatenate([-sin, sin, -sin, sin], axis=1)
    return jnp.asarray(cos_t), jnp.asarray(sin_t)


def kernel(x, norm1_g, w_in, ssm_lam_re, ssm_lam_im, ssm_log_step, ssm_b_re, ssm_b_im, ssm_c_re, ssm_c_im, ssm_d, ssm_w_glu, ssm_b_glu, att_sinks, gla_w_gate, gla_b_gate, gla_norm_g, w_branch_ssm, w_branch_att, w_branch_gla, w_out, norm2_g, w_ff1, w_ff2, final_norm_g):
    assert x.shape == (1, SEQ, D_MODEL)
    nl = DEPTH
    xs = x.reshape(SEQ, D_MODEL).astype(F32)
    cos_t, sin_t = _rope_tables()
    fg = final_norm_g.reshape(1, D_MODEL).astype(F32)
    w_in_p = _pack_w_in(w_in)
    n1 = norm1_g.reshape(nl, 1, D_MODEL).astype(F32)
    n2 = norm2_g.reshape(nl, 1, D_MODEL).astype(F32)
    krow, m_in, m_out, at_r, at_i = jax.vmap(_s5_prepare)(
        ssm_lam_re, ssm_lam_im, ssm_log_step, ssm_b_re, ssm_b_im, ssm_c_re, ssm_c_im)
    prep = (_s5_toeplitz(krow), m_in, m_out, at_r, at_i)
    d_skip = ssm_d.reshape(nl, 1, SSM_WIDTH).astype(F32)
    w_glu = ssm_w_glu.astype(BF16)
    b_glu = ssm_b_glu.reshape(nl, 1, SSM_WIDTH).astype(F32)
    sinks = att_sinks.astype(F32)
    wg_pad = jnp.concatenate(
        [gla_w_gate.astype(BF16), jnp.zeros((nl, LANES - GLA_GATE_RANK, GLA_QK_WIDTH), BF16)], axis=1)
    bg = gla_b_gate.reshape(nl, 1, GLA_QK_WIDTH).astype(F32)
    ng = gla_norm_g.reshape(nl, 1, GLA_V_WIDTH).astype(F32)
    w_bs = w_branch_ssm.astype(BF16)
    w_ba = w_branch_att.astype(BF16)
    w_bg = w_branch_gla.astype(BF16)
    w_o = w_out.astype(BF16)
    w1 = w_ff1.astype(BF16)
    w2 = w_ff2.astype(BF16)
    for l in range(nl):
        proj, uv = _inproj(xs, n1, w_in_p, l)
        y_ssm = _s5_branch(uv, prep, d_skip, w_glu, b_glu, l)
        y_att = _swa(proj, sinks[l], cos_t, sin_t)
        y_gla = _gla(proj, wg_pad, bg, ng, l)
        xs = _mix(xs, proj, y_ssm, y_att, y_gla, w_bs, w_ba, w_bg, w_o, l)
        xs = _ffn(xs, n2, w1, w2, fg, l, final_norm=(l == nl - 1))
    return xs.reshape(1, SEQ, D_MODEL)
```

```python
import functools
import math

import jax
import jax.numpy as jnp
import numpy as np
from jax import lax
from jax.experimental import pallas as pl
from jax.experimental.pallas import tpu as pltpu

F32 = jnp.float32
BF16 = jnp.bfloat16

D_MODEL = 2048
SEQ = 16384
DEPTH = 2
SSM_WIDTH = 1024
SSM_GROUP = 16
SSM_GROUPS = 64
SSM_STATE = 64
HEAD_DIM = 64
ATT_HEADS = 16
ATT_KV_HEADS = 4
ATT_GROUP = 4
ATT_Q_WIDTH = 1024
ATT_KV_WIDTH = 256
WINDOW = 128
ATT_BLOCK = 128
ROPE_THETA = 10000.0
GLA_HEADS = 4
GLA_V_WIDTH = 1024
GLA_DV = 256
GLA_DK = 128
GLA_QK_WIDTH = 512
GLA_GATE_RANK = 16
GLA_TAU = 16.0
GLA_CHUNK = 64
D_FF = 8192
EPS = 1e-6

LANES = 128
VMEM_LIMIT = 56 * 1024 * 1024

COL_MERGE = 0
COL_AQ = 6144
COL_GV = 7168
COL_GOUT = 8192
COL_GQ = 9216
COL_GK = 9728
COL_AK = 10240
COL_AV = 10496
COL_GLR = 10752
GLR_PAD = 512
PROJ_WIDTH = 11264
INPROJ_TN = 1024

SSM_T = 16
SSM_NC = SEQ // SSM_T
SSM_TC = SSM_T * SSM_GROUP
SSM_PAIRS = SSM_GROUPS // 2
SSM_STATES = SSM_GROUPS * SSM_STATE
SSM_OCT = LANES // SSM_GROUP
SSM_NOCT = SSM_GROUPS // SSM_OCT
NEG_BIG = -1e30
LOG2E = math.log2(math.e)
ATT_Q_SCALE = HEAD_DIM ** -0.5 * LOG2E


def _cparams(sem, vmem=VMEM_LIMIT):
    return pltpu.CompilerParams(dimension_semantics=sem, vmem_limit_bytes=vmem)


def _rms_rows(x_ref, g_ref, dst_ref, rows, chunk=128):
    g = g_ref[...]

    def body(r, c):
        sl = pl.ds(pl.multiple_of(r * chunk, chunk), chunk)
        xv = x_ref[sl, :].astype(F32)
        ms = jnp.mean(xv * xv, axis=-1, keepdims=True)
        dst_ref[sl, :] = (xv * lax.rsqrt(ms + EPS) * g).astype(dst_ref.dtype)
        return c

    lax.fori_loop(0, rows // chunk, body, 0)


def _rope(x, cos, sin_signed):
    lane = lax.broadcasted_iota(jnp.int32, x.shape, 1) % HEAD_DIM
    swapped = jnp.where(lane < HEAD_DIM // 2,
                        pltpu.roll(x, LANES - HEAD_DIM // 2, axis=1),
                        pltpu.roll(x, HEAD_DIM // 2, axis=1))
    return x * cos + swapped * sin_signed


def _inproj_kernel(x_ref, g_ref, w_ref, o_ref, uv_ref, h_ref, scr_ref, *, tm):
    j = pl.program_id(1)

    def project():
        return lax.dot_general(h_ref[...], w_ref[...], (((1,), (1,)), ((), ())), preferred_element_type=F32)

    @pl.when(j == 0)
    def _():
        _rms_rows(x_ref, g_ref, h_ref, tm)
        res = project()
        nt = res.shape[1] // LANES
        for k in range(nt):
            scr_ref[k] = res[:, k * LANES:(k + 1) * LANES]
        for t in range(SSM_T):
            for k in range(nt):
                uv_ref[t, :, k * LANES:(k + 1) * LANES] = scr_ref[
                    k, pl.ds(t, tm // SSM_T, stride=SSM_T), :].astype(uv_ref.dtype)

    @pl.when(j > 0)
    def _():
        o_ref[...] = project().astype(o_ref.dtype)


def _inproj(x, g, w, layer, tm=1024):
    s, d = x.shape
    tn = INPROJ_TN
    n = w.shape[1]
    return pl.pallas_call(
        functools.partial(_inproj_kernel, tm=tm),
        out_shape=[jax.ShapeDtypeStruct((s, n - tn), BF16),
                   jax.ShapeDtypeStruct((SSM_T, s // SSM_T, SSM_WIDTH), BF16)],
        grid=(s // tm, n // tn),
        in_specs=[
            pl.BlockSpec((tm, d), lambda i, j: (i, 0)),
            pl.BlockSpec((None, 1, d), lambda i, j: (layer, 0, 0)),
            pl.BlockSpec((None, tn, d), lambda i, j: (layer, j, 0)),
        ],
        out_specs=[pl.BlockSpec((tm, tn), lambda i, j: (i, jnp.maximum(j - 1, 0))),
                   pl.BlockSpec((SSM_T, tm // SSM_T, SSM_WIDTH), lambda i, j: (0, i, 0))],
        scratch_shapes=[pltpu.VMEM((tm, d), BF16), pltpu.VMEM((tn // LANES, tm, LANES), F32)],
        compiler_params=_cparams(("parallel", "arbitrary")),
        name="inproj",
    )(x, g, w)


def _s5_prepare(lam_re, lam_im, log_step, b_re, b_im, c_re, c_im):
    t_len = SSM_T
    g_n, p_n = SSM_GROUPS, SSM_STATE
    step = jnp.exp(log_step.astype(F32))[:, None]
    lr = lam_re.astype(F32)
    li = lam_im.astype(F32)
    mag = jnp.exp(lr * step)
    ar = mag * jnp.cos(li * step)
    ai = mag * jnp.sin(li * step)
    den = lr * lr + li * li
    fr = ((ar - 1.0) * lr + ai * li) / den
    fi = (ai * lr - (ar - 1.0) * li) / den
    br = b_re.astype(F32)
    bi = b_im.astype(F32)
    bbr = fr[..., None] * br - fi[..., None] * bi
    bbi = fr[..., None] * bi + fi[..., None] * br
    jj = jnp.arange(t_len + 1, dtype=F32)[None, None, :]
    pmag = jnp.exp((lr * step)[..., None] * jj)
    pang = (li * step)[..., None] * jj
    pw_r = pmag * jnp.cos(pang)
    pw_i = pmag * jnp.sin(pang)
    cr_t = jnp.transpose(c_re.astype(F32), (0, 2, 1))[:, :, None, :]
    ci_t = jnp.transpose(c_im.astype(F32), (0, 2, 1))[:, :, None, :]
    ca_r = cr_t * pw_r[..., None] - ci_t * pw_i[..., None]
    ca_i = cr_t * pw_i[..., None] + ci_t * pw_r[..., None]
    bbr_t = jnp.transpose(bbr, (0, 2, 1))
    bbi_t = jnp.transpose(bbi, (0, 2, 1))
    k_lhs = jnp.concatenate([bbr_t, bbi_t], axis=-1)
    k_rhs = jnp.concatenate([ca_r[:, :, :t_len].reshape(g_n, p_n, SSM_TC),
                             -ca_i[:, :, :t_len].reshape(g_n, p_n, SSM_TC)], axis=1)
    d_r = ca_r[:, :, 1:].reshape(g_n, p_n, SSM_TC)
    d_i = ca_i[:, :, 1:].reshape(g_n, p_n, SSM_TC)
    rev_r = jnp.transpose(pw_r, (0, 2, 1))[:, t_len - 1::-1][:, :t_len, None, :]
    rev_i = jnp.transpose(pw_i, (0, 2, 1))[:, t_len - 1::-1][:, :t_len, None, :]
    e_r = (rev_r * bbr_t[:, None] - rev_i * bbi_t[:, None]).reshape(g_n, SSM_TC, p_n)
    e_i = (rev_r * bbi_t[:, None] + rev_i * bbr_t[:, None]).reshape(g_n, SSM_TC, p_n)
    z_in = jnp.zeros((SSM_PAIRS, SSM_TC, p_n), F32)
    e_r2 = e_r.reshape(SSM_PAIRS, 2, SSM_TC, p_n)
    e_i2 = e_i.reshape(SSM_PAIRS, 2, SSM_TC, p_n)
    min_top = jnp.concatenate([e_r2[:, 0], z_in, e_i2[:, 0], z_in], axis=-1)
    min_bot = jnp.concatenate([z_in, e_r2[:, 1], z_in, e_i2[:, 1]], axis=-1)
    m_in = jnp.concatenate([min_top, min_bot], axis=1)
    z_out = jnp.zeros((SSM_PAIRS, p_n, SSM_TC), F32)
    d_r2 = d_r.reshape(SSM_PAIRS, 2, p_n, SSM_TC)
    d_i2 = d_i.reshape(SSM_PAIRS, 2, p_n, SSM_TC)
    m_out = jnp.concatenate([
        jnp.concatenate([d_r2[:, 0], z_out], axis=-1),
        jnp.concatenate([z_out, d_r2[:, 1]], axis=-1),
        jnp.concatenate([-d_i2[:, 0], z_out], axis=-1),
        jnp.concatenate([z_out, -d_i2[:, 1]], axis=-1)], axis=1)
    at_r = pw_r[:, :, t_len].reshape(1, SSM_STATES)
    at_i = pw_i[:, :, t_len].reshape(1, SSM_STATES)
    return k_lhs, k_rhs, m_in.astype(BF16), m_out.astype(BF16), at_r, at_i


def _s5_toeplitz_kernel(a_ref, b_ref, o_ref):
    lane = lax.broadcasted_iota(jnp.int32, (SSM_GROUP, SSM_TC), 1)

    def body(g, c):
        kr = jnp.dot(a_ref[g], b_ref[g], preferred_element_type=F32,
                     precision=lax.Precision.HIGHEST)
        for s in range(SSM_T):
            shifted = pltpu.roll(kr, s * SSM_GROUP, axis=1) if s else kr
            o_ref[g, s * SSM_GROUP:(s + 1) * SSM_GROUP, :] = jnp.where(
                lane >= s * SSM_GROUP, shifted, 0.0).astype(o_ref.dtype)
        return c

    lax.fori_loop(0, SSM_GROUPS, body, 0)


def _s5_toeplitz(k_lhs, k_rhs):
    nl = k_lhs.shape[0]
    return pl.pallas_call(
        _s5_toeplitz_kernel,
        out_shape=jax.ShapeDtypeStruct((nl, SSM_GROUPS, SSM_TC, SSM_TC), BF16),
        grid=(nl,),
        in_specs=[pl.BlockSpec((None, SSM_GROUPS, SSM_GROUP, 2 * SSM_STATE), lambda l: (l, 0, 0, 0)),
                  pl.BlockSpec((None, SSM_GROUPS, 2 * SSM_STATE, SSM_TC), lambda l: (l, 0, 0, 0))],
        out_specs=pl.BlockSpec((None, SSM_GROUPS, SSM_TC, SSM_TC), lambda l: (l, 0, 0, 0)),
        compiler_params=_cparams(("parallel",)),
        name="s5_toeplitz",
    )(k_lhs, k_rhs)


def _granule_transpose(src):
    x = list(src)
    grp = lax.broadcasted_iota(jnp.int32, x[0].shape, 1) // SSM_GROUP
    stride = SSM_OCT // 2
    while stride:
        upper = (grp & stride) != 0
        for a in range(SSM_OCT):
            if a & stride:
                continue
            lo, hi = x[a], x[a + stride]
            x[a] = jnp.where(upper, pltpu.roll(hi, stride * SSM_GROUP, axis=1), lo)
            x[a + stride] = jnp.where(upper, hi, pltpu.roll(lo, LANES - stride * SSM_GROUP, axis=1))
        stride //= 2
    return x


S5_RELAYOUT_ROWS = 64
S5_RELAYOUT_UNROLL = 1


def _s5_x_kernel(uv_ref, min_ref, u_ref, xr_ref, xi_ref):
    rb = S5_RELAYOUT_ROWS

    def body(r, c):
        rows = pl.ds(pl.multiple_of(r * rb, rb), rb)
        for half in range(2):
            src = [uv_ref[half * SSM_OCT + tp, rows, :].astype(F32) for tp in range(SSM_OCT)]
            out = _granule_transpose(src)
            for g in range(SSM_OCT):
                u_ref[g, rows, half * LANES:(half + 1) * LANES] = out[g].astype(BF16)
        return c

    lax.fori_loop(0, SSM_NC // rb, body, 0, unroll=S5_RELAYOUT_UNROLL)
    for pr in range(SSM_OCT // 2):
        up = jnp.concatenate([u_ref[2 * pr], u_ref[2 * pr + 1]], axis=1)
        x = jnp.dot(up, min_ref[pr], preferred_element_type=F32)
        xr_ref[:, pr * LANES:(pr + 1) * LANES] = x[:, :LANES]
        xi_ref[:, pr * LANES:(pr + 1) * LANES] = x[:, LANES:]


def _s5_x(uv, m_in, layer):
    nc = SSM_NC
    ppo = SSM_OCT // 2
    sw = ppo * LANES
    return pl.pallas_call(
        _s5_x_kernel,
        out_shape=[jax.ShapeDtypeStruct((SSM_GROUPS, nc, SSM_TC), BF16),
                   jax.ShapeDtypeStruct((nc, SSM_STATES), F32),
                   jax.ShapeDtypeStruct((nc, SSM_STATES), F32)],
        grid=(SSM_NOCT,),
        in_specs=[
            pl.BlockSpec((SSM_T, nc, LANES), lambda k: (0, 0, k)),
            pl.BlockSpec((None, ppo, 2 * SSM_TC, 256), lambda k: (layer, k, 0, 0)),
        ],
        out_specs=[pl.BlockSpec((SSM_OCT, nc, SSM_TC), lambda k: (k, 0, 0)),
                   pl.BlockSpec((nc, sw), lambda k: (0, k)),
                   pl.BlockSpec((nc, sw), lambda k: (0, k))],
        compiler_params=_cparams(("parallel",)),
        name="s5_x",
    )(uv, m_in)


def _s5_scan_kernel(xr_ref, xi_ref, ar_ref, ai_ref, hr_ref, hi_ref, *, nc):
    ar = ar_ref[...]
    ai = ai_ref[...]

    def body(n, carry):
        hr, hi = carry
        row = pl.ds(n, 1)
        hr_ref[row, :] = hr
        hi_ref[row, :] = hi
        xr = xr_ref[row, :]
        xi = xi_ref[row, :]
        return (ar * hr - ai * hi + xr, ar * hi + ai * hr + xi)

    zero = jnp.zeros_like(ar)
    lax.fori_loop(0, nc, body, (zero, zero), unroll=8)


def _s5_scan(xr, xi, at_r, at_i, layer, lb=1024):
    nc, ns = xr.shape
    blk = pl.BlockSpec((nc, lb), lambda i: (0, i))
    tab = pl.BlockSpec((None, 1, lb), lambda i: (layer, 0, i))
    return pl.pallas_call(
        functools.partial(_s5_scan_kernel, nc=nc),
        out_shape=[jax.ShapeDtypeStruct((nc, ns), F32)] * 2,
        grid=(ns // lb,),
        in_specs=[blk, blk, tab, tab],
        out_specs=[blk, blk],
        compiler_params=_cparams(("parallel",)),
        name="s5_scan",
    )(xr, xi, at_r, at_i)


def _s5_y_kernel(u_ref, hr_ref, hi_ref, toep_ref, mout_ref, yv_ref, y_scr):
    for pr in range(SSM_OCT // 2):
        hcat = jnp.concatenate([hr_ref[:, pr * LANES:(pr + 1) * LANES],
                                hi_ref[:, pr * LANES:(pr + 1) * LANES]], axis=1).astype(BF16)
        carry = jnp.dot(hcat, mout_ref[pr], preferred_element_type=F32)
        for k in range(2):
            g = 2 * pr + k
            y = jnp.dot(u_ref[g], toep_ref[g], preferred_element_type=F32)
            y_scr[g] = y + carry[:, k * SSM_TC:(k + 1) * SSM_TC]

    rb = S5_RELAYOUT_ROWS

    def body(r, c):
        rows = pl.ds(pl.multiple_of(r * rb, rb), rb)
        for half in range(2):
            src = [y_scr[g, rows, half * LANES:(half + 1) * LANES] for g in range(SSM_OCT)]
            out = _granule_transpose(src)
            for tp in range(SSM_OCT):
                yv_ref[half * SSM_OCT + tp, rows, :] = out[tp].astype(BF16)
        return c

    lax.fori_loop(0, SSM_NC // rb, body, 0, unroll=S5_RELAYOUT_UNROLL)


def _s5_y(u_g, hr, hi, toep, m_out, layer):
    nc = SSM_NC
    ppo = SSM_OCT // 2
    hblk = pl.BlockSpec((nc, ppo * LANES), lambda k: (0, k))
    return pl.pallas_call(
        _s5_y_kernel,
        out_shape=jax.ShapeDtypeStruct((SSM_T, nc, SSM_WIDTH), BF16),
        grid=(SSM_NOCT,),
        in_specs=[
            pl.BlockSpec((SSM_OCT, nc, SSM_TC), lambda k: (k, 0, 0)),
            hblk, hblk,
            pl.BlockSpec((None, SSM_OCT, SSM_TC, SSM_TC), lambda k: (layer, k, 0, 0)),
            pl.BlockSpec((None, ppo, 256, 2 * SSM_TC), lambda k: (layer, k, 0, 0)),
        ],
        out_specs=pl.BlockSpec((SSM_T, nc, LANES), lambda k: (0, 0, k)),
        scratch_shapes=[pltpu.VMEM((SSM_OCT, nc, SSM_TC), F32)],
        compiler_params=_cparams(("parallel",)),
        name="s5_y",
    )(u_g, hr, hi, toep, m_out)


def _s5_glu_kernel(yv_ref, uv_ref, d_ref, w_ref, b_ref, o_ref, z_scr, zb_scr, g_scr, nat_scr, *, ncb):
    d = d_ref[...]
    for t in range(SSM_T):
        z = jax.nn.gelu(yv_ref[t].astype(F32) + d * uv_ref[t].astype(F32))
        z_scr[t * ncb:(t + 1) * ncb, :] = z
        zb_scr[t * ncb:(t + 1) * ncb, :] = z.astype(BF16)
    g_scr[...] = jnp.dot(zb_scr[...], w_ref[...], preferred_element_type=F32)
    b = b_ref[...]
    nt = SSM_WIDTH // LANES
    for t in range(SSM_T):
        rows = slice(t * ncb, (t + 1) * ncb)
        out = z_scr[rows, :] * jax.nn.sigmoid(g_scr[rows, :] + b)
        for k in range(nt):
            nat_scr[k, pl.ds(t, ncb, stride=SSM_T), :] = out[:, k * LANES:(k + 1) * LANES]
    for k in range(nt):
        o_ref[:, k * LANES:(k + 1) * LANES] = nat_scr[k].astype(o_ref.dtype)


def _s5_glu(yv, uv, d_skip, w_glu, b_glu, layer, ncb=64):
    w = SSM_WIDTH
    tm = ncb * SSM_T
    vblk = pl.BlockSpec((SSM_T, ncb, w), lambda i: (0, i, 0))
    vec = pl.BlockSpec((None, 1, w), lambda i: (layer, 0, 0))
    return pl.pallas_call(
        functools.partial(_s5_glu_kernel, ncb=ncb),
        out_shape=jax.ShapeDtypeStruct((SEQ, w), BF16),
        grid=(SSM_NC // ncb,),
        in_specs=[vblk, vblk, vec,
                  pl.BlockSpec((None, w, w), lambda i: (layer, 0, 0)),
                  vec],
        out_specs=pl.BlockSpec((tm, w), lambda i: (i, 0)),
        scratch_shapes=[pltpu.VMEM((tm, w), F32), pltpu.VMEM((tm, w), BF16), pltpu.VMEM((tm, w), F32),
                        pltpu.VMEM((w // LANES, tm, LANES), F32)],
        compiler_params=_cparams(("parallel",)),
        name="s5_glu",
    )(yv, uv, d_skip, w_glu, b_glu)


def _s5_branch(uv, prep, d_skip, w_glu, b_glu, layer):
    toep, m_in, m_out, at_r, at_i = prep
    u_g, xr, xi = _s5_x(uv, m_in, layer)
    hr, hi = _s5_scan(xr, xi, at_r, at_i, layer)
    yv = _s5_y(u_g, hr, hi, toep, m_out, layer)
    return _s5_glu(yv, uv, d_skip, w_glu, b_glu, layer)


def _swa_kernel(sink_ref, q_ref, k_ref, v_ref, cos_ref, sin_ref, o_ref, s_scr, p_scr, kpad_scr, vext_scr):
    n = pl.program_id(0)
    blk = ATT_BLOCK
    hpt = LANES // HEAD_DIM
    nvar = ATT_KV_HEADS * hpt
    tiles_per_kv = ATT_GROUP // hpt
    row = lax.broadcasted_iota(jnp.int32, (blk, blk), 0)
    col = lax.broadcasted_iota(jnp.int32, (blk, blk), 1)

    @pl.when(n == 0)
    def _():
        kpad_scr[...] = jnp.zeros_like(kpad_scr)
        vext_scr[:, :, :LANES] = jnp.zeros((nvar, 2 * blk, LANES), BF16)
        vext_scr[:, :, LANES:] = jnp.ones((nvar, 2 * blk, LANES), BF16)

    @pl.when(n > 0)
    def _():
        for i in range(nvar):
            kpad_scr[i, :blk, :] = kpad_scr[i, blk:, :]
            vext_scr[i, :blk, :LANES] = jnp.where(row == 0, jnp.zeros((), BF16), vext_scr[i, blk:, :LANES])

    lo = lax.broadcasted_iota(jnp.int32, (blk, LANES), 1) < HEAD_DIM

    def lane_padded(tile, half_in):
        other = pltpu.roll(tile, HEAD_DIM, axis=1)
        at_lo, at_hi = (tile, other) if half_in == 0 else (other, tile)
        return jnp.where(lo, at_lo, 0.0).astype(BF16), jnp.where(lo, 0.0, at_hi).astype(BF16)

    cos = cos_ref[...]
    sin = sin_ref[...]
    for kv_tile in range(ATT_KV_HEADS // hpt):
        lanes = slice(kv_tile * LANES, (kv_tile + 1) * LANES)
        kt = _rope(k_ref[:, lanes].astype(F32), cos, sin)
        vt = v_ref[:, lanes].astype(F32)
        for half in range(hpt):
            h = kv_tile * hpt + half
            k_pad = lane_padded(kt, half)
            v_pad = lane_padded(vt, half)
            for e in range(hpt):
                kpad_scr[h * hpt + e, blk:, :] = k_pad[e]
                vext_scr[h * hpt + e, blk:, :LANES] = v_pad[e]

    cur = col <= row
    sink_slot = (col == 0).astype(F32)
    prev_bias = jnp.where(n > 0, 0.0, NEG_BIG)

    for tile in range(ATT_HEADS // hpt):
        qt = _rope(q_ref[:, tile * LANES:(tile + 1) * LANES].astype(F32) * ATT_Q_SCALE, cos, sin).astype(BF16)
        for e in range(hpt):
            var = (tile // tiles_per_kv) * hpt + e
            s_scr[tile * hpt + e] = lax.dot_general(qt, kpad_scr[var], (((1,), (1,)), ((), ())),
                                                    preferred_element_type=F32)

    for hd in range(ATT_HEADS):
        sk = sink_ref[hd] * LOG2E
        s = jnp.where(cur, s_scr[hd, :, blk:], s_scr[hd, :, :blk] + prev_bias)
        m = jnp.maximum(jnp.max(s, axis=-1, keepdims=True), sk)
        p = jnp.exp2(s - m)
        p_scr[hd, :, :blk] = jnp.where(cur, jnp.exp2(sk - m) * sink_slot, p).astype(BF16)
        p_scr[hd, :, blk:] = jnp.where(cur, p, 0.0).astype(BF16)

    for tile in range(ATT_HEADS // hpt):
        acc = None
        for e in range(hpt):
            var = (tile // tiles_per_kv) * hpt + e
            oe = jnp.dot(p_scr[tile * hpt + e], vext_scr[var], preferred_element_type=F32)
            term = oe[:, :LANES] / oe[:, LANES:]
            acc = term if acc is None else acc + term
        o_ref[:, tile * LANES:(tile + 1) * LANES] = acc.astype(o_ref.dtype)


def _swa(proj, sinks, cos_t, sin_t):
    s = proj.shape[0]
    blk = ATT_BLOCK
    nvar = ATT_KV_HEADS * (LANES // HEAD_DIM)
    tab = pl.BlockSpec((blk, LANES), lambda n: (n, 0))
    return pl.pallas_call(
        _swa_kernel,
        out_shape=jax.ShapeDtypeStruct((s, ATT_Q_WIDTH), BF16),
        grid=(s // blk,),
        in_specs=[
            pl.BlockSpec(memory_space=pltpu.SMEM),
            pl.BlockSpec((blk, ATT_Q_WIDTH), lambda n: (n, COL_AQ // ATT_Q_WIDTH)),
            pl.BlockSpec((blk, ATT_KV_WIDTH), lambda n: (n, COL_AK // ATT_KV_WIDTH)),
            pl.BlockSpec((blk, ATT_KV_WIDTH), lambda n: (n, COL_AV // ATT_KV_WIDTH)),
            tab, tab,
        ],
        out_specs=pl.BlockSpec((blk, ATT_Q_WIDTH), lambda n: (n, 0)),
        scratch_shapes=[pltpu.VMEM((ATT_HEADS, blk, 2 * blk), F32), pltpu.VMEM((ATT_HEADS, blk, 2 * blk), BF16),
                        pltpu.VMEM((nvar, 2 * blk, LANES), BF16), pltpu.VMEM((nvar, 2 * blk, 2 * LANES), BF16)],
        compiler_params=_cparams(("arbitrary",)),
        name="swa",
    )(sinks, proj, proj, proj, cos_t, sin_t)


def _log_sigmoid(x):
    return jnp.minimum(x, 0.0) - jnp.log(1.0 + jnp.exp(-jnp.abs(x)))


def _gla_kernel(q_ref, k_ref, v_ref, glr_ref, gout_ref, wg_ref, bg_ref, ng_ref, tri_ref, o_ref,
                st_ref, qin_scr, att_scr, upd_scr, sb_scr, *, rb):
    @pl.when(pl.program_id(0) == 0)
    def _():
        st_ref[...] = jnp.zeros_like(st_ref)

    ch = GLA_CHUNK
    ncb = rb // ch
    nh = GLA_HEADS
    logit = jnp.dot(glr_ref[...], wg_ref[...], preferred_element_type=F32) + bg_ref[...]
    log_a = _log_sigmoid(logit) * (1.0 / GLA_TAU)
    tri = tri_ref[...]
    a_hi = log_a.astype(BF16)
    a_lo = (log_a - a_hi.astype(F32)).astype(BF16)
    bcum = (jnp.dot(tri, a_hi, preferred_element_type=F32)
            + jnp.dot(tri, a_lo, preferred_element_type=F32))
    qin_scr[...] = (q_ref[...].astype(F32) * (GLA_DK ** -0.5) * jnp.exp(bcum)).astype(BF16)
    k = k_ref[...].astype(F32)
    k_in = (k * jnp.exp(-bcum)).astype(BF16)
    causal = (lax.broadcasted_iota(jnp.int32, (ch, ch), 1)
              <= lax.broadcasted_iota(jnp.int32, (ch, ch), 0))

    decays = []
    for c in range(ncb):
        rows = slice(c * ch, (c + 1) * ch)
        b_last = bcum[(c + 1) * ch - 1:(c + 1) * ch, :]
        k_dec = (k[rows] * jnp.exp(b_last - bcum[rows])).astype(BF16)
        decays.append(jnp.exp(b_last))
        for h in range(nh):
            ks = slice(h * GLA_DK, (h + 1) * GLA_DK)
            att = lax.dot_general(qin_scr[rows, ks], k_in[rows, ks], (((1,), (1,)), ((), ())),
                                  preferred_element_type=F32)
            att_scr[c * nh + h] = jnp.where(causal, att, 0.0).astype(BF16)
            upd_scr[c * nh + h] = lax.dot_general(v_ref[rows, h * GLA_DV:(h + 1) * GLA_DV], k_dec[:, ks],
                                                  (((0,), (0,)), ((), ())), preferred_element_type=F32)

    for h in range(nh):
        ks = slice(h * GLA_DK, (h + 1) * GLA_DK)
        st = st_ref[h]
        for c in range(ncb):
            sb_scr[c * nh + h] = st.astype(BF16)
            st = st * decays[c][:, ks] + upd_scr[c * nh + h]
        st_ref[h] = st

    for c in range(ncb):
        rows = slice(c * ch, (c + 1) * ch)
        for h in range(nh):
            ks = slice(h * GLA_DK, (h + 1) * GLA_DK)
            vs = slice(h * GLA_DV, (h + 1) * GLA_DV)
            o = jnp.dot(att_scr[c * nh + h], v_ref[rows, vs], preferred_element_type=F32)
            o = o + lax.dot_general(qin_scr[rows, ks], sb_scr[c * nh + h], (((1,), (1,)), ((), ())),
                                    preferred_element_type=F32)
            o = o * lax.rsqrt(jnp.mean(o * o, axis=-1, keepdims=True) + EPS)
            o = o * ng_ref[:, vs]
            go = gout_ref[rows, vs].astype(F32)
            o = o * (go * jax.nn.sigmoid(go))
            o_ref[rows, vs] = o.astype(o_ref.dtype)


def _gla(proj, w_gate, b_gate, norm_g, layer, rb=256):
    s = proj.shape[0]
    ch = GLA_CHUNK
    idx = np.arange(rb)
    tri = jnp.asarray((idx[None, :] <= idx[:, None]) & (idx[None, :] // ch == idx[:, None] // ch), BF16)
    nblk = (rb // ch) * GLA_HEADS
    return pl.pallas_call(
        functools.partial(_gla_kernel, rb=rb),
        out_shape=jax.ShapeDtypeStruct((s, GLA_V_WIDTH), BF16),
        grid=(s // rb,),
        in_specs=[
            pl.BlockSpec((rb, GLA_QK_WIDTH), lambda i: (i, COL_GQ // GLA_QK_WIDTH)),
            pl.BlockSpec((rb, GLA_QK_WIDTH), lambda i: (i, COL_GK // GLA_QK_WIDTH)),
            pl.BlockSpec((rb, GLA_V_WIDTH), lambda i: (i, COL_GV // GLA_V_WIDTH)),
            pl.BlockSpec((rb, LANES), lambda i: (i, COL_GLR // LANES)),
            pl.BlockSpec((rb, GLA_V_WIDTH), lambda i: (i, COL_GOUT // GLA_V_WIDTH)),
            pl.BlockSpec((None, LANES, GLA_QK_WIDTH), lambda i: (layer, 0, 0)),
            pl.BlockSpec((None, 1, GLA_QK_WIDTH), lambda i: (layer, 0, 0)),
            pl.BlockSpec((None, 1, GLA_V_WIDTH), lambda i: (layer, 0, 0)),
            pl.BlockSpec((rb, rb), lambda i: (0, 0)),
        ],
        out_specs=pl.BlockSpec((rb, GLA_V_WIDTH), lambda i: (i, 0)),
        scratch_shapes=[pltpu.VMEM((GLA_HEADS, GLA_DV, GLA_DK), F32),
                        pltpu.VMEM((rb, GLA_QK_WIDTH), BF16),
                        pltpu.VMEM((nblk, ch, ch), BF16),
                        pltpu.VMEM((nblk, GLA_DV, GLA_DK), F32),
                        pltpu.VMEM((nblk, GLA_DV, GLA_DK), BF16)],
        compiler_params=_cparams(("arbitrary",)),
        name="gla",
    )(proj, proj, proj, proj, proj, w_gate, b_gate, norm_g, tri)


MIX_TN = 512


def _mix_kernel(x_ref, ys_ref, ya_ref, yg_ref, gate_ref, ws_ref, wa_ref, wg_ref, wo_ref, w1f_ref, w2f_ref,
                o_ref, w1b_ref, w2b_ref, m_scr):
    d = D_MODEL
    w1b_ref[...] = w1f_ref[...].astype(BF16)
    w2b_ref[...] = w2f_ref[...].astype(BF16)
    for j in range(d // MIX_TN):
        cols = slice(j * MIX_TN, (j + 1) * MIX_TN)
        m = None
        for k, (y_ref, w_ref) in enumerate(((ys_ref, ws_ref), (ya_ref, wa_ref), (yg_ref, wg_ref))):
            proj = jnp.dot(y_ref[...], w_ref[:, cols], preferred_element_type=F32)
            gate = gate_ref[:, k * d + j * MIX_TN:k * d + (j + 1) * MIX_TN].astype(F32)
            term = jax.nn.sigmoid(gate) * proj
            m = term if m is None else m + term
        m_scr[:, cols] = m.astype(BF16)
    o_ref[...] = x_ref[...] + jnp.dot(m_scr[...], wo_ref[...], preferred_element_type=F32)


def _mix(x, proj, y_ssm, y_att, y_gla, w_bs, w_ba, w_bg, w_out, w_ff1, w_ff2, layer, tm=256):
    s, d = x.shape
    bw = y_ssm.shape[1]
    dff = w_ff1.shape[2]
    steps = s // tm
    r1, r2 = d // steps, dff // steps
    yblk = pl.BlockSpec((tm, bw), lambda i: (i, 0))
    wblk = pl.BlockSpec((None, bw, d), lambda i: (layer, 0, 0), pipeline_mode=pl.Buffered(1))
    return pl.pallas_call(
        _mix_kernel,
        out_shape=[jax.ShapeDtypeStruct((s, d), F32),
                   jax.ShapeDtypeStruct((d, dff), BF16),
                   jax.ShapeDtypeStruct((dff, d), BF16)],
        grid=(steps,),
        in_specs=[
            pl.BlockSpec((tm, d), lambda i: (i, 0)),
            yblk, yblk, yblk,
            pl.BlockSpec((tm, 3 * d), lambda i: (i, COL_MERGE // (3 * d))),
            wblk, wblk, wblk,
            pl.BlockSpec((None, d, d), lambda i: (layer, 0, 0), pipeline_mode=pl.Buffered(1)),
            pl.BlockSpec((None, r1, dff), lambda i: (layer, i, 0)),
            pl.BlockSpec((None, r2, d), lambda i: (layer, i, 0)),
        ],
        out_specs=[pl.BlockSpec((tm, d), lambda i: (i, 0)),
                   pl.BlockSpec((r1, dff), lambda i: (i, 0)),
                   pl.BlockSpec((r2, d), lambda i: (i, 0))],
        scratch_shapes=[pltpu.VMEM((tm, d), BF16)],
        compiler_params=_cparams(("parallel",)),
        name="mix",
    )(x, y_ssm, y_att, y_gla, proj, w_bs, w_ba, w_bg, w_out, w_ff1, w_ff2)


FFN_SLABS = 2


def _ffn_kernel(x_ref, g_ref, w1_ref, w2_ref, fg_ref, o_ref, h_ref, *, tm, nf, final_norm):
    f = pl.program_id(1)

    @pl.when(f == 0)
    def _():
        _rms_rows(x_ref, g_ref, h_ref, tm)
        o_ref[...] = x_ref[...]

    slab = tm // FFN_SLABS
    for r in range(FFN_SLABS):
        rows = slice(r * slab, (r + 1) * slab)
        a = jnp.dot(h_ref[rows, :], w1_ref[...], preferred_element_type=F32)
        a = jnp.square(jnp.maximum(a, 0.0)).astype(BF16)
        o_ref[rows, :] += jnp.dot(a, w2_ref[...], preferred_element_type=F32)

    if final_norm:
        @pl.when(f == nf - 1)
        def _():
            _rms_rows(o_ref, fg_ref, o_ref, tm)


def _ffn(x, g, w1, w2, fg, layer, final_norm, tm=1024, tf=512):
    s, d = x.shape
    dff = w1.shape[1]
    nf = dff // tf
    return pl.pallas_call(
        functools.partial(_ffn_kernel, tm=tm, nf=nf, final_norm=final_norm),
        out_shape=jax.ShapeDtypeStruct((s, d), F32),
        grid=(s // tm, nf),
        in_specs=[
            pl.BlockSpec((tm, d), lambda i, f: (i, 0)),
            pl.BlockSpec((None, 1, d), lambda i, f: (layer, 0, 0)),
            pl.BlockSpec((d, tf), lambda i, f: (0, f)),
            pl.BlockSpec((tf, d), lambda i, f: (f, 0)),
            pl.BlockSpec((1, d), lambda i, f: (0, 0)),
        ],
        out_specs=pl.BlockSpec((tm, d), lambda i, f: (i, 0)),
        scratch_shapes=[pltpu.VMEM((tm, d), BF16)],
        compiler_params=_cparams(("parallel", "arbitrary")),
        name="ffn",
    )(x, g, w1, w2, fg)


PACK_TN = 512
PACK_ALIGN = 16
PACK_COPY, PACK_GLR = 0, 1


def _pack_tables():
    sizes = [SSM_WIDTH, ATT_Q_WIDTH, ATT_KV_WIDTH, ATT_KV_WIDTH, GLA_QK_WIDTH, GLA_QK_WIDTH,
             GLA_V_WIDTH, GLA_GATE_RANK, GLA_V_WIDTH, 3 * D_MODEL]
    o_u, o_aq, o_ak, o_av, o_gq, o_gk, o_gv, o_glr, o_gout, o_merge = np.concatenate([[0], np.cumsum(sizes)])[:-1]
    assert o_av == o_ak + ATT_KV_WIDTH
    pieces = [(o_u, SSM_WIDTH), (o_merge, 3 * D_MODEL), (o_aq, ATT_Q_WIDTH), (o_gv, GLA_V_WIDTH),
              (o_gout, GLA_V_WIDTH), (o_gq, GLA_QK_WIDTH), (o_gk, GLA_QK_WIDTH), (o_ak, 2 * ATT_KV_WIDTH)]
    src, mode = [], []
    for start, width in pieces:
        assert start % PACK_ALIGN == 0 and width % PACK_TN == 0
        for k in range(width // PACK_TN):
            src.append((start + k * PACK_TN) // PACK_ALIGN)
            mode.append(PACK_COPY)
    assert o_glr % PACK_ALIGN == 0 and o_glr + PACK_TN <= sum(sizes)
    src.append(o_glr // PACK_ALIGN)
    mode.append(PACK_GLR)
    assert len(mode) * PACK_TN == INPROJ_TN + PROJ_WIDTH
    return [jnp.asarray(np.array(t, np.int32)) for t in (src, mode)]


def _pack_kernel(src, mode, a_ref, o_ref, *, chunk=128):
    md = mode[pl.program_id(1)]

    def for_chunks(fn):
        def body(r, c):
            sl = pl.ds(pl.multiple_of(r * chunk, chunk), chunk)
            o_ref[sl, :] = fn(r, a_ref[0, sl, :]).astype(o_ref.dtype)
            return c
        lax.fori_loop(0, PACK_TN // chunk, body, 0)

    @pl.when(md == PACK_COPY)
    def _():
        for_chunks(lambda r, a: a)

    @pl.when(md == PACK_GLR)
    def _():
        row = lax.broadcasted_iota(jnp.int32, (chunk, a_ref.shape[2]), 0)
        for_chunks(lambda r, a: jnp.where(row + r * chunk < GLA_GATE_RANK, a, 0.0))


def _pack_w_in(w):
    nl, d, _ = w.shape
    wt = jnp.swapaxes(w, 1, 2)
    src, mode = _pack_tables()
    nt = (INPROJ_TN + PROJ_WIDTH) // PACK_TN
    return pl.pallas_call(
        _pack_kernel,
        out_shape=jax.ShapeDtypeStruct((nl, nt * PACK_TN, d), BF16),
        grid_spec=pltpu.PrefetchScalarGridSpec(
            num_scalar_prefetch=2,
            grid=(nl, nt),
            in_specs=[
                pl.BlockSpec((pl.Element(1), pl.Element(PACK_TN), pl.Element(d)),
                             lambda l, j, s, m: (l, pl.multiple_of(s[j] * PACK_ALIGN, PACK_ALIGN), 0)),
            ],
            out_specs=pl.BlockSpec((None, PACK_TN, d), lambda l, j, s, m: (l, j, 0)),
        ),
        compiler_params=_cparams(("parallel", "parallel")),
        name="pack_w_in",
    )(src, mode, wt)


def _rope_tables():
    half = HEAD_DIM // 2
    inv = np.float32(ROPE_THETA) ** (-np.arange(half, dtype=np.float32) / np.float32(half))
    ang = (np.arange(SEQ, dtype=np.float32)[:, None] * inv[None, :].astype(np.float32)).astype(np.float32)
    cos = np.cos(ang.astype(np.float64)).astype(np.float32)
    sin = np.sin(ang.astype(np.float64)).astype(np.float32)
    cos_t = np.concatenate([cos, cos, cos, cos], axis=1)
    sin_t = np.concatenate([-sin, sin, -sin, sin], axis=1)
    return jnp.asarray(cos_t), jnp.asarray(sin_t)


def kernel(x, norm1_g, w_in, ssm_lam_re, ssm_lam_im, ssm_log_step, ssm_b_re, ssm_b_im, ssm_c_re, ssm_c_im, ssm_d, ssm_w_glu, ssm_b_glu, att_sinks, gla_w_gate, gla_b_gate, gla_norm_g, w_branch_ssm, w_branch_att, w_branch_gla, w_out, norm2_g, w_ff1, w_ff2, final_norm_g):
    assert x.shape == (1, SEQ, D_MODEL)
    nl = DEPTH
    xs = x.reshape(SEQ, D_MODEL).astype(F32)
    cos_t, sin_t = _rope_tables()
    fg = final_norm_g.reshape(1, D_MODEL).astype(F32)
    w_in_p = _pack_w_in(w_in)
    n1 = norm1_g.reshape(nl, 1, D_MODEL).astype(F32)
    n2 = norm2_g.reshape(nl, 1, D_MODEL).astype(F32)
    k_lhs, k_rhs, m_in, m_out, at_r, at_i = jax.vmap(_s5_prepare)(
        ssm_lam_re, ssm_lam_im, ssm_log_step, ssm_b_re, ssm_b_im, ssm_c_re, ssm_c_im)
    prep = (_s5_toeplitz(k_lhs, k_rhs), m_in, m_out, at_r, at_i)
    d_skip = ssm_d.reshape(nl, 1, SSM_WIDTH).astype(F32)
    w_glu = ssm_w_glu.astype(BF16)
    b_glu = ssm_b_glu.reshape(nl, 1, SSM_WIDTH).astype(F32)
    sinks = att_sinks.astype(F32)
    wg_pad = jnp.concatenate(
        [gla_w_gate.astype(BF16), jnp.zeros((nl, LANES - GLA_GATE_RANK, GLA_QK_WIDTH), BF16)], axis=1)
    bg = gla_b_gate.reshape(nl, 1, GLA_QK_WIDTH).astype(F32)
    ng = gla_norm_g.reshape(nl, 1, GLA_V_WIDTH).astype(F32)
    w_bs = w_branch_ssm.astype(BF16)
    w_ba = w_branch_att.astype(BF16)
    w_bg = w_branch_gla.astype(BF16)
    w_o = w_out.astype(BF16)
    w_ff1 = w_ff1.astype(F32)
    w_ff2 = w_ff2.astype(F32)
    for l in range(nl):
        proj, uv = _inproj(xs, n1, w_in_p, l)
        y_ssm = _s5_branch(uv, prep, d_skip, w_glu, b_glu, l)
        y_att = _swa(proj, sinks[l], cos_t, sin_t)
        y_gla = _gla(proj, wg_pad, bg, ng, l)
        xs, w1, w2 = _mix(xs, proj, y_ssm, y_att, y_gla, w_bs, w_ba, w_bg, w_o, w_ff1, w_ff2, l)
        xs = _ffn(xs, n2, w1, w2, fg, l, final_norm=(l == nl - 1))
    return xs.reshape(1, SEQ, D_MODEL)
```

```python
import functools
import math

import jax
import jax.numpy as jnp
import numpy as np
from jax import lax
from jax.experimental import pallas as pl
from jax.experimental.pallas import tpu as pltpu

F32 = jnp.float32
BF16 = jnp.bfloat16

D_MODEL = 2048
SEQ = 16384
DEPTH = 2
SSM_WIDTH = 1024
SSM_GROUP = 16
SSM_GROUPS = 64
SSM_STATE = 64
HEAD_DIM = 64
ATT_HEADS = 16
ATT_KV_HEADS = 4
ATT_GROUP = 4
ATT_Q_WIDTH = 1024
ATT_KV_WIDTH = 256
WINDOW = 128
ATT_BLOCK = 128
ROPE_THETA = 10000.0
GLA_HEADS = 4
GLA_V_WIDTH = 1024
GLA_DV = 256
GLA_DK = 128
GLA_QK_WIDTH = 512
GLA_GATE_RANK = 16
GLA_TAU = 16.0
GLA_CHUNK = 64
D_FF = 8192
EPS = 1e-6

LANES = 128
VMEM_LIMIT = 56 * 1024 * 1024

COL_MERGE = 0
COL_AQ = 6144
COL_GV = 7168
COL_GOUT = 8192
COL_GQ = 9216
COL_GK = 9728
COL_AK = 10240
COL_AV = 10496
COL_GLR = 10752
GLR_PAD = 512
PROJ_WIDTH = 11264
INPROJ_TN = 1024

SSM_T = 16
SSM_NC = SEQ // SSM_T
SSM_TC = SSM_T * SSM_GROUP
SSM_PAIRS = SSM_GROUPS // 2
SSM_STATES = SSM_GROUPS * SSM_STATE
SSM_OCT = LANES // SSM_GROUP
SSM_NOCT = SSM_GROUPS // SSM_OCT
NEG_BIG = -1e30
LOG2E = math.log2(math.e)
ATT_Q_SCALE = HEAD_DIM ** -0.5 * LOG2E


def _cparams(sem, vmem=VMEM_LIMIT):
    return pltpu.CompilerParams(dimension_semantics=sem, vmem_limit_bytes=vmem)


def _rms_rows(x_ref, g_ref, dst_ref, rows, chunk=128):
    g = g_ref[...]

    def body(r, c):
        sl = pl.ds(pl.multiple_of(r * chunk, chunk), chunk)
        xv = x_ref[sl, :].astype(F32)
        ms = jnp.mean(xv * xv, axis=-1, keepdims=True)
        dst_ref[sl, :] = (xv * lax.rsqrt(ms + EPS) * g).astype(dst_ref.dtype)
        return c

    lax.fori_loop(0, rows // chunk, body, 0)


def _rope(x, cos, sin_signed):
    lane = lax.broadcasted_iota(jnp.int32, x.shape, 1) % HEAD_DIM
    swapped = jnp.where(lane < HEAD_DIM // 2,
                        pltpu.roll(x, LANES - HEAD_DIM // 2, axis=1),
                        pltpu.roll(x, HEAD_DIM // 2, axis=1))
    return x * cos + swapped * sin_signed


def _inproj_kernel(x_ref, g_ref, w_ref, *refs, tm, n_cast):
    cast_src = refs[:n_cast]
    o_ref, uv_ref = refs[n_cast:n_cast + 2]
    cast_dst = refs[n_cast + 2:2 * n_cast + 2]
    h_ref, scr_ref = refs[2 * n_cast + 2:]
    j = pl.program_id(1)

    def project():
        return lax.dot_general(h_ref[...], w_ref[...], (((1,), (1,)), ((), ())), preferred_element_type=F32)

    @pl.when(j == 0)
    def _():
        for src, dst in zip(cast_src, cast_dst):
            dst[...] = src[...].astype(dst.dtype)
        _rms_rows(x_ref, g_ref, h_ref, tm)
        res = project()
        nt = res.shape[1] // LANES
        for k in range(nt):
            scr_ref[k] = res[:, k * LANES:(k + 1) * LANES]
        for t in range(SSM_T):
            for k in range(nt):
                uv_ref[t, :, k * LANES:(k + 1) * LANES] = scr_ref[
                    k, pl.ds(t, tm // SSM_T, stride=SSM_T), :].astype(uv_ref.dtype)

    @pl.when(j > 0)
    def _():
        o_ref[...] = project().astype(o_ref.dtype)


def _inproj(x, g, w, cast_weights, layer, tm=1024):
    s, d = x.shape
    tn = INPROJ_TN
    n = w.shape[1]
    tiles = s // tm
    cast_in, cast_out, cast_shapes = [], [], []
    for cw in cast_weights:
        _, rows, cols = cw.shape
        slab = rows // tiles
        cast_in.append(pl.BlockSpec((None, slab, cols), lambda i, j: (layer, i, 0)))
        cast_out.append(pl.BlockSpec((slab, cols), lambda i, j: (i, 0)))
        cast_shapes.append(jax.ShapeDtypeStruct((rows, cols), BF16))
    outs = pl.pallas_call(
        functools.partial(_inproj_kernel, tm=tm, n_cast=len(cast_weights)),
        out_shape=[jax.ShapeDtypeStruct((s, n - tn), BF16),
                   jax.ShapeDtypeStruct((SSM_T, s // SSM_T, SSM_WIDTH), BF16)] + cast_shapes,
        grid=(tiles, n // tn),
        in_specs=[
            pl.BlockSpec((tm, d), lambda i, j: (i, 0)),
            pl.BlockSpec((None, 1, d), lambda i, j: (layer, 0, 0)),
            pl.BlockSpec((None, tn, d), lambda i, j: (layer, j, 0)),
        ] + cast_in,
        out_specs=[pl.BlockSpec((tm, tn), lambda i, j: (i, jnp.maximum(j - 1, 0))),
                   pl.BlockSpec((SSM_T, tm // SSM_T, SSM_WIDTH), lambda i, j: (0, i, 0))] + cast_out,
        scratch_shapes=[pltpu.VMEM((tm, d), BF16), pltpu.VMEM((tn // LANES, tm, LANES), F32)],
        compiler_params=_cparams(("parallel", "arbitrary")),
        name="inproj",
    )(x, g, w, *cast_weights)
    return outs[0], outs[1], outs[2:]


def _s5_prepare(lam_re, lam_im, log_step, b_re, b_im, c_re, c_im):
    t_len = SSM_T
    g_n, p_n = SSM_GROUPS, SSM_STATE
    step = jnp.exp(log_step.astype(F32))[:, None]
    lr = lam_re.astype(F32)
    li = lam_im.astype(F32)
    mag = jnp.exp(lr * step)
    ar = mag * jnp.cos(li * step)
    ai = mag * jnp.sin(li * step)
    den = lr * lr + li * li
    fr = ((ar - 1.0) * lr + ai * li) / den
    fi = (ai * lr - (ar - 1.0) * li) / den
    br = b_re.astype(F32)
    bi = b_im.astype(F32)
    bbr = fr[..., None] * br - fi[..., None] * bi
    bbi = fr[..., None] * bi + fi[..., None] * br
    jj = jnp.arange(t_len + 1, dtype=F32)[None, None, :]
    pmag = jnp.exp((lr * step)[..., None] * jj)
    pang = (li * step)[..., None] * jj
    pw_r = pmag * jnp.cos(pang)
    pw_i = pmag * jnp.sin(pang)
    cr_t = jnp.transpose(c_re.astype(F32), (0, 2, 1))[:, :, None, :]
    ci_t = jnp.transpose(c_im.astype(F32), (0, 2, 1))[:, :, None, :]

    def c_times_powers(first):
        qr = pw_r[:, :, first:first + t_len, None]
        qi = pw_i[:, :, first:first + t_len, None]
        return ((cr_t * qr - ci_t * qi).reshape(g_n, p_n, SSM_TC),
                (cr_t * qi + ci_t * qr).reshape(g_n, p_n, SSM_TC))

    bbr_t = jnp.transpose(bbr, (0, 2, 1))
    bbi_t = jnp.transpose(bbi, (0, 2, 1))
    k_r, k_i = c_times_powers(0)
    k_lhs = jnp.concatenate([bbr_t, bbi_t], axis=-1)
    k_rhs = jnp.concatenate([k_r, -k_i], axis=1)
    d_r, d_i = c_times_powers(1)
    rev_r = jnp.transpose(pw_r, (0, 2, 1))[:, t_len - 1::-1][:, :t_len, None, :]
    rev_i = jnp.transpose(pw_i, (0, 2, 1))[:, t_len - 1::-1][:, :t_len, None, :]
    e_r = (rev_r * bbr_t[:, None] - rev_i * bbi_t[:, None]).reshape(g_n, SSM_TC, p_n)
    e_i = (rev_r * bbi_t[:, None] + rev_i * bbr_t[:, None]).reshape(g_n, SSM_TC, p_n)
    z_in = jnp.zeros((SSM_PAIRS, SSM_TC, p_n), F32)
    e_r2 = e_r.reshape(SSM_PAIRS, 2, SSM_TC, p_n)
    e_i2 = e_i.reshape(SSM_PAIRS, 2, SSM_TC, p_n)
    min_top = jnp.concatenate([e_r2[:, 0], z_in, e_i2[:, 0], z_in], axis=-1)
    min_bot = jnp.concatenate([z_in, e_r2[:, 1], z_in, e_i2[:, 1]], axis=-1)
    m_in = jnp.concatenate([min_top, min_bot], axis=1)
    z_out = jnp.zeros((SSM_PAIRS, p_n, SSM_TC), F32)
    d_r2 = d_r.reshape(SSM_PAIRS, 2, p_n, SSM_TC)
    d_i2 = d_i.reshape(SSM_PAIRS, 2, p_n, SSM_TC)
    m_out = jnp.concatenate([
        jnp.concatenate([d_r2[:, 0], z_out], axis=-1),
        jnp.concatenate([z_out, d_r2[:, 1]], axis=-1),
        jnp.concatenate([-d_i2[:, 0], z_out], axis=-1),
        jnp.concatenate([z_out, -d_i2[:, 1]], axis=-1)], axis=1)
    at_r = pw_r[:, :, t_len].reshape(1, SSM_STATES)
    at_i = pw_i[:, :, t_len].reshape(1, SSM_STATES)
    return k_lhs, k_rhs, m_in.astype(BF16), m_out.astype(BF16), at_r, at_i


def _s5_toeplitz_kernel(a_ref, b_ref, o_ref):
    lane = lax.broadcasted_iota(jnp.int32, (SSM_GROUP, SSM_TC), 1)

    def body(g, c):
        kr = jnp.dot(a_ref[g], b_ref[g], preferred_element_type=F32,
                     precision=lax.Precision.HIGHEST)
        for s in range(SSM_T):
            shifted = pltpu.roll(kr, s * SSM_GROUP, axis=1) if s else kr
            o_ref[g, s * SSM_GROUP:(s + 1) * SSM_GROUP, :] = jnp.where(
                lane >= s * SSM_GROUP, shifted, 0.0).astype(o_ref.dtype)
        return c

    lax.fori_loop(0, SSM_GROUPS, body, 0)


def _s5_toeplitz(k_lhs, k_rhs):
    nl = k_lhs.shape[0]
    return pl.pallas_call(
        _s5_toeplitz_kernel,
        out_shape=jax.ShapeDtypeStruct((nl, SSM_GROUPS, SSM_TC, SSM_TC), BF16),
        grid=(nl,),
        in_specs=[pl.BlockSpec((None, SSM_GROUPS, SSM_GROUP, 2 * SSM_STATE), lambda l: (l, 0, 0, 0)),
                  pl.BlockSpec((None, SSM_GROUPS, 2 * SSM_STATE, SSM_TC), lambda l: (l, 0, 0, 0))],
        out_specs=pl.BlockSpec((None, SSM_GROUPS, SSM_TC, SSM_TC), lambda l: (l, 0, 0, 0)),
        compiler_params=_cparams(("parallel",)),
        name="s5_toeplitz",
    )(k_lhs, k_rhs)


def _granule_transpose(src):
    x = list(src)
    grp = lax.broadcasted_iota(jnp.int32, x[0].shape, 1) // SSM_GROUP
    stride = SSM_OCT // 2
    while stride:
        upper = (grp & stride) != 0
        for a in range(SSM_OCT):
            if a & stride:
                continue
            lo, hi = x[a], x[a + stride]
            x[a] = jnp.where(upper, pltpu.roll(hi, stride * SSM_GROUP, axis=1), lo)
            x[a + stride] = jnp.where(upper, hi, pltpu.roll(lo, LANES - stride * SSM_GROUP, axis=1))
        stride //= 2
    return x


S5_RELAYOUT_ROWS = 64
S5_RELAYOUT_UNROLL = 1


def _s5_x_kernel(uv_ref, min_ref, u_ref, xr_ref, xi_ref):
    rb = S5_RELAYOUT_ROWS

    def body(r, c):
        rows = pl.ds(pl.multiple_of(r * rb, rb), rb)
        for half in range(2):
            src = [uv_ref[half * SSM_OCT + tp, rows, :].astype(F32) for tp in range(SSM_OCT)]
            out = _granule_transpose(src)
            for g in range(SSM_OCT):
                u_ref[g, rows, half * LANES:(half + 1) * LANES] = out[g].astype(BF16)
        return c

    lax.fori_loop(0, SSM_NC // rb, body, 0, unroll=S5_RELAYOUT_UNROLL)
    for pr in range(SSM_OCT // 2):
        up = jnp.concatenate([u_ref[2 * pr], u_ref[2 * pr + 1]], axis=1)
        x = jnp.dot(up, min_ref[pr], preferred_element_type=F32)
        xr_ref[:, pr * LANES:(pr + 1) * LANES] = x[:, :LANES]
        xi_ref[:, pr * LANES:(pr + 1) * LANES] = x[:, LANES:]


def _s5_x(uv, m_in, layer):
    nc = SSM_NC
    ppo = SSM_OCT // 2
    sw = ppo * LANES
    return pl.pallas_call(
        _s5_x_kernel,
        out_shape=[jax.ShapeDtypeStruct((SSM_GROUPS, nc, SSM_TC), BF16),
                   jax.ShapeDtypeStruct((nc, SSM_STATES), F32),
                   jax.ShapeDtypeStruct((nc, SSM_STATES), F32)],
        grid=(SSM_NOCT,),
        in_specs=[
            pl.BlockSpec((SSM_T, nc, LANES), lambda k: (0, 0, k)),
            pl.BlockSpec((None, ppo, 2 * SSM_TC, 256), lambda k: (layer, k, 0, 0)),
        ],
        out_specs=[pl.BlockSpec((SSM_OCT, nc, SSM_TC), lambda k: (k, 0, 0)),
                   pl.BlockSpec((nc, sw), lambda k: (0, k)),
                   pl.BlockSpec((nc, sw), lambda k: (0, k))],
        compiler_params=_cparams(("parallel",)),
        name="s5_x",
    )(uv, m_in)


def _s5_scan_kernel(xr_ref, xi_ref, ar_ref, ai_ref, hr_ref, hi_ref, *, nc):
    ar = ar_ref[...]
    ai = ai_ref[...]

    def body(n, carry):
        hr, hi = carry
        row = pl.ds(n, 1)
        hr_ref[row, :] = hr
        hi_ref[row, :] = hi
        xr = xr_ref[row, :]
        xi = xi_ref[row, :]
        return (ar * hr - ai * hi + xr, ar * hi + ai * hr + xi)

    zero = jnp.zeros_like(ar)
    lax.fori_loop(0, nc, body, (zero, zero), unroll=8)


def _s5_scan(xr, xi, at_r, at_i, layer, lb=1024):
    nc, ns = xr.shape
    blk = pl.BlockSpec((nc, lb), lambda i: (0, i))
    tab = pl.BlockSpec((None, 1, lb), lambda i: (layer, 0, i))
    return pl.pallas_call(
        functools.partial(_s5_scan_kernel, nc=nc),
        out_shape=[jax.ShapeDtypeStruct((nc, ns), F32)] * 2,
        grid=(ns // lb,),
        in_specs=[blk, blk, tab, tab],
        out_specs=[blk, blk],
        compiler_params=_cparams(("parallel",)),
        name="s5_scan",
    )(xr, xi, at_r, at_i)


def _s5_y_kernel(u_ref, hr_ref, hi_ref, toep_ref, mout_ref, yv_ref, y_scr):
    for pr in range(SSM_OCT // 2):
        hcat = jnp.concatenate([hr_ref[:, pr * LANES:(pr + 1) * LANES],
                                hi_ref[:, pr * LANES:(pr + 1) * LANES]], axis=1).astype(BF16)
        carry = jnp.dot(hcat, mout_ref[pr], preferred_element_type=F32)
        for k in range(2):
            g = 2 * pr + k
            y = jnp.dot(u_ref[g], toep_ref[g], preferred_element_type=F32)
            y_scr[g] = y + carry[:, k * SSM_TC:(k + 1) * SSM_TC]

    rb = S5_RELAYOUT_ROWS

    def body(r, c):
        rows = pl.ds(pl.multiple_of(r * rb, rb), rb)
        for half in range(2):
            src = [y_scr[g, rows, half * LANES:(half + 1) * LANES] for g in range(SSM_OCT)]
            out = _granule_transpose(src)
            for tp in range(SSM_OCT):
                yv_ref[half * SSM_OCT + tp, rows, :] = out[tp].astype(BF16)
        return c

    lax.fori_loop(0, SSM_NC // rb, body, 0, unroll=S5_RELAYOUT_UNROLL)


def _s5_y(u_g, hr, hi, toep, m_out, layer):
    nc = SSM_NC
    ppo = SSM_OCT // 2
    hblk = pl.BlockSpec((nc, ppo * LANES), lambda k: (0, k))
    return pl.pallas_call(
        _s5_y_kernel,
        out_shape=jax.ShapeDtypeStruct((SSM_T, nc, SSM_WIDTH), BF16),
        grid=(SSM_NOCT,),
        in_specs=[
            pl.BlockSpec((SSM_OCT, nc, SSM_TC), lambda k: (k, 0, 0)),
            hblk, hblk,
            pl.BlockSpec((None, SSM_OCT, SSM_TC, SSM_TC), lambda k: (layer, k, 0, 0)),
            pl.BlockSpec((None, ppo, 256, 2 * SSM_TC), lambda k: (layer, k, 0, 0)),
        ],
        out_specs=pl.BlockSpec((SSM_T, nc, LANES), lambda k: (0, 0, k)),
        scratch_shapes=[pltpu.VMEM((SSM_OCT, nc, SSM_TC), F32)],
        compiler_params=_cparams(("parallel",)),
        name="s5_y",
    )(u_g, hr, hi, toep, m_out)


def _s5_glu_kernel(yv_ref, uv_ref, d_ref, w_ref, b_ref, o_ref, z_scr, zb_scr, g_scr, nat_scr, *, ncb):
    d = d_ref[...]
    for t in range(SSM_T):
        z = jax.nn.gelu(yv_ref[t].astype(F32) + d * uv_ref[t].astype(F32))
        z_scr[t * ncb:(t + 1) * ncb, :] = z
        zb_scr[t * ncb:(t + 1) * ncb, :] = z.astype(BF16)
    g_scr[...] = jnp.dot(zb_scr[...], w_ref[...], preferred_element_type=F32)
    b = b_ref[...]
    nt = SSM_WIDTH // LANES
    for t in range(SSM_T):
        rows = slice(t * ncb, (t + 1) * ncb)
        out = z_scr[rows, :] * jax.nn.sigmoid(g_scr[rows, :] + b)
        for k in range(nt):
            nat_scr[k, pl.ds(t, ncb, stride=SSM_T), :] = out[:, k * LANES:(k + 1) * LANES]
    for k in range(nt):
        o_ref[:, k * LANES:(k + 1) * LANES] = nat_scr[k].astype(o_ref.dtype)


def _s5_glu(yv, uv, d_skip, w_glu, b_glu, layer, ncb=64):
    w = SSM_WIDTH
    tm = ncb * SSM_T
    vblk = pl.BlockSpec((SSM_T, ncb, w), lambda i: (0, i, 0))
    vec = pl.BlockSpec((None, 1, w), lambda i: (layer, 0, 0))
    return pl.pallas_call(
        functools.partial(_s5_glu_kernel, ncb=ncb),
        out_shape=jax.ShapeDtypeStruct((SEQ, w), BF16),
        grid=(SSM_NC // ncb,),
        in_specs=[vblk, vblk, vec,
                  pl.BlockSpec((w, w), lambda i: (0, 0)),
                  vec],
        out_specs=pl.BlockSpec((tm, w), lambda i: (i, 0)),
        scratch_shapes=[pltpu.VMEM((tm, w), F32), pltpu.VMEM((tm, w), BF16), pltpu.VMEM((tm, w), F32),
                        pltpu.VMEM((w // LANES, tm, LANES), F32)],
        compiler_params=_cparams(("parallel",)),
        name="s5_glu",
    )(yv, uv, d_skip, w_glu, b_glu)


def _s5_branch(uv, prep, d_skip, w_glu, b_glu, layer):
    toep, m_in, m_out, at_r, at_i = prep
    u_g, xr, xi = _s5_x(uv, m_in, layer)
    hr, hi = _s5_scan(xr, xi, at_r, at_i, layer)
    yv = _s5_y(u_g, hr, hi, toep, m_out, layer)
    return _s5_glu(yv, uv, d_skip, w_glu, b_glu, layer)


def _swa_kernel(sink_ref, q_ref, k_ref, v_ref, cos_ref, sin_ref, o_ref, s_scr, p_scr, kpad_scr, vext_scr):
    n = pl.program_id(0)
    blk = ATT_BLOCK
    hpt = LANES // HEAD_DIM
    nvar = ATT_KV_HEADS * hpt
    tiles_per_kv = ATT_GROUP // hpt
    row = lax.broadcasted_iota(jnp.int32, (blk, blk), 0)
    col = lax.broadcasted_iota(jnp.int32, (blk, blk), 1)

    @pl.when(n == 0)
    def _():
        kpad_scr[...] = jnp.zeros_like(kpad_scr)
        vext_scr[:, :, :LANES] = jnp.zeros((nvar, 2 * blk, LANES), BF16)
        vext_scr[:, :, LANES:] = jnp.ones((nvar, 2 * blk, LANES), BF16)

    @pl.when(n > 0)
    def _():
        for i in range(nvar):
            kpad_scr[i, :blk, :] = kpad_scr[i, blk:, :]
            vext_scr[i, :blk, :LANES] = jnp.where(row == 0, jnp.zeros((), BF16), vext_scr[i, blk:, :LANES])

    lo = lax.broadcasted_iota(jnp.int32, (blk, LANES), 1) < HEAD_DIM

    def lane_padded(tile, half_in):
        other = pltpu.roll(tile, HEAD_DIM, axis=1)
        at_lo, at_hi = (tile, other) if half_in == 0 else (other, tile)
        return jnp.where(lo, at_lo, 0.0).astype(BF16), jnp.where(lo, 0.0, at_hi).astype(BF16)

    cos = cos_ref[...]
    sin = sin_ref[...]
    for kv_tile in range(ATT_KV_HEADS // hpt):
        lanes = slice(kv_tile * LANES, (kv_tile + 1) * LANES)
        kt = _rope(k_ref[:, lanes].astype(F32), cos, sin)
        vt = v_ref[:, lanes].astype(F32)
        for half in range(hpt):
            h = kv_tile * hpt + half
            k_pad = lane_padded(kt, half)
            v_pad = lane_padded(vt, half)
            for e in range(hpt):
                kpad_scr[h * hpt + e, blk:, :] = k_pad[e]
                vext_scr[h * hpt + e, blk:, :LANES] = v_pad[e]

    cur = col <= row
    sink_slot = (col == 0).astype(F32)
    prev_bias = jnp.where(n > 0, 0.0, NEG_BIG)

    for tile in range(ATT_HEADS // hpt):
        qt = _rope(q_ref[:, tile * LANES:(tile + 1) * LANES].astype(F32) * ATT_Q_SCALE, cos, sin).astype(BF16)
        for e in range(hpt):
            var = (tile // tiles_per_kv) * hpt + e
            s_scr[tile * hpt + e] = lax.dot_general(qt, kpad_scr[var], (((1,), (1,)), ((), ())),
                                                    preferred_element_type=F32)

    for hd in range(ATT_HEADS):
        sk = sink_ref[hd] * LOG2E
        s = jnp.where(cur, s_scr[hd, :, blk:], s_scr[hd, :, :blk] + prev_bias)
        m = jnp.maximum(jnp.max(s, axis=-1, keepdims=True), sk)
        p = jnp.exp2(s - m)
        p_scr[hd, :, :blk] = jnp.where(cur, jnp.exp2(sk - m) * sink_slot, p).astype(BF16)
        p_scr[hd, :, blk:] = jnp.where(cur, p, 0.0).astype(BF16)

    for tile in range(ATT_HEADS // hpt):
        acc = None
        for e in range(hpt):
            var = (tile // tiles_per_kv) * hpt + e
            oe = jnp.dot(p_scr[tile * hpt + e], vext_scr[var], preferred_element_type=F32)
            term = oe[:, :LANES] / oe[:, LANES:]
            acc = term if acc is None else acc + term
        o_ref[:, tile * LANES:(tile + 1) * LANES] = acc.astype(o_ref.dtype)


def _swa(proj, sinks, cos_t, sin_t):
    s = proj.shape[0]
    blk = ATT_BLOCK
    nvar = ATT_KV_HEADS * (LANES // HEAD_DIM)
    tab = pl.BlockSpec((blk, LANES), lambda n: (n, 0))
    return pl.pallas_call(
        _swa_kernel,
        out_shape=jax.ShapeDtypeStruct((s, ATT_Q_WIDTH), BF16),
        grid=(s // blk,),
        in_specs=[
            pl.BlockSpec(memory_space=pltpu.SMEM),
            pl.BlockSpec((blk, ATT_Q_WIDTH), lambda n: (n, COL_AQ // ATT_Q_WIDTH)),
            pl.BlockSpec((blk, ATT_KV_WIDTH), lambda n: (n, COL_AK // ATT_KV_WIDTH)),
            pl.BlockSpec((blk, ATT_KV_WIDTH), lambda n: (n, COL_AV // ATT_KV_WIDTH)),
            tab, tab,
        ],
        out_specs=pl.BlockSpec((blk, ATT_Q_WIDTH), lambda n: (n, 0)),
        scratch_shapes=[pltpu.VMEM((ATT_HEADS, blk, 2 * blk), F32), pltpu.VMEM((ATT_HEADS, blk, 2 * blk), BF16),
                        pltpu.VMEM((nvar, 2 * blk, LANES), BF16), pltpu.VMEM((nvar, 2 * blk, 2 * LANES), BF16)],
        compiler_params=_cparams(("arbitrary",)),
        name="swa",
    )(sinks, proj, proj, proj, cos_t, sin_t)


def _log_sigmoid(x):
    return jnp.minimum(x, 0.0) - jnp.log(1.0 + jnp.exp(-jnp.abs(x)))


def _gla_kernel(q_ref, k_ref, v_ref, glr_ref, gout_ref, wg_ref, bg_ref, ng_ref, tri_ref, o_ref,
                st_ref, qin_scr, att_scr, upd_scr, sb_scr, *, rb):
    @pl.when(pl.program_id(0) == 0)
    def _():
        st_ref[...] = jnp.zeros_like(st_ref)

    ch = GLA_CHUNK
    ncb = rb // ch
    nh = GLA_HEADS
    logit = jnp.dot(glr_ref[...], wg_ref[...], preferred_element_type=F32) + bg_ref[...]
    log_a = _log_sigmoid(logit) * (1.0 / GLA_TAU)
    tri = tri_ref[...]
    a_hi = log_a.astype(BF16)
    a_lo = (log_a - a_hi.astype(F32)).astype(BF16)
    bcum = (jnp.dot(tri, a_hi, preferred_element_type=F32)
            + jnp.dot(tri, a_lo, preferred_element_type=F32))
    qin_scr[...] = (q_ref[...].astype(F32) * (GLA_DK ** -0.5) * jnp.exp(bcum)).astype(BF16)
    k = k_ref[...].astype(F32)
    k_in = (k * jnp.exp(-bcum)).astype(BF16)
    causal = (lax.broadcasted_iota(jnp.int32, (ch, ch), 1)
              <= lax.broadcasted_iota(jnp.int32, (ch, ch), 0))

    decays = []
    for c in range(ncb):
        rows = slice(c * ch, (c + 1) * ch)
        b_last = bcum[(c + 1) * ch - 1:(c + 1) * ch, :]
        k_dec = (k[rows] * jnp.exp(b_last - bcum[rows])).astype(BF16)
        decays.append(jnp.exp(b_last))
        for h in range(nh):
            ks = slice(h * GLA_DK, (h + 1) * GLA_DK)
            att = lax.dot_general(qin_scr[rows, ks], k_in[rows, ks], (((1,), (1,)), ((), ())),
                                  preferred_element_type=F32)
            att_scr[c * nh + h] = jnp.where(causal, att, 0.0).astype(BF16)
            upd_scr[c * nh + h] = lax.dot_general(v_ref[rows, h * GLA_DV:(h + 1) * GLA_DV], k_dec[:, ks],
                                                  (((0,), (0,)), ((), ())), preferred_element_type=F32)

    for h in range(nh):
        ks = slice(h * GLA_DK, (h + 1) * GLA_DK)
        st = st_ref[h]
        for c in range(ncb):
            sb_scr[c * nh + h] = st.astype(BF16)
            st = st * decays[c][:, ks] + upd_scr[c * nh + h]
        st_ref[h] = st

    for c in range(ncb):
        rows = slice(c * ch, (c + 1) * ch)
        for h in range(nh):
            ks = slice(h * GLA_DK, (h + 1) * GLA_DK)
            vs = slice(h * GLA_DV, (h + 1) * GLA_DV)
            o = jnp.dot(att_scr[c * nh + h], v_ref[rows, vs], preferred_element_type=F32)
            o = o + lax.dot_general(qin_scr[rows, ks], sb_scr[c * nh + h], (((1,), (1,)), ((), ())),
                                    preferred_element_type=F32)
            o = o * lax.rsqrt(jnp.mean(o * o, axis=-1, keepdims=True) + EPS)
            o = o * ng_ref[:, vs]
            go = gout_ref[rows, vs].astype(F32)
            o = o * (go * jax.nn.sigmoid(go))
            o_ref[rows, vs] = o.astype(o_ref.dtype)


def _gla(proj, w_gate, b_gate, norm_g, layer, rb=256):
    s = proj.shape[0]
    ch = GLA_CHUNK
    idx = np.arange(rb)
    tri = jnp.asarray((idx[None, :] <= idx[:, None]) & (idx[None, :] // ch == idx[:, None] // ch), BF16)
    nblk = (rb // ch) * GLA_HEADS
    return pl.pallas_call(
        functools.partial(_gla_kernel, rb=rb),
        out_shape=jax.ShapeDtypeStruct((s, GLA_V_WIDTH), BF16),
        grid=(s // rb,),
        in_specs=[
            pl.BlockSpec((rb, GLA_QK_WIDTH), lambda i: (i, COL_GQ // GLA_QK_WIDTH)),
            pl.BlockSpec((rb, GLA_QK_WIDTH), lambda i: (i, COL_GK // GLA_QK_WIDTH)),
            pl.BlockSpec((rb, GLA_V_WIDTH), lambda i: (i, COL_GV // GLA_V_WIDTH)),
            pl.BlockSpec((rb, LANES), lambda i: (i, COL_GLR // LANES)),
            pl.BlockSpec((rb, GLA_V_WIDTH), lambda i: (i, COL_GOUT // GLA_V_WIDTH)),
            pl.BlockSpec((None, LANES, GLA_QK_WIDTH), lambda i: (layer, 0, 0)),
            pl.BlockSpec((None, 1, GLA_QK_WIDTH), lambda i: (layer, 0, 0)),
            pl.BlockSpec((None, 1, GLA_V_WIDTH), lambda i: (layer, 0, 0)),
            pl.BlockSpec((rb, rb), lambda i: (0, 0)),
        ],
        out_specs=pl.BlockSpec((rb, GLA_V_WIDTH), lambda i: (i, 0)),
        scratch_shapes=[pltpu.VMEM((GLA_HEADS, GLA_DV, GLA_DK), F32),
                        pltpu.VMEM((rb, GLA_QK_WIDTH), BF16),
                        pltpu.VMEM((nblk, ch, ch), BF16),
                        pltpu.VMEM((nblk, GLA_DV, GLA_DK), F32),
                        pltpu.VMEM((nblk, GLA_DV, GLA_DK), BF16)],
        compiler_params=_cparams(("arbitrary",)),
        name="gla",
    )(proj, proj, proj, proj, proj, w_gate, b_gate, norm_g, tri)


MIX_TN = 512


def _mix_kernel(x_ref, ys_ref, ya_ref, yg_ref, gate_ref, ws_ref, wa_ref, wg_ref, wo_ref, w1f_ref, w2f_ref,
                o_ref, w1b_ref, w2b_ref, m_scr):
    d = D_MODEL
    w1b_ref[...] = w1f_ref[...].astype(BF16)
    w2b_ref[...] = w2f_ref[...].astype(BF16)
    for j in range(d // MIX_TN):
        cols = slice(j * MIX_TN, (j + 1) * MIX_TN)
        m = None
        for k, (y_ref, w_ref) in enumerate(((ys_ref, ws_ref), (ya_ref, wa_ref), (yg_ref, wg_ref))):
            proj = jnp.dot(y_ref[...], w_ref[:, cols], preferred_element_type=F32)
            gate = gate_ref[:, k * d + j * MIX_TN:k * d + (j + 1) * MIX_TN].astype(F32)
            term = jax.nn.sigmoid(gate) * proj
            m = term if m is None else m + term
        m_scr[:, cols] = m.astype(BF16)
    o_ref[...] = x_ref[...] + jnp.dot(m_scr[...], wo_ref[...], preferred_element_type=F32)


def _mix(x, proj, y_ssm, y_att, y_gla, w_bs, w_ba, w_bg, w_out, w_ff1, w_ff2, layer, tm=256):
    s, d = x.shape
    bw = y_ssm.shape[1]
    dff = w_ff1.shape[2]
    steps = s // tm
    r1, r2 = d // steps, dff // steps
    yblk = pl.BlockSpec((tm, bw), lambda i: (i, 0))
    wblk = pl.BlockSpec((bw, d), lambda i: (0, 0), pipeline_mode=pl.Buffered(1))
    return pl.pallas_call(
        _mix_kernel,
        out_shape=[jax.ShapeDtypeStruct((s, d), F32),
                   jax.ShapeDtypeStruct((d, dff), BF16),
                   jax.ShapeDtypeStruct((dff, d), BF16)],
        grid=(steps,),
        in_specs=[
            pl.BlockSpec((tm, d), lambda i: (i, 0)),
            yblk, yblk, yblk,
            pl.BlockSpec((tm, 3 * d), lambda i: (i, COL_MERGE // (3 * d))),
            wblk, wblk, wblk,
            pl.BlockSpec((d, d), lambda i: (0, 0), pipeline_mode=pl.Buffered(1)),
            pl.BlockSpec((None, r1, dff), lambda i: (layer, i, 0)),
            pl.BlockSpec((None, r2, d), lambda i: (layer, i, 0)),
        ],
        out_specs=[pl.BlockSpec((tm, d), lambda i: (i, 0)),
                   pl.BlockSpec((r1, dff), lambda i: (i, 0)),
                   pl.BlockSpec((r2, d), lambda i: (i, 0))],
        scratch_shapes=[pltpu.VMEM((tm, d), BF16)],
        compiler_params=_cparams(("parallel",)),
        name="mix",
    )(x, y_ssm, y_att, y_gla, proj, w_bs, w_ba, w_bg, w_out, w_ff1, w_ff2)


FFN_SLABS = 2


def _ffn_kernel(x_ref, g_ref, w1_ref, w2_ref, fg_ref, o_ref, h_ref, *, tm, nf, final_norm):
    f = pl.program_id(1)

    @pl.when(f == 0)
    def _():
        _rms_rows(x_ref, g_ref, h_ref, tm)
        o_ref[...] = x_ref[...]

    slab = tm // FFN_SLABS
    for r in range(FFN_SLABS):
        rows = slice(r * slab, (r + 1) * slab)
        a = jnp.dot(h_ref[rows, :], w1_ref[...], preferred_element_type=F32)
        a = jnp.square(jnp.maximum(a, 0.0)).astype(BF16)
        o_ref[rows, :] += jnp.dot(a, w2_ref[...], preferred_element_type=F32)

    if final_norm:
        @pl.when(f == nf - 1)
        def _():
            _rms_rows(o_ref, fg_ref, o_ref, tm)


def _ffn(x, g, w1, w2, fg, layer, final_norm, tm=1024, tf=512):
    s, d = x.shape
    dff = w1.shape[1]
    nf = dff // tf
    return pl.pallas_call(
        functools.partial(_ffn_kernel, tm=tm, nf=nf, final_norm=final_norm),
        out_shape=jax.ShapeDtypeStruct((s, d), F32),
        grid=(s // tm, nf),
        in_specs=[
            pl.BlockSpec((tm, d), lambda i, f: (i, 0)),
            pl.BlockSpec((None, 1, d), lambda i, f: (layer, 0, 0)),
            pl.BlockSpec((d, tf), lambda i, f: (0, f)),
            pl.BlockSpec((tf, d), lambda i, f: (f, 0)),
            pl.BlockSpec((1, d), lambda i, f: (0, 0)),
        ],
        out_specs=pl.BlockSpec((tm, d), lambda i, f: (i, 0)),
        scratch_shapes=[pltpu.VMEM((tm, d), BF16)],
        compiler_params=_cparams(("parallel", "arbitrary")),
        name="ffn",
    )(x, g, w1, w2, fg)


PACK_TN = 512
PACK_ALIGN = 16
PACK_COPY, PACK_GLR = 0, 1


def _pack_tables():
    sizes = [SSM_WIDTH, ATT_Q_WIDTH, ATT_KV_WIDTH, ATT_KV_WIDTH, GLA_QK_WIDTH, GLA_QK_WIDTH,
             GLA_V_WIDTH, GLA_GATE_RANK, GLA_V_WIDTH, 3 * D_MODEL]
    o_u, o_aq, o_ak, o_av, o_gq, o_gk, o_gv, o_glr, o_gout, o_merge = np.concatenate([[0], np.cumsum(sizes)])[:-1]
    assert o_av == o_ak + ATT_KV_WIDTH
    pieces = [(o_u, SSM_WIDTH), (o_merge, 3 * D_MODEL), (o_aq, ATT_Q_WIDTH), (o_gv, GLA_V_WIDTH),
              (o_gout, GLA_V_WIDTH), (o_gq, GLA_QK_WIDTH), (o_gk, GLA_QK_WIDTH), (o_ak, 2 * ATT_KV_WIDTH)]
    src, mode = [], []
    for start, width in pieces:
        assert start % PACK_ALIGN == 0 and width % PACK_TN == 0
        for k in range(width // PACK_TN):
            src.append((start + k * PACK_TN) // PACK_ALIGN)
            mode.append(PACK_COPY)
    assert o_glr % PACK_ALIGN == 0 and o_glr + PACK_TN <= sum(sizes)
    src.append(o_glr // PACK_ALIGN)
    mode.append(PACK_GLR)
    assert len(mode) * PACK_TN == INPROJ_TN + PROJ_WIDTH
    return [jnp.asarray(np.array(t, np.int32)) for t in (src, mode)]


def _pack_kernel(src, mode, a_ref, o_ref, *, chunk=128):
    md = mode[pl.program_id(1)]

    def for_chunks(fn):
        def body(r, c):
            sl = pl.ds(pl.multiple_of(r * chunk, chunk), chunk)
            o_ref[sl, :] = fn(r, a_ref[0, sl, :]).astype(o_ref.dtype)
            return c
        lax.fori_loop(0, PACK_TN // chunk, body, 0)

    @pl.when(md == PACK_COPY)
    def _():
        for_chunks(lambda r, a: a)

    @pl.when(md == PACK_GLR)
    def _():
        row = lax.broadcasted_iota(jnp.int32, (chunk, a_ref.shape[2]), 0)
        for_chunks(lambda r, a: jnp.where(row + r * chunk < GLA_GATE_RANK, a, 0.0))


def _pack_w_in(w):
    nl, d, _ = w.shape
    wt = jnp.swapaxes(w, 1, 2)
    src, mode = _pack_tables()
    nt = (INPROJ_TN + PROJ_WIDTH) // PACK_TN
    return pl.pallas_call(
        _pack_kernel,
        out_shape=jax.ShapeDtypeStruct((nl, nt * PACK_TN, d), BF16),
        grid_spec=pltpu.PrefetchScalarGridSpec(
            num_scalar_prefetch=2,
            grid=(nl, nt),
            in_specs=[
                pl.BlockSpec((pl.Element(1), pl.Element(PACK_TN), pl.Element(d)),
                             lambda l, j, s, m: (l, pl.multiple_of(s[j] * PACK_ALIGN, PACK_ALIGN), 0)),
            ],
            out_specs=pl.BlockSpec((None, PACK_TN, d), lambda l, j, s, m: (l, j, 0)),
        ),
        compiler_params=_cparams(("parallel", "parallel")),
        name="pack_w_in",
    )(src, mode, wt)


def _rope_tables():
    half = HEAD_DIM // 2
    inv = np.float32(ROPE_THETA) ** (-np.arange(half, dtype=np.float32) / np.float32(half))
    ang = (np.arange(SEQ, dtype=np.float32)[:, None] * inv[None, :].astype(np.float32)).astype(np.float32)
    cos = np.cos(ang.astype(np.float64)).astype(np.float32)
    sin = np.sin(ang.astype(np.float64)).astype(np.float32)
    cos_t = np.concatenate([cos, cos, cos, cos], axis=1)
    sin_t = np.concatenate([-sin, sin, -sin, sin], axis=1)
    return jnp.asarray(cos_t), jnp.asarray(sin_t)


def kernel(x, norm1_g, w_in, ssm_lam_re, ssm_lam_im, ssm_log_step, ssm_b_re, ssm_b_im, ssm_c_re, ssm_c_im, ssm_d, ssm_w_glu, ssm_b_glu, att_sinks, gla_w_gate, gla_b_gate, gla_norm_g, w_branch_ssm, w_branch_att, w_branch_gla, w_out, norm2_g, w_ff1, w_ff2, final_norm_g):
    assert x.shape == (1, SEQ, D_MODEL)
    nl = DEPTH
    xs = x.reshape(SEQ, D_MODEL).astype(F32)
    cos_t, sin_t = _rope_tables()
    fg = final_norm_g.reshape(1, D_MODEL).astype(F32)
    w_in_p = _pack_w_in(w_in)
    n1 = norm1_g.reshape(nl, 1, D_MODEL).astype(F32)
    n2 = norm2_g.reshape(nl, 1, D_MODEL).astype(F32)
    k_lhs, k_rhs, m_in, m_out, at_r, at_i = jax.vmap(_s5_prepare)(
        ssm_lam_re, ssm_lam_im, ssm_log_step, ssm_b_re, ssm_b_im, ssm_c_re, ssm_c_im)
    prep = (_s5_toeplitz(k_lhs, k_rhs), m_in, m_out, at_r, at_i)
    d_skip = ssm_d.reshape(nl, 1, SSM_WIDTH).astype(F32)
    b_glu = ssm_b_glu.reshape(nl, 1, SSM_WIDTH).astype(F32)
    sinks = att_sinks.astype(F32)
    wg_pad = jnp.concatenate(
        [gla_w_gate.astype(BF16), jnp.zeros((nl, LANES - GLA_GATE_RANK, GLA_QK_WIDTH), BF16)], axis=1)
    bg = gla_b_gate.reshape(nl, 1, GLA_QK_WIDTH).astype(F32)
    ng = gla_norm_g.reshape(nl, 1, GLA_V_WIDTH).astype(F32)
    small_f32 = [w.astype(F32) for w in (ssm_w_glu, w_branch_ssm, w_branch_att, w_branch_gla, w_out)]
    w_ff1 = w_ff1.astype(F32)
    w_ff2 = w_ff2.astype(F32)
    for l in range(nl):
        proj, uv, (w_glu, w_bs, w_ba, w_bg, w_o) = _inproj(xs, n1, w_in_p, small_f32, l)
        y_ssm = _s5_branch(uv, prep, d_skip, w_glu, b_glu, l)
        y_att = _swa(proj, sinks[l], cos_t, sin_t)
        y_gla = _gla(proj, wg_pad, bg, ng, l)
        xs, w1, w2 = _mix(xs, proj, y_ssm, y_att, y_gla, w_bs, w_ba, w_bg, w_o, w_ff1, w_ff2, l)
        xs = _ffn(xs, n2, w1, w2, fg, l, final_norm=(l == nl - 1))
    return xs.reshape(1, SEQ, D_MODEL)
```

```python
import functools
import math

import jax
import jax.numpy as jnp
import numpy as np
from jax import lax
from jax.experimental import pallas as pl
from jax.experimental.pallas import tpu as pltpu

F32 = jnp.float32
BF16 = jnp.bfloat16

D_MODEL = 2048
SEQ = 16384
DEPTH = 2
SSM_WIDTH = 1024
SSM_GROUP = 16
SSM_GROUPS = 64
SSM_STATE = 64
HEAD_DIM = 64
ATT_HEADS = 16
ATT_KV_HEADS = 4
ATT_GROUP = 4
ATT_Q_WIDTH = 1024
ATT_KV_WIDTH = 256
WINDOW = 128
ATT_BLOCK = 128
ROPE_THETA = 10000.0
GLA_HEADS = 4
GLA_V_WIDTH = 1024
GLA_DV = 256
GLA_DK = 128
GLA_QK_WIDTH = 512
GLA_GATE_RANK = 16
GLA_TAU = 16.0
GLA_CHUNK = 64
D_FF = 8192
EPS = 1e-6

LANES = 128
VMEM_LIMIT = 56 * 1024 * 1024

COL_MERGE = 0
COL_AQ = 6144
COL_GV = 7168
COL_GOUT = 8192
COL_GQ = 9216
COL_GK = 9728
COL_AK = 10240
COL_AV = 10496
COL_GLR = 10752
GLR_PAD = 512
PROJ_WIDTH = 11264
INPROJ_TN = 1024
INPROJ_CAST_STEPS = 4

SSM_T = 16
SSM_NC = SEQ // SSM_T
SSM_TC = SSM_T * SSM_GROUP
SSM_PAIRS = SSM_GROUPS // 2
SSM_STATES = SSM_GROUPS * SSM_STATE
SSM_OCT = LANES // SSM_GROUP
SSM_NOCT = SSM_GROUPS // SSM_OCT
NEG_BIG = -1e30
LOG2E = math.log2(math.e)
ATT_Q_SCALE = HEAD_DIM ** -0.5 * LOG2E


def _cparams(sem, vmem=VMEM_LIMIT):
    return pltpu.CompilerParams(dimension_semantics=sem, vmem_limit_bytes=vmem)


def _rms_rows(x_ref, g_ref, dst_ref, rows, chunk=128):
    g = g_ref[...]

    def body(r, c):
        sl = pl.ds(pl.multiple_of(r * chunk, chunk), chunk)
        xv = x_ref[sl, :].astype(F32)
        ms = jnp.mean(xv * xv, axis=-1, keepdims=True)
        dst_ref[sl, :] = (xv * lax.rsqrt(ms + EPS) * g).astype(dst_ref.dtype)
        return c

    lax.fori_loop(0, rows // chunk, body, 0)


def _rope(x, cos, sin_signed):
    lane = lax.broadcasted_iota(jnp.int32, x.shape, 1) % HEAD_DIM
    swapped = jnp.where(lane < HEAD_DIM // 2,
                        pltpu.roll(x, LANES - HEAD_DIM // 2, axis=1),
                        pltpu.roll(x, HEAD_DIM // 2, axis=1))
    return x * cos + swapped * sin_signed


def _inproj_kernel(x_ref, g_ref, w_ref, *refs, tm, n_cast):
    cast_src = refs[:n_cast]
    o_ref, uv_ref = refs[n_cast:n_cast + 2]
    cast_dst = refs[n_cast + 2:2 * n_cast + 2]
    h_ref, scr_ref = refs[2 * n_cast + 2:]
    j = pl.program_id(1)

    def project():
        return lax.dot_general(h_ref[...], w_ref[...], (((1,), (1,)), ((), ())), preferred_element_type=F32)

    @pl.when(j == 0)
    def _():
        _rms_rows(x_ref, g_ref, h_ref, tm)
        res = project()
        nt = res.shape[1] // LANES
        for k in range(nt):
            scr_ref[k] = res[:, k * LANES:(k + 1) * LANES]
        for t in range(SSM_T):
            for k in range(nt):
                uv_ref[t, :, k * LANES:(k + 1) * LANES] = scr_ref[
                    k, pl.ds(t, tm // SSM_T, stride=SSM_T), :].astype(uv_ref.dtype)

    @pl.when(j > 0)
    def _():
        for src, dst in zip(cast_src, cast_dst):
            dst[...] = src[...].astype(dst.dtype)
        o_ref[...] = project().astype(o_ref.dtype)


def _inproj(x, g, w, cast_weights, layer, tm=1024):
    s, d = x.shape
    tn = INPROJ_TN
    n = w.shape[1]
    tiles = s // tm
    cast_in, cast_out, cast_shapes = [], [], []

    def slab_index(i, j):
        return i * INPROJ_CAST_STEPS + jnp.clip(j - 1, 0, INPROJ_CAST_STEPS - 1)

    for cw in cast_weights:
        _, rows, cols = cw.shape
        slab = rows // (tiles * INPROJ_CAST_STEPS)
        cast_in.append(pl.BlockSpec((None, slab, cols), lambda i, j: (layer, slab_index(i, j), 0)))
        cast_out.append(pl.BlockSpec((slab, cols), lambda i, j: (slab_index(i, j), 0)))
        cast_shapes.append(jax.ShapeDtypeStruct((rows, cols), BF16))
    outs = pl.pallas_call(
        functools.partial(_inproj_kernel, tm=tm, n_cast=len(cast_weights)),
        out_shape=[jax.ShapeDtypeStruct((s, n - tn), BF16),
                   jax.ShapeDtypeStruct((SSM_T, s // SSM_T, SSM_WIDTH), BF16)] + cast_shapes,
        grid=(tiles, n // tn),
        in_specs=[
            pl.BlockSpec((tm, d), lambda i, j: (i, 0)),
            pl.BlockSpec((None, 1, d), lambda i, j: (layer, 0, 0)),
            pl.BlockSpec((None, tn, d), lambda i, j: (layer, j, 0)),
        ] + cast_in,
        out_specs=[pl.BlockSpec((tm, tn), lambda i, j: (i, jnp.maximum(j - 1, 0))),
                   pl.BlockSpec((SSM_T, tm // SSM_T, SSM_WIDTH), lambda i, j: (0, i, 0))] + cast_out,
        scratch_shapes=[pltpu.VMEM((tm, d), BF16), pltpu.VMEM((tn // LANES, tm, LANES), F32)],
        compiler_params=_cparams(("parallel", "arbitrary")),
        name="inproj",
    )(x, g, w, *cast_weights)
    return outs[0], outs[1], outs[2:]


def _s5_prepare(lam_re, lam_im, log_step, b_re, b_im, c_re, c_im):
    t_len = SSM_T
    g_n, p_n = SSM_GROUPS, SSM_STATE
    step = jnp.exp(log_step.astype(F32))[:, None]
    lr = lam_re.astype(F32)
    li = lam_im.astype(F32)
    mag = jnp.exp(lr * step)
    ar = mag * jnp.cos(li * step)
    ai = mag * jnp.sin(li * step)
    den = lr * lr + li * li
    fr = ((ar - 1.0) * lr + ai * li) / den
    fi = (ai * lr - (ar - 1.0) * li) / den
    br = b_re.astype(F32)
    bi = b_im.astype(F32)
    bbr = fr[..., None] * br - fi[..., None] * bi
    bbi = fr[..., None] * bi + fi[..., None] * br
    jj = jnp.arange(t_len + 1, dtype=F32)[None, None, :]
    pmag = jnp.exp((lr * step)[..., None] * jj)
    pang = (li * step)[..., None] * jj
    pw_r = pmag * jnp.cos(pang)
    pw_i = pmag * jnp.sin(pang)
    cr_t = jnp.transpose(c_re.astype(F32), (0, 2, 1))[:, :, None, :]
    ci_t = jnp.transpose(c_im.astype(F32), (0, 2, 1))[:, :, None, :]

    def c_times_powers(first):
        qr = pw_r[:, :, first:first + t_len, None]
        qi = pw_i[:, :, first:first + t_len, None]
        return ((cr_t * qr - ci_t * qi).reshape(g_n, p_n, SSM_TC),
                (cr_t * qi + ci_t * qr).reshape(g_n, p_n, SSM_TC))

    bbr_t = jnp.transpose(bbr, (0, 2, 1))
    bbi_t = jnp.transpose(bbi, (0, 2, 1))
    k_r, k_i = c_times_powers(0)
    k_lhs = jnp.concatenate([bbr_t, bbi_t], axis=-1)
    k_rhs = jnp.concatenate([k_r, -k_i], axis=1)
    d_r, d_i = c_times_powers(1)
    rev_r = jnp.transpose(pw_r, (0, 2, 1))[:, t_len - 1::-1][:, :t_len, None, :]
    rev_i = jnp.transpose(pw_i, (0, 2, 1))[:, t_len - 1::-1][:, :t_len, None, :]
    e_r = (rev_r * bbr_t[:, None] - rev_i * bbi_t[:, None]).reshape(g_n, SSM_TC, p_n)
    e_i = (rev_r * bbi_t[:, None] + rev_i * bbr_t[:, None]).reshape(g_n, SSM_TC, p_n)
    z_in = jnp.zeros((SSM_PAIRS, SSM_TC, p_n), F32)
    e_r2 = e_r.reshape(SSM_PAIRS, 2, SSM_TC, p_n)
    e_i2 = e_i.reshape(SSM_PAIRS, 2, SSM_TC, p_n)
    min_top = jnp.concatenate([e_r2[:, 0], z_in, e_i2[:, 0], z_in], axis=-1)
    min_bot = jnp.concatenate([z_in, e_r2[:, 1], z_in, e_i2[:, 1]], axis=-1)
    m_in = jnp.concatenate([min_top, min_bot], axis=1)
    z_out = jnp.zeros((SSM_PAIRS, p_n, SSM_TC), F32)
    d_r2 = d_r.reshape(SSM_PAIRS, 2, p_n, SSM_TC)
    d_i2 = d_i.reshape(SSM_PAIRS, 2, p_n, SSM_TC)
    m_out = jnp.concatenate([
        jnp.concatenate([d_r2[:, 0], z_out], axis=-1),
        jnp.concatenate([z_out, d_r2[:, 1]], axis=-1),
        jnp.concatenate([-d_i2[:, 0], z_out], axis=-1),
        jnp.concatenate([z_out, -d_i2[:, 1]], axis=-1)], axis=1)
    at_r = pw_r[:, :, t_len].reshape(1, SSM_STATES)
    at_i = pw_i[:, :, t_len].reshape(1, SSM_STATES)
    return k_lhs, k_rhs, m_in.astype(BF16), m_out.astype(BF16), at_r, at_i


def _s5_toeplitz_kernel(a_ref, b_ref, o_ref):
    lane = lax.broadcasted_iota(jnp.int32, (SSM_GROUP, SSM_TC), 1)

    def body(g, c):
        kr = jnp.dot(a_ref[g], b_ref[g], preferred_element_type=F32,
                     precision=lax.Precision.HIGHEST)
        for s in range(SSM_T):
            shifted = pltpu.roll(kr, s * SSM_GROUP, axis=1) if s else kr
            o_ref[g, s * SSM_GROUP:(s + 1) * SSM_GROUP, :] = jnp.where(
                lane >= s * SSM_GROUP, shifted, 0.0).astype(o_ref.dtype)
        return c

    lax.fori_loop(0, SSM_GROUPS, body, 0)


def _s5_toeplitz(k_lhs, k_rhs):
    nl = k_lhs.shape[0]
    return pl.pallas_call(
        _s5_toeplitz_kernel,
        out_shape=jax.ShapeDtypeStruct((nl, SSM_GROUPS, SSM_TC, SSM_TC), BF16),
        grid=(nl,),
        in_specs=[pl.BlockSpec((None, SSM_GROUPS, SSM_GROUP, 2 * SSM_STATE), lambda l: (l, 0, 0, 0)),
                  pl.BlockSpec((None, SSM_GROUPS, 2 * SSM_STATE, SSM_TC), lambda l: (l, 0, 0, 0))],
        out_specs=pl.BlockSpec((None, SSM_GROUPS, SSM_TC, SSM_TC), lambda l: (l, 0, 0, 0)),
        compiler_params=_cparams(("parallel",)),
        name="s5_toeplitz",
    )(k_lhs, k_rhs)


def _granule_transpose(src):
    x = list(src)
    grp = lax.broadcasted_iota(jnp.int32, x[0].shape, 1) // SSM_GROUP
    stride = SSM_OCT // 2
    while stride:
        upper = (grp & stride) != 0
        for a in range(SSM_OCT):
            if a & stride:
                continue
            lo, hi = x[a], x[a + stride]
            x[a] = jnp.where(upper, pltpu.roll(hi, stride * SSM_GROUP, axis=1), lo)
            x[a + stride] = jnp.where(upper, hi, pltpu.roll(lo, LANES - stride * SSM_GROUP, axis=1))
        stride //= 2
    return x


S5_RELAYOUT_ROWS = 64
S5_RELAYOUT_UNROLL = 1


def _s5_x_kernel(uv_ref, min_ref, u_ref, xr_ref, xi_ref):
    rb = S5_RELAYOUT_ROWS

    def body(r, c):
        rows = pl.ds(pl.multiple_of(r * rb, rb), rb)
        for half in range(2):
            src = [uv_ref[half * SSM_OCT + tp, rows, :].astype(F32) for tp in range(SSM_OCT)]
            out = _granule_transpose(src)
            for g in range(SSM_OCT):
                u_ref[g, rows, half * LANES:(half + 1) * LANES] = out[g].astype(BF16)
        return c

    lax.fori_loop(0, SSM_NC // rb, body, 0, unroll=S5_RELAYOUT_UNROLL)
    for pr in range(SSM_OCT // 2):
        up = jnp.concatenate([u_ref[2 * pr], u_ref[2 * pr + 1]], axis=1)
        x = jnp.dot(up, min_ref[pr], preferred_element_type=F32)
        xr_ref[:, pr * LANES:(pr + 1) * LANES] = x[:, :LANES]
        xi_ref[:, pr * LANES:(pr + 1) * LANES] = x[:, LANES:]


def _s5_x(uv, m_in, layer):
    nc = SSM_NC
    ppo = SSM_OCT // 2
    sw = ppo * LANES
    return pl.pallas_call(
        _s5_x_kernel,
        out_shape=[jax.ShapeDtypeStruct((SSM_GROUPS, nc, SSM_TC), BF16),
                   jax.ShapeDtypeStruct((nc, SSM_STATES), F32),
                   jax.ShapeDtypeStruct((nc, SSM_STATES), F32)],
        grid=(SSM_NOCT,),
        in_specs=[
            pl.BlockSpec((SSM_T, nc, LANES), lambda k: (0, 0, k)),
            pl.BlockSpec((None, ppo, 2 * SSM_TC, 256), lambda k: (layer, k, 0, 0)),
        ],
        out_specs=[pl.BlockSpec((SSM_OCT, nc, SSM_TC), lambda k: (k, 0, 0)),
                   pl.BlockSpec((nc, sw), lambda k: (0, k)),
                   pl.BlockSpec((nc, sw), lambda k: (0, k))],
        compiler_params=_cparams(("parallel",)),
        name="s5_x",
    )(uv, m_in)


def _s5_scan_kernel(xr_ref, xi_ref, ar_ref, ai_ref, hr_ref, hi_ref, *, nc):
    ar = ar_ref[...]
    ai = ai_ref[...]

    def body(n, carry):
        hr, hi = carry
        row = pl.ds(n, 1)
        hr_ref[row, :] = hr
        hi_ref[row, :] = hi
        xr = xr_ref[row, :]
        xi = xi_ref[row, :]
        return (ar * hr - ai * hi + xr, ar * hi + ai * hr + xi)

    zero = jnp.zeros_like(ar)
    lax.fori_loop(0, nc, body, (zero, zero), unroll=8)


def _s5_scan(xr, xi, at_r, at_i, layer, lb=1024):
    nc, ns = xr.shape
    blk = pl.BlockSpec((nc, lb), lambda i: (0, i))
    tab = pl.BlockSpec((None, 1, lb), lambda i: (layer, 0, i))
    return pl.pallas_call(
        functools.partial(_s5_scan_kernel, nc=nc),
        out_shape=[jax.ShapeDtypeStruct((nc, ns), F32)] * 2,
        grid=(ns // lb,),
        in_specs=[blk, blk, tab, tab],
        out_specs=[blk, blk],
        compiler_params=_cparams(("parallel",)),
        name="s5_scan",
    )(xr, xi, at_r, at_i)


def _s5_y_kernel(u_ref, hr_ref, hi_ref, toep_ref, mout_ref, yv_ref, y_scr):
    for pr in range(SSM_OCT // 2):
        hcat = jnp.concatenate([hr_ref[:, pr * LANES:(pr + 1) * LANES],
                                hi_ref[:, pr * LANES:(pr + 1) * LANES]], axis=1).astype(BF16)
        carry = jnp.dot(hcat, mout_ref[pr], preferred_element_type=F32)
        for k in range(2):
            g = 2 * pr + k
            y = jnp.dot(u_ref[g], toep_ref[g], preferred_element_type=F32)
            y_scr[g] = y + carry[:, k * SSM_TC:(k + 1) * SSM_TC]

    rb = S5_RELAYOUT_ROWS

    def body(r, c):
        rows = pl.ds(pl.multiple_of(r * rb, rb), rb)
        for half in range(2):
            src = [y_scr[g, rows, half * LANES:(half + 1) * LANES] for g in range(SSM_OCT)]
            out = _granule_transpose(src)
            for tp in range(SSM_OCT):
                yv_ref[half * SSM_OCT + tp, rows, :] = out[tp].astype(BF16)
        return c

    lax.fori_loop(0, SSM_NC // rb, body, 0, unroll=S5_RELAYOUT_UNROLL)


def _s5_y(u_g, hr, hi, toep, m_out, layer):
    nc = SSM_NC
    ppo = SSM_OCT // 2
    hblk = pl.BlockSpec((nc, ppo * LANES), lambda k: (0, k))
    return pl.pallas_call(
        _s5_y_kernel,
        out_shape=jax.ShapeDtypeStruct((SSM_T, nc, SSM_WIDTH), BF16),
        grid=(SSM_NOCT,),
        in_specs=[
            pl.BlockSpec((SSM_OCT, nc, SSM_TC), lambda k: (k, 0, 0)),
            hblk, hblk,
            pl.BlockSpec((None, SSM_OCT, SSM_TC, SSM_TC), lambda k: (layer, k, 0, 0)),
            pl.BlockSpec((None, ppo, 256, 2 * SSM_TC), lambda k: (layer, k, 0, 0)),
        ],
        out_specs=pl.BlockSpec((SSM_T, nc, LANES), lambda k: (0, 0, k)),
        scratch_shapes=[pltpu.VMEM((SSM_OCT, nc, SSM_TC), F32)],
        compiler_params=_cparams(("parallel",)),
        name="s5_y",
    )(u_g, hr, hi, toep, m_out)


def _s5_glu_kernel(yv_ref, uv_ref, d_ref, w_ref, b_ref, o_ref, z_scr, zb_scr, g_scr, nat_scr, *, ncb):
    d = d_ref[...]
    for t in range(SSM_T):
        z = jax.nn.gelu(yv_ref[t].astype(F32) + d * uv_ref[t].astype(F32))
        z_scr[t * ncb:(t + 1) * ncb, :] = z
        zb_scr[t * ncb:(t + 1) * ncb, :] = z.astype(BF16)
    g_scr[...] = jnp.dot(zb_scr[...], w_ref[...], preferred_element_type=F32)
    b = b_ref[...]
    nt = SSM_WIDTH // LANES
    for t in range(SSM_T):
        rows = slice(t * ncb, (t + 1) * ncb)
        out = z_scr[rows, :] * jax.nn.sigmoid(g_scr[rows, :] + b)
        for k in range(nt):
            nat_scr[k, pl.ds(t, ncb, stride=SSM_T), :] = out[:, k * LANES:(k + 1) * LANES]
    for k in range(nt):
        o_ref[:, k * LANES:(k + 1) * LANES] = nat_scr[k].astype(o_ref.dtype)


def _s5_glu(yv, uv, d_skip, w_glu, b_glu, layer, ncb=64):
    w = SSM_WIDTH
    tm = ncb * SSM_T
    vblk = pl.BlockSpec((SSM_T, ncb, w), lambda i: (0, i, 0))
    vec = pl.BlockSpec((None, 1, w), lambda i: (layer, 0, 0))
    return pl.pallas_call(
        functools.partial(_s5_glu_kernel, ncb=ncb),
        out_shape=jax.ShapeDtypeStruct((SEQ, w), BF16),
        grid=(SSM_NC // ncb,),
        in_specs=[vblk, vblk, vec,
                  pl.BlockSpec((w, w), lambda i: (0, 0)),
                  vec],
        out_specs=pl.BlockSpec((tm, w), lambda i: (i, 0)),
        scratch_shapes=[pltpu.VMEM((tm, w), F32), pltpu.VMEM((tm, w), BF16), pltpu.VMEM((tm, w), F32),
                        pltpu.VMEM((w // LANES, tm, LANES), F32)],
        compiler_params=_cparams(("parallel",)),
        name="s5_glu",
    )(yv, uv, d_skip, w_glu, b_glu)


def _s5_branch(uv, prep, d_skip, w_glu, b_glu, layer):
    toep, m_in, m_out, at_r, at_i = prep
    u_g, xr, xi = _s5_x(uv, m_in, layer)
    hr, hi = _s5_scan(xr, xi, at_r, at_i, layer)
    yv = _s5_y(u_g, hr, hi, toep, m_out, layer)
    return _s5_glu(yv, uv, d_skip, w_glu, b_glu, layer)


def _swa_kernel(sink_ref, q_ref, k_ref, v_ref, cos_ref, sin_ref, o_ref, s_scr, p_scr, kpad_scr, vext_scr):
    n = pl.program_id(0)
    blk = ATT_BLOCK
    hpt = LANES // HEAD_DIM
    nvar = ATT_KV_HEADS * hpt
    tiles_per_kv = ATT_GROUP // hpt
    row = lax.broadcasted_iota(jnp.int32, (blk, blk), 0)
    col = lax.broadcasted_iota(jnp.int32, (blk, blk), 1)

    @pl.when(n == 0)
    def _():
        kpad_scr[...] = jnp.zeros_like(kpad_scr)
        vext_scr[:, :, :LANES] = jnp.zeros((nvar, 2 * blk, LANES), BF16)
        vext_scr[:, :, LANES:] = jnp.ones((nvar, 2 * blk, LANES), BF16)

    @pl.when(n > 0)
    def _():
        for i in range(nvar):
            kpad_scr[i, :blk, :] = kpad_scr[i, blk:, :]
            vext_scr[i, :blk, :LANES] = jnp.where(row == 0, jnp.zeros((), BF16), vext_scr[i, blk:, :LANES])

    lo = lax.broadcasted_iota(jnp.int32, (blk, LANES), 1) < HEAD_DIM

    def lane_padded(tile, half_in):
        other = pltpu.roll(tile, HEAD_DIM, axis=1)
        at_lo, at_hi = (tile, other) if half_in == 0 else (other, tile)
        return jnp.where(lo, at_lo, 0.0).astype(BF16), jnp.where(lo, 0.0, at_hi).astype(BF16)

    cos = cos_ref[...]
    sin = sin_ref[...]
    for kv_tile in range(ATT_KV_HEADS // hpt):
        lanes = slice(kv_tile * LANES, (kv_tile + 1) * LANES)
        kt = _rope(k_ref[:, lanes].astype(F32), cos, sin)
        vt = v_ref[:, lanes].astype(F32)
        for half in range(hpt):
            h = kv_tile * hpt + half
            k_pad = lane_padded(kt, half)
            v_pad = lane_padded(vt, half)
            for e in range(hpt):
                kpad_scr[h * hpt + e, blk:, :] = k_pad[e]
                vext_scr[h * hpt + e, blk:, :LANES] = v_pad[e]

    cur = col <= row
    sink_slot = (col == 0).astype(F32)
    prev_bias = jnp.where(n > 0, 0.0, NEG_BIG)

    for tile in range(ATT_HEADS // hpt):
        qt = _rope(q_ref[:, tile * LANES:(tile + 1) * LANES].astype(F32) * ATT_Q_SCALE, cos, sin).astype(BF16)
        for e in range(hpt):
            var = (tile // tiles_per_kv) * hpt + e
            s_scr[tile * hpt + e] = lax.dot_general(qt, kpad_scr[var], (((1,), (1,)), ((), ())),
                                                    preferred_element_type=F32)

    for hd in range(ATT_HEADS):
        sk = sink_ref[hd] * LOG2E
        s = jnp.where(cur, s_scr[hd, :, blk:], s_scr[hd, :, :blk] + prev_bias)
        m = jnp.maximum(jnp.max(s, axis=-1, keepdims=True), sk)
        p = jnp.exp2(s - m)
        p_scr[hd, :, :blk] = jnp.where(cur, jnp.exp2(sk - m) * sink_slot, p).astype(BF16)
        p_scr[hd, :, blk:] = jnp.where(cur, p, 0.0).astype(BF16)

    for tile in range(ATT_HEADS // hpt):
        acc = None
        for e in range(hpt):
            var = (tile // tiles_per_kv) * hpt + e
            oe = jnp.dot(p_scr[tile * hpt + e], vext_scr[var], preferred_element_type=F32)
            term = oe[:, :LANES] / oe[:, LANES:]
            acc = term if acc is None else acc + term
        o_ref[:, tile * LANES:(tile + 1) * LANES] = acc.astype(o_ref.dtype)


def _swa(proj, sinks, cos_t, sin_t):
    s = proj.shape[0]
    blk = ATT_BLOCK
    nvar = ATT_KV_HEADS * (LANES // HEAD_DIM)
    tab = pl.BlockSpec((blk, LANES), lambda n: (n, 0))
    return pl.pallas_call(
        _swa_kernel,
        out_shape=jax.ShapeDtypeStruct((s, ATT_Q_WIDTH), BF16),
        grid=(s // blk,),
        in_specs=[
            pl.BlockSpec(memory_space=pltpu.SMEM),
            pl.BlockSpec((blk, ATT_Q_WIDTH), lambda n: (n, COL_AQ // ATT_Q_WIDTH)),
            pl.BlockSpec((blk, ATT_KV_WIDTH), lambda n: (n, COL_AK // ATT_KV_WIDTH)),
            pl.BlockSpec((blk, ATT_KV_WIDTH), lambda n: (n, COL_AV // ATT_KV_WIDTH)),
            tab, tab,
        ],
        out_specs=pl.BlockSpec((blk, ATT_Q_WIDTH), lambda n: (n, 0)),
        scratch_shapes=[pltpu.VMEM((ATT_HEADS, blk, 2 * blk), F32), pltpu.VMEM((ATT_HEADS, blk, 2 * blk), BF16),
                        pltpu.VMEM((nvar, 2 * blk, LANES), BF16), pltpu.VMEM((nvar, 2 * blk, 2 * LANES), BF16)],
        compiler_params=_cparams(("arbitrary",)),
        name="swa",
    )(sinks, proj, proj, proj, cos_t, sin_t)


def _log_sigmoid(x):
    return jnp.minimum(x, 0.0) - jnp.log(1.0 + jnp.exp(-jnp.abs(x)))


def _gla_kernel(q_ref, k_ref, v_ref, glr_ref, gout_ref, wg_ref, bg_ref, ng_ref, tri_ref, o_ref,
                st_ref, qin_scr, att_scr, upd_scr, sb_scr, *, rb):
    @pl.when(pl.program_id(0) == 0)
    def _():
        st_ref[...] = jnp.zeros_like(st_ref)

    ch = GLA_CHUNK
    ncb = rb // ch
    nh = GLA_HEADS
    logit = jnp.dot(glr_ref[...], wg_ref[...], preferred_element_type=F32) + bg_ref[...]
    log_a = _log_sigmoid(logit) * (1.0 / GLA_TAU)
    tri = tri_ref[...]
    a_hi = log_a.astype(BF16)
    a_lo = (log_a - a_hi.astype(F32)).astype(BF16)
    bcum = (jnp.dot(tri, a_hi, preferred_element_type=F32)
            + jnp.dot(tri, a_lo, preferred_element_type=F32))
    qin_scr[...] = (q_ref[...].astype(F32) * (GLA_DK ** -0.5) * jnp.exp(bcum)).astype(BF16)
    k = k_ref[...].astype(F32)
    k_in = (k * jnp.exp(-bcum)).astype(BF16)
    causal = (lax.broadcasted_iota(jnp.int32, (ch, ch), 1)
              <= lax.broadcasted_iota(jnp.int32, (ch, ch), 0))

    decays = []
    for c in range(ncb):
        rows = slice(c * ch, (c + 1) * ch)
        b_last = bcum[(c + 1) * ch - 1:(c + 1) * ch, :]
        k_dec = (k[rows] * jnp.exp(b_last - bcum[rows])).astype(BF16)
        decays.append(jnp.exp(b_last))
        for h in range(nh):
            ks = slice(h * GLA_DK, (h + 1) * GLA_DK)
            att = lax.dot_general(qin_scr[rows, ks], k_in[rows, ks], (((1,), (1,)), ((), ())),
                                  preferred_element_type=F32)
            att_scr[c * nh + h] = jnp.where(causal, att, 0.0).astype(BF16)
            upd_scr[c * nh + h] = lax.dot_general(v_ref[rows, h * GLA_DV:(h + 1) * GLA_DV], k_dec[:, ks],
                                                  (((0,), (0,)), ((), ())), preferred_element_type=F32)

    for h in range(nh):
        ks = slice(h * GLA_DK, (h + 1) * GLA_DK)
        st = st_ref[h]
        for c in range(ncb):
            sb_scr[c * nh + h] = st.astype(BF16)
            st = st * decays[c][:, ks] + upd_scr[c * nh + h]
        st_ref[h] = st

    for c in range(ncb):
        rows = slice(c * ch, (c + 1) * ch)
        for h in range(nh):
            ks = slice(h * GLA_DK, (h + 1) * GLA_DK)
            vs = slice(h * GLA_DV, (h + 1) * GLA_DV)
            o = jnp.dot(att_scr[c * nh + h], v_ref[rows, vs], preferred_element_type=F32)
            o = o + lax.dot_general(qin_scr[rows, ks], sb_scr[c * nh + h], (((1,), (1,)), ((), ())),
                                    preferred_element_type=F32)
            o = o * lax.rsqrt(jnp.mean(o * o, axis=-1, keepdims=True) + EPS)
            o = o * ng_ref[:, vs]
            go = gout_ref[rows, vs].astype(F32)
            o = o * (go * jax.nn.sigmoid(go))
            o_ref[rows, vs] = o.astype(o_ref.dtype)


def _gla(proj, w_gate, b_gate, norm_g, layer, rb=256):
    s = proj.shape[0]
    ch = GLA_CHUNK
    idx = np.arange(rb)
    tri = jnp.asarray((idx[None, :] <= idx[:, None]) & (idx[None, :] // ch == idx[:, None] // ch), BF16)
    nblk = (rb // ch) * GLA_HEADS
    return pl.pallas_call(
        functools.partial(_gla_kernel, rb=rb),
        out_shape=jax.ShapeDtypeStruct((s, GLA_V_WIDTH), BF16),
        grid=(s // rb,),
        in_specs=[
            pl.BlockSpec((rb, GLA_QK_WIDTH), lambda i: (i, COL_GQ // GLA_QK_WIDTH)),
            pl.BlockSpec((rb, GLA_QK_WIDTH), lambda i: (i, COL_GK // GLA_QK_WIDTH)),
            pl.BlockSpec((rb, GLA_V_WIDTH), lambda i: (i, COL_GV // GLA_V_WIDTH)),
            pl.BlockSpec((rb, LANES), lambda i: (i, COL_GLR // LANES)),
            pl.BlockSpec((rb, GLA_V_WIDTH), lambda i: (i, COL_GOUT // GLA_V_WIDTH)),
            pl.BlockSpec((None, LANES, GLA_QK_WIDTH), lambda i: (layer, 0, 0)),
            pl.BlockSpec((None, 1, GLA_QK_WIDTH), lambda i: (layer, 0, 0)),
            pl.BlockSpec((None, 1, GLA_V_WIDTH), lambda i: (layer, 0, 0)),
            pl.BlockSpec((rb, rb), lambda i: (0, 0)),
        ],
        out_specs=pl.BlockSpec((rb, GLA_V_WIDTH), lambda i: (i, 0)),
        scratch_shapes=[pltpu.VMEM((GLA_HEADS, GLA_DV, GLA_DK), F32),
                        pltpu.VMEM((rb, GLA_QK_WIDTH), BF16),
                        pltpu.VMEM((nblk, ch, ch), BF16),
                        pltpu.VMEM((nblk, GLA_DV, GLA_DK), F32),
                        pltpu.VMEM((nblk, GLA_DV, GLA_DK), BF16)],
        compiler_params=_cparams(("arbitrary",)),
        name="gla",
    )(proj, proj, proj, proj, proj, w_gate, b_gate, norm_g, tri)


MIX_TN = 512


def _mix_kernel(x_ref, ys_ref, ya_ref, yg_ref, gate_ref, ws_ref, wa_ref, wg_ref, wo_ref, w1f_ref, w2f_ref,
                o_ref, w1b_ref, w2b_ref, m_scr):
    d = D_MODEL
    w1b_ref[...] = w1f_ref[...].astype(BF16)
    w2b_ref[...] = w2f_ref[...].astype(BF16)
    for j in range(d // MIX_TN):
        cols = slice(j * MIX_TN, (j + 1) * MIX_TN)
        m = None
        for k, (y_ref, w_ref) in enumerate(((ys_ref, ws_ref), (ya_ref, wa_ref), (yg_ref, wg_ref))):
            proj = jnp.dot(y_ref[...], w_ref[:, cols], preferred_element_type=F32)
            gate = gate_ref[:, k * d + j * MIX_TN:k * d + (j + 1) * MIX_TN].astype(F32)
            term = jax.nn.sigmoid(gate) * proj
            m = term if m is None else m + term
        m_scr[:, cols] = m.astype(BF16)
    o_ref[...] = x_ref[...] + jnp.dot(m_scr[...], wo_ref[...], preferred_element_type=F32)


def _mix(x, proj, y_ssm, y_att, y_gla, w_bs, w_ba, w_bg, w_out, w_ff1, w_ff2, layer, tm=256):
    s, d = x.shape
    bw = y_ssm.shape[1]
    dff = w_ff1.shape[2]
    steps = s // tm
    r1, r2 = d // steps, dff // steps
    yblk = pl.BlockSpec((tm, bw), lambda i: (i, 0))
    wblk = pl.BlockSpec((bw, d), lambda i: (0, 0), pipeline_mode=pl.Buffered(1))
    return pl.pallas_call(
        _mix_kernel,
        out_shape=[jax.ShapeDtypeStruct((s, d), F32),
                   jax.ShapeDtypeStruct((d, dff), BF16),
                   jax.ShapeDtypeStruct((dff, d), BF16)],
        grid=(steps,),
        in_specs=[
            pl.BlockSpec((tm, d), lambda i: (i, 0)),
            yblk, yblk, yblk,
            pl.BlockSpec((tm, 3 * d), lambda i: (i, COL_MERGE // (3 * d))),
            wblk, wblk, wblk,
            pl.BlockSpec((d, d), lambda i: (0, 0), pipeline_mode=pl.Buffered(1)),
            pl.BlockSpec((None, r1, dff), lambda i: (layer, i, 0)),
            pl.BlockSpec((None, r2, d), lambda i: (layer, i, 0)),
        ],
        out_specs=[pl.BlockSpec((tm, d), lambda i: (i, 0)),
                   pl.BlockSpec((r1, dff), lambda i: (i, 0)),
                   pl.BlockSpec((r2, d), lambda i: (i, 0))],
        scratch_shapes=[pltpu.VMEM((tm, d), BF16)],
        compiler_params=_cparams(("parallel",)),
        name="mix",
    )(x, y_ssm, y_att, y_gla, proj, w_bs, w_ba, w_bg, w_out, w_ff1, w_ff2)


FFN_SLABS = 2


def _ffn_kernel(x_ref, g_ref, w1_ref, w2_ref, fg_ref, o_ref, h_ref, *, tm, nf, final_norm):
    f = pl.program_id(1)

    @pl.when(f == 0)
    def _():
        _rms_rows(x_ref, g_ref, h_ref, tm)
        o_ref[...] = x_ref[...]

    slab = tm // FFN_SLABS
    for r in range(FFN_SLABS):
        rows = slice(r * slab, (r + 1) * slab)
        a = jnp.dot(h_ref[rows, :], w1_ref[...], preferred_element_type=F32)
        a = jnp.square(jnp.maximum(a, 0.0)).astype(BF16)
        o_ref[rows, :] += jnp.dot(a, w2_ref[...], preferred_element_type=F32)

    if final_norm:
        @pl.when(f == nf - 1)
        def _():
            _rms_rows(o_ref, fg_ref, o_ref, tm)


def _ffn(x, g, w1, w2, fg, layer, final_norm, tm=1024, tf=512):
    s, d = x.shape
    dff = w1.shape[1]
    nf = dff // tf
    return pl.pallas_call(
        functools.partial(_ffn_kernel, tm=tm, nf=nf, final_norm=final_norm),
        out_shape=jax.ShapeDtypeStruct((s, d), F32),
        grid=(s // tm, nf),
        in_specs=[
            pl.BlockSpec((tm, d), lambda i, f: (i, 0)),
            pl.BlockSpec((None, 1, d), lambda i, f: (layer, 0, 0)),
            pl.BlockSpec((d, tf), lambda i, f: (0, f)),
            pl.BlockSpec((tf, d), lambda i, f: (f, 0)),
            pl.BlockSpec((1, d), lambda i, f: (0, 0)),
        ],
        out_specs=pl.BlockSpec((tm, d), lambda i, f: (i, 0)),
        scratch_shapes=[pltpu.VMEM((tm, d), BF16)],
        compiler_params=_cparams(("parallel", "arbitrary")),
        name="ffn",
    )(x, g, w1, w2, fg)


PACK_TN = 512
PACK_ALIGN = 16
PACK_COPY, PACK_GLR = 0, 1


def _pack_tables():
    sizes = [SSM_WIDTH, ATT_Q_WIDTH, ATT_KV_WIDTH, ATT_KV_WIDTH, GLA_QK_WIDTH, GLA_QK_WIDTH,
             GLA_V_WIDTH, GLA_GATE_RANK, GLA_V_WIDTH, 3 * D_MODEL]
    o_u, o_aq, o_ak, o_av, o_gq, o_gk, o_gv, o_glr, o_gout, o_merge = np.concatenate([[0], np.cumsum(sizes)])[:-1]
    assert o_av == o_ak + ATT_KV_WIDTH
    pieces = [(o_u, SSM_WIDTH), (o_merge, 3 * D_MODEL), (o_aq, ATT_Q_WIDTH), (o_gv, GLA_V_WIDTH),
              (o_gout, GLA_V_WIDTH), (o_gq, GLA_QK_WIDTH), (o_gk, GLA_QK_WIDTH), (o_ak, 2 * ATT_KV_WIDTH)]
    src, mode = [], []
    for start, width in pieces:
        assert start % PACK_ALIGN == 0 and width % PACK_TN == 0
        for k in range(width // PACK_TN):
            src.append((start + k * PACK_TN) // PACK_ALIGN)
            mode.append(PACK_COPY)
    assert o_glr % PACK_ALIGN == 0 and o_glr + PACK_TN <= sum(sizes)
    src.append(o_glr // PACK_ALIGN)
    mode.append(PACK_GLR)
    assert len(mode) * PACK_TN == INPROJ_TN + PROJ_WIDTH
    return [jnp.asarray(np.array(t, np.int32)) for t in (src, mode)]


def _pack_kernel(src, mode, a_ref, o_ref, *, chunk=128):
    md = mode[pl.program_id(1)]

    def for_chunks(fn):
        def body(r, c):
            sl = pl.ds(pl.multiple_of(r * chunk, chunk), chunk)
            o_ref[sl, :] = fn(r, a_ref[0, sl, :]).astype(o_ref.dtype)
            return c
        lax.fori_loop(0, PACK_TN // chunk, body, 0)

    @pl.when(md == PACK_COPY)
    def _():
        for_chunks(lambda r, a: a)

    @pl.when(md == PACK_GLR)
    def _():
        row = lax.broadcasted_iota(jnp.int32, (chunk, a_ref.shape[2]), 0)
        for_chunks(lambda r, a: jnp.where(row + r * chunk < GLA_GATE_RANK, a, 0.0))


def _pack_w_in(w):
    nl, d, _ = w.shape
    wt = jnp.swapaxes(w, 1, 2)
    src, mode = _pack_tables()
    nt = (INPROJ_TN + PROJ_WIDTH) // PACK_TN
    return pl.pallas_call(
        _pack_kernel,
        out_shape=jax.ShapeDtypeStruct((nl, nt * PACK_TN, d), BF16),
        grid_spec=pltpu.PrefetchScalarGridSpec(
            num_scalar_prefetch=2,
            grid=(nl, nt),
            in_specs=[
                pl.BlockSpec((pl.Element(1), pl.Element(PACK_TN), pl.Element(d)),
                             lambda l, j, s, m: (l, pl.multiple_of(s[j] * PACK_ALIGN, PACK_ALIGN), 0)),
            ],
            out_specs=pl.BlockSpec((None, PACK_TN, d), lambda l, j, s, m: (l, j, 0)),
        ),
        compiler_params=_cparams(("parallel", "parallel")),
        name="pack_w_in",
    )(src, mode, wt)


def _rope_tables():
    half = HEAD_DIM // 2
    inv = np.float32(ROPE_THETA) ** (-np.arange(half, dtype=np.float32) / np.float32(half))
    ang = (np.arange(SEQ, dtype=np.float32)[:, None] * inv[None, :].astype(np.float32)).astype(np.float32)
    cos = np.cos(ang.astype(np.float64)).astype(np.float32)
    sin = np.sin(ang.astype(np.float64)).astype(np.float32)
    cos_t = np.concatenate([cos, cos, cos, cos], axis=1)
    sin_t = np.concatenate([-sin, sin, -sin, sin], axis=1)
    return jnp.asarray(cos_t), jnp.asarray(sin_t)


def kernel(x, norm1_g, w_in, ssm_lam_re, ssm_lam_im, ssm_log_step, ssm_b_re, ssm_b_im, ssm_c_re, ssm_c_im, ssm_d, ssm_w_glu, ssm_b_glu, att_sinks, gla_w_gate, gla_b_gate, gla_norm_g, w_branch_ssm, w_branch_att, w_branch_gla, w_out, norm2_g, w_ff1, w_ff2, final_norm_g):
    assert x.shape == (1, SEQ, D_MODEL)
    nl = DEPTH
    xs = x.reshape(SEQ, D_MODEL).astype(F32)
    cos_t, sin_t = _rope_tables()
    fg = final_norm_g.reshape(1, D_MODEL).astype(F32)
    w_in_p = _pack_w_in(w_in)
    n1 = norm1_g.reshape(nl, 1, D_MODEL).astype(F32)
    n2 = norm2_g.reshape(nl, 1, D_MODEL).astype(F32)
    k_lhs, k_rhs, m_in, m_out, at_r, at_i = jax.vmap(_s5_prepare)(
        ssm_lam_re, ssm_lam_im, ssm_log_step, ssm_b_re, ssm_b_im, ssm_c_re, ssm_c_im)
    prep = (_s5_toeplitz(k_lhs, k_rhs), m_in, m_out, at_r, at_i)
    d_skip = ssm_d.reshape(nl, 1, SSM_WIDTH).astype(F32)
    b_glu = ssm_b_glu.reshape(nl, 1, SSM_WIDTH).astype(F32)
    sinks = att_sinks.astype(F32)
    wg_pad = jnp.concatenate(
        [gla_w_gate.astype(BF16), jnp.zeros((nl, LANES - GLA_GATE_RANK, GLA_QK_WIDTH), BF16)], axis=1)
    bg = gla_b_gate.reshape(nl, 1, GLA_QK_WIDTH).astype(F32)
    ng = gla_norm_g.reshape(nl, 1, GLA_V_WIDTH).astype(F32)
    small_f32 = [w.astype(F32) for w in (ssm_w_glu, w_branch_ssm, w_branch_att, w_branch_gla, w_out)]
    w_ff1 = w_ff1.astype(F32)
    w_ff2 = w_ff2.astype(F32)
    for l in range(nl):
        proj, uv, (w_glu, w_bs, w_ba, w_bg, w_o) = _inproj(xs, n1, w_in_p, small_f32, l)
        y_ssm = _s5_branch(uv, prep, d_skip, w_glu, b_glu, l)
        y_att = _swa(proj, sinks[l], cos_t, sin_t)
        y_gla = _gla(proj, wg_pad, bg, ng, l)
        xs, w1, w2 = _mix(xs, proj, y_ssm, y_att, y_gla, w_bs, w_ba, w_bg, w_o, w_ff1, w_ff2, l)
        xs = _ffn(xs, n2, w1, w2, fg, l, final_norm=(l == nl - 1))
    return xs.reshape(1, SEQ, D_MODEL)
```

```python
import functools
import math

import jax
import jax.numpy as jnp
import numpy as np
from jax import lax
from jax.experimental import pallas as pl
from jax.experimental.pallas import tpu as pltpu

F32 = jnp.float32
BF16 = jnp.bfloat16

D_MODEL = 2048
SEQ = 16384
DEPTH = 2
SSM_WIDTH = 1024
SSM_GROUP = 16
SSM_GROUPS = 64
SSM_STATE = 64
HEAD_DIM = 64
ATT_HEADS = 16
ATT_KV_HEADS = 4
ATT_GROUP = 4
ATT_Q_WIDTH = 1024
ATT_KV_WIDTH = 256
WINDOW = 128
ATT_BLOCK = 128
ROPE_THETA = 10000.0
GLA_HEADS = 4
GLA_V_WIDTH = 1024
GLA_DV = 256
GLA_DK = 128
GLA_QK_WIDTH = 512
GLA_GATE_RANK = 16
GLA_TAU = 16.0
GLA_CHUNK = 64
EPS = 1e-6

LANES = 128
VMEM_LIMIT = 56 * 1024 * 1024

COL_MERGE = 0
COL_AQ = 6144
COL_GV = 7168
COL_GOUT = 8192
COL_GQ = 9216
COL_GK = 9728
COL_AK = 10240
COL_AV = 10496
COL_GLR = 10752
PROJ_WIDTH = 11264
INPROJ_TN = 1024
INPROJ_CAST_STEPS = 4

SSM_T = 16
SSM_NC = SEQ // SSM_T
SSM_TC = SSM_T * SSM_GROUP
SSM_PAIRS = SSM_GROUPS // 2
SSM_STATES = SSM_GROUPS * SSM_STATE
SSM_OCT = LANES // SSM_GROUP
SSM_NOCT = SSM_GROUPS // SSM_OCT
NEG_BIG = -1e30
LOG2E = math.log2(math.e)
ATT_Q_SCALE = HEAD_DIM ** -0.5 * LOG2E


def _cparams(sem, vmem=VMEM_LIMIT):
    return pltpu.CompilerParams(dimension_semantics=sem, vmem_limit_bytes=vmem)


def _rms_rows(x_ref, g_ref, dst_ref, rows, chunk=128):
    g = g_ref[...]

    def body(r, c):
        sl = pl.ds(pl.multiple_of(r * chunk, chunk), chunk)
        xv = x_ref[sl, :].astype(F32)
        ms = jnp.mean(xv * xv, axis=-1, keepdims=True)
        dst_ref[sl, :] = (xv * lax.rsqrt(ms + EPS) * g).astype(dst_ref.dtype)
        return c

    lax.fori_loop(0, rows // chunk, body, 0)


def _rope(x, cos, sin_signed):
    lane = lax.broadcasted_iota(jnp.int32, x.shape, 1) % HEAD_DIM
    swapped = jnp.where(lane < HEAD_DIM // 2,
                        pltpu.roll(x, LANES - HEAD_DIM // 2, axis=1),
                        pltpu.roll(x, HEAD_DIM // 2, axis=1))
    return x * cos + swapped * sin_signed


def _inproj_kernel(x_ref, g_ref, w_ref, *refs, tm, n_cast):
    cast_src = refs[:n_cast]
    o_ref, uv_ref = refs[n_cast:n_cast + 2]
    cast_dst = refs[n_cast + 2:2 * n_cast + 2]
    h_ref, scr_ref = refs[2 * n_cast + 2:]
    j = pl.program_id(1)

    def project():
        return lax.dot_general(h_ref[...], w_ref[...], (((1,), (1,)), ((), ())), preferred_element_type=F32)

    @pl.when(j == 0)
    def _():
        _rms_rows(x_ref, g_ref, h_ref, tm)
        res = project()
        nt = res.shape[1] // LANES
        for k in range(nt):
            scr_ref[k] = res[:, k * LANES:(k + 1) * LANES]
        for t in range(SSM_T):
            for k in range(nt):
                uv_ref[t, :, k * LANES:(k + 1) * LANES] = scr_ref[
                    k, pl.ds(t, tm // SSM_T, stride=SSM_T), :].astype(uv_ref.dtype)

    @pl.when(j > 0)
    def _():
        for src, dst in zip(cast_src, cast_dst):
            dst[...] = src[...].astype(dst.dtype)
        o_ref[...] = project().astype(o_ref.dtype)


def _inproj(x, g, w, cast_weights, layer, tm=1024):
    s, d = x.shape
    tn = INPROJ_TN
    n = w.shape[1]
    tiles = s // tm
    cast_in, cast_out, cast_shapes = [], [], []

    def slab_index(i, j):
        return i * INPROJ_CAST_STEPS + jnp.clip(j - 1, 0, INPROJ_CAST_STEPS - 1)

    for cw in cast_weights:
        _, rows, cols = cw.shape
        slab = rows // (tiles * INPROJ_CAST_STEPS)
        cast_in.append(pl.BlockSpec((None, slab, cols), lambda i, j: (layer, slab_index(i, j), 0)))
        cast_out.append(pl.BlockSpec((slab, cols), lambda i, j: (slab_index(i, j), 0)))
        cast_shapes.append(jax.ShapeDtypeStruct((rows, cols), BF16))
    outs = pl.pallas_call(
        functools.partial(_inproj_kernel, tm=tm, n_cast=len(cast_weights)),
        out_shape=[jax.ShapeDtypeStruct((s, n - tn), BF16),
                   jax.ShapeDtypeStruct((SSM_T, s // SSM_T, SSM_WIDTH), BF16)] + cast_shapes,
        grid=(tiles, n // tn),
        in_specs=[
            pl.BlockSpec((tm, d), lambda i, j: (i, 0)),
            pl.BlockSpec((None, 1, d), lambda i, j: (layer, 0, 0)),
            pl.BlockSpec((None, tn, d), lambda i, j: (layer, j, 0)),
        ] + cast_in,
        out_specs=[pl.BlockSpec((tm, tn), lambda i, j: (i, jnp.maximum(j - 1, 0))),
                   pl.BlockSpec((SSM_T, tm // SSM_T, SSM_WIDTH), lambda i, j: (0, i, 0))] + cast_out,
        scratch_shapes=[pltpu.VMEM((tm, d), BF16), pltpu.VMEM((tn // LANES, tm, LANES), F32)],
        compiler_params=_cparams(("parallel", "arbitrary")),
        name="inproj",
    )(x, g, w, *cast_weights)
    return outs[0], outs[1], outs[2:]


def _s5_prepare(lam_re, lam_im, log_step, b_re, b_im, c_re, c_im):
    t_len = SSM_T
    g_n, p_n = SSM_GROUPS, SSM_STATE
    step = jnp.exp(log_step.astype(F32))[:, None]
    lr = lam_re.astype(F32)
    li = lam_im.astype(F32)
    mag = jnp.exp(lr * step)
    ar = mag * jnp.cos(li * step)
    ai = mag * jnp.sin(li * step)
    den = lr * lr + li * li
    fr = ((ar - 1.0) * lr + ai * li) / den
    fi = (ai * lr - (ar - 1.0) * li) / den
    br = b_re.astype(F32)
    bi = b_im.astype(F32)
    bbr = fr[..., None] * br - fi[..., None] * bi
    bbi = fr[..., None] * bi + fi[..., None] * br
    jj = jnp.arange(t_len + 1, dtype=F32)[None, None, :]
    pmag = jnp.exp((lr * step)[..., None] * jj)
    pang = (li * step)[..., None] * jj
    pw_r = pmag * jnp.cos(pang)
    pw_i = pmag * jnp.sin(pang)
    cr_t = jnp.transpose(c_re.astype(F32), (0, 2, 1))[:, :, None, :]
    ci_t = jnp.transpose(c_im.astype(F32), (0, 2, 1))[:, :, None, :]

    def c_times_powers(first):
        qr = pw_r[:, :, first:first + t_len, None]
        qi = pw_i[:, :, first:first + t_len, None]
        return ((cr_t * qr - ci_t * qi).reshape(g_n, p_n, SSM_TC),
                (cr_t * qi + ci_t * qr).reshape(g_n, p_n, SSM_TC))

    bbr_t = jnp.transpose(bbr, (0, 2, 1))
    bbi_t = jnp.transpose(bbi, (0, 2, 1))
    k_r, k_i = c_times_powers(0)
    k_lhs = jnp.concatenate([bbr_t, bbi_t], axis=-1)
    k_rhs = jnp.concatenate([k_r, -k_i], axis=1)
    d_r, d_i = c_times_powers(1)
    rev_r = jnp.transpose(pw_r, (0, 2, 1))[:, t_len - 1::-1][:, :t_len, None, :]
    rev_i = jnp.transpose(pw_i, (0, 2, 1))[:, t_len - 1::-1][:, :t_len, None, :]
    e_r = (rev_r * bbr_t[:, None] - rev_i * bbi_t[:, None]).reshape(g_n, SSM_TC, p_n)
    e_i = (rev_r * bbi_t[:, None] + rev_i * bbr_t[:, None]).reshape(g_n, SSM_TC, p_n)
    z_in = jnp.zeros((SSM_PAIRS, SSM_TC, p_n), F32)
    e_r2 = e_r.reshape(SSM_PAIRS, 2, SSM_TC, p_n)
    e_i2 = e_i.reshape(SSM_PAIRS, 2, SSM_TC, p_n)
    min_top = jnp.concatenate([e_r2[:, 0], z_in, e_i2[:, 0], z_in], axis=-1)
    min_bot = jnp.concatenate([z_in, e_r2[:, 1], z_in, e_i2[:, 1]], axis=-1)
    m_in = jnp.concatenate([min_top, min_bot], axis=1)
    z_out = jnp.zeros((SSM_PAIRS, p_n, SSM_TC), F32)
    d_r2 = d_r.reshape(SSM_PAIRS, 2, p_n, SSM_TC)
    d_i2 = d_i.reshape(SSM_PAIRS, 2, p_n, SSM_TC)
    m_out = jnp.concatenate([
        jnp.concatenate([d_r2[:, 0], z_out], axis=-1),
        jnp.concatenate([z_out, d_r2[:, 1]], axis=-1),
        jnp.concatenate([-d_i2[:, 0], z_out], axis=-1),
        jnp.concatenate([z_out, -d_i2[:, 1]], axis=-1)], axis=1)
    at_r = pw_r[:, :, t_len].reshape(1, SSM_STATES)
    at_i = pw_i[:, :, t_len].reshape(1, SSM_STATES)
    return k_lhs, k_rhs, m_in.astype(BF16), m_out.astype(BF16), at_r, at_i


def _s5_toeplitz_kernel(a_ref, b_ref, o_ref):
    lane = lax.broadcasted_iota(jnp.int32, (SSM_GROUP, SSM_TC), 1)

    def body(g, c):
        kr = jnp.dot(a_ref[g], b_ref[g], preferred_element_type=F32,
                     precision=lax.Precision.HIGHEST)
        for s in range(SSM_T):
            shifted = pltpu.roll(kr, s * SSM_GROUP, axis=1) if s else kr
            o_ref[g, s * SSM_GROUP:(s + 1) * SSM_GROUP, :] = jnp.where(
                lane >= s * SSM_GROUP, shifted, 0.0).astype(o_ref.dtype)
        return c

    lax.fori_loop(0, SSM_GROUPS, body, 0)


def _s5_toeplitz(k_lhs, k_rhs):
    nl = k_lhs.shape[0]
    return pl.pallas_call(
        _s5_toeplitz_kernel,
        out_shape=jax.ShapeDtypeStruct((nl, SSM_GROUPS, SSM_TC, SSM_TC), BF16),
        grid=(nl,),
        in_specs=[pl.BlockSpec((None, SSM_GROUPS, SSM_GROUP, 2 * SSM_STATE), lambda l: (l, 0, 0, 0)),
                  pl.BlockSpec((None, SSM_GROUPS, 2 * SSM_STATE, SSM_TC), lambda l: (l, 0, 0, 0))],
        out_specs=pl.BlockSpec((None, SSM_GROUPS, SSM_TC, SSM_TC), lambda l: (l, 0, 0, 0)),
        compiler_params=_cparams(("parallel",)),
        name="s5_toeplitz",
    )(k_lhs, k_rhs)


def _granule_transpose(sets):
    xs = [list(src) for src in sets]
    grp = lax.broadcasted_iota(jnp.int32, xs[0][0].shape, 1) // SSM_GROUP
    stride = SSM_OCT // 2
    while stride:
        upper = (grp & stride) != 0
        for x in xs:
            for a in range(SSM_OCT):
                if a & stride:
                    continue
                lo, hi = x[a], x[a + stride]
                x[a] = jnp.where(upper, pltpu.roll(hi, stride * SSM_GROUP, axis=1), lo)
                x[a + stride] = jnp.where(upper, hi, pltpu.roll(lo, LANES - stride * SSM_GROUP, axis=1))
        stride //= 2
    return xs


S5_RELAYOUT_ROWS = 64
S5_RELAYOUT_UNROLL = 1


def _s5_x_kernel(uv_ref, min_ref, u_ref, xr_ref, xi_ref):
    rb = S5_RELAYOUT_ROWS

    def body(r, c):
        rows = pl.ds(pl.multiple_of(r * rb, rb), rb)
        outs = _granule_transpose(
            [[uv_ref[half * SSM_OCT + tp, rows, :].astype(F32) for tp in range(SSM_OCT)] for half in range(2)])
        for half in range(2):
            for g in range(SSM_OCT):
                u_ref[g, rows, half * LANES:(half + 1) * LANES] = outs[half][g].astype(BF16)
        return c

    lax.fori_loop(0, SSM_NC // rb, body, 0, unroll=S5_RELAYOUT_UNROLL)
    for pr in range(SSM_OCT // 2):
        up = jnp.concatenate([u_ref[2 * pr], u_ref[2 * pr + 1]], axis=1)
        x = jnp.dot(up, min_ref[pr], preferred_element_type=F32)
        xr_ref[:, pr * LANES:(pr + 1) * LANES] = x[:, :LANES]
        xi_ref[:, pr * LANES:(pr + 1) * LANES] = x[:, LANES:]


def _s5_x(uv, m_in, layer):
    nc = SSM_NC
    ppo = SSM_OCT // 2
    sw = ppo * LANES
    return pl.pallas_call(
        _s5_x_kernel,
        out_shape=[jax.ShapeDtypeStruct((SSM_GROUPS, nc, SSM_TC), BF16),
                   jax.ShapeDtypeStruct((nc, SSM_STATES), F32),
                   jax.ShapeDtypeStruct((nc, SSM_STATES), F32)],
        grid=(SSM_NOCT,),
        in_specs=[
            pl.BlockSpec((SSM_T, nc, LANES), lambda k: (0, 0, k)),
            pl.BlockSpec((None, ppo, 2 * SSM_TC, 256), lambda k: (layer, k, 0, 0)),
        ],
        out_specs=[pl.BlockSpec((SSM_OCT, nc, SSM_TC), lambda k: (k, 0, 0)),
                   pl.BlockSpec((nc, sw), lambda k: (0, k)),
                   pl.BlockSpec((nc, sw), lambda k: (0, k))],
        compiler_params=_cparams(("parallel",)),
        name="s5_x",
    )(uv, m_in)


def _s5_scan_kernel(xr_ref, xi_ref, ar_ref, ai_ref, hr_ref, hi_ref, *, nc):
    ar = ar_ref[...]
    ai = ai_ref[...]

    def body(n, carry):
        hr, hi = carry
        row = pl.ds(n, 1)
        hr_ref[row, :] = hr
        hi_ref[row, :] = hi
        xr = xr_ref[row, :]
        xi = xi_ref[row, :]
        return (ar * hr - ai * hi + xr, ar * hi + ai * hr + xi)

    zero = jnp.zeros_like(ar)
    lax.fori_loop(0, nc, body, (zero, zero), unroll=8)


def _s5_scan(xr, xi, at_r, at_i, layer, lb=1024):
    nc, ns = xr.shape
    blk = pl.BlockSpec((nc, lb), lambda i: (0, i))
    tab = pl.BlockSpec((None, 1, lb), lambda i: (layer, 0, i))
    return pl.pallas_call(
        functools.partial(_s5_scan_kernel, nc=nc),
        out_shape=[jax.ShapeDtypeStruct((nc, ns), F32)] * 2,
        grid=(ns // lb,),
        in_specs=[blk, blk, tab, tab],
        out_specs=[blk, blk],
        compiler_params=_cparams(("parallel",)),
        name="s5_scan",
    )(xr, xi, at_r, at_i)


def _s5_y_kernel(u_ref, hr_ref, hi_ref, toep_ref, mout_ref, yv_ref, y_scr):
    for pr in range(SSM_OCT // 2):
        hcat = jnp.concatenate([hr_ref[:, pr * LANES:(pr + 1) * LANES],
                                hi_ref[:, pr * LANES:(pr + 1) * LANES]], axis=1).astype(BF16)
        carry = jnp.dot(hcat, mout_ref[pr], preferred_element_type=F32)
        for k in range(2):
            g = 2 * pr + k
            y = jnp.dot(u_ref[g], toep_ref[g], preferred_element_type=F32)
            y_scr[g] = y + carry[:, k * SSM_TC:(k + 1) * SSM_TC]

    rb = S5_RELAYOUT_ROWS

    def body(r, c):
        rows = pl.ds(pl.multiple_of(r * rb, rb), rb)
        outs = _granule_transpose(
            [[y_scr[g, rows, half * LANES:(half + 1) * LANES] for g in range(SSM_OCT)] for half in range(2)])
        for half in range(2):
            for tp in range(SSM_OCT):
                yv_ref[half * SSM_OCT + tp, rows, :] = outs[half][tp].astype(BF16)
        return c

    lax.fori_loop(0, SSM_NC // rb, body, 0, unroll=S5_RELAYOUT_UNROLL)


def _s5_y(u_g, hr, hi, toep, m_out, layer):
    nc = SSM_NC
    ppo = SSM_OCT // 2
    hblk = pl.BlockSpec((nc, ppo * LANES), lambda k: (0, k))
    return pl.pallas_call(
        _s5_y_kernel,
        out_shape=jax.ShapeDtypeStruct((SSM_T, nc, SSM_WIDTH), BF16),
        grid=(SSM_NOCT,),
        in_specs=[
            pl.BlockSpec((SSM_OCT, nc, SSM_TC), lambda k: (k, 0, 0)),
            hblk, hblk,
            pl.BlockSpec((None, SSM_OCT, SSM_TC, SSM_TC), lambda k: (layer, k, 0, 0)),
            pl.BlockSpec((None, ppo, 256, 2 * SSM_TC), lambda k: (layer, k, 0, 0)),
        ],
        out_specs=pl.BlockSpec((SSM_T, nc, LANES), lambda k: (0, 0, k)),
        scratch_shapes=[pltpu.VMEM((SSM_OCT, nc, SSM_TC), F32)],
        compiler_params=_cparams(("parallel",)),
        name="s5_y",
    )(u_g, hr, hi, toep, m_out)


def _s5_glu_kernel(yv_ref, uv_ref, d_ref, w_ref, b_ref, o_ref, z_scr, zb_scr, g_scr, nat_scr, *, ncb):
    d = d_ref[...]
    for t in range(SSM_T):
        z = jax.nn.gelu(yv_ref[t].astype(F32) + d * uv_ref[t].astype(F32))
        z_scr[t * ncb:(t + 1) * ncb, :] = z
        zb_scr[t * ncb:(t + 1) * ncb, :] = z.astype(BF16)
    g_scr[...] = jnp.dot(zb_scr[...], w_ref[...], preferred_element_type=F32)
    b = b_ref[...]
    nt = SSM_WIDTH // LANES
    for t in range(SSM_T):
        rows = slice(t * ncb, (t + 1) * ncb)
        out = z_scr[rows, :] * jax.nn.sigmoid(g_scr[rows, :] + b)
        for k in range(nt):
            nat_scr[k, pl.ds(t, ncb, stride=SSM_T), :] = out[:, k * LANES:(k + 1) * LANES]
    for k in range(nt):
        o_ref[:, k * LANES:(k + 1) * LANES] = nat_scr[k].astype(o_ref.dtype)


def _s5_glu(yv, uv, d_skip, w_glu, b_glu, layer, ncb=64):
    w = SSM_WIDTH
    tm = ncb * SSM_T
    vblk = pl.BlockSpec((SSM_T, ncb, w), lambda i: (0, i, 0))
    vec = pl.BlockSpec((None, 1, w), lambda i: (layer, 0, 0))
    return pl.pallas_call(
        functools.partial(_s5_glu_kernel, ncb=ncb),
        out_shape=jax.ShapeDtypeStruct((SEQ, w), BF16),
        grid=(SSM_NC // ncb,),
        in_specs=[vblk, vblk, vec,
                  pl.BlockSpec((w, w), lambda i: (0, 0)),
                  vec],
        out_specs=pl.BlockSpec((tm, w), lambda i: (i, 0)),
        scratch_shapes=[pltpu.VMEM((tm, w), F32), pltpu.VMEM((tm, w), BF16), pltpu.VMEM((tm, w), F32),
                        pltpu.VMEM((w // LANES, tm, LANES), F32)],
        compiler_params=_cparams(("parallel",)),
        name="s5_glu",
    )(yv, uv, d_skip, w_glu, b_glu)


def _s5_branch(uv, prep, d_skip, w_glu, b_glu, layer):
    toep, m_in, m_out, at_r, at_i = prep
    u_g, xr, xi = _s5_x(uv, m_in, layer)
    hr, hi = _s5_scan(xr, xi, at_r, at_i, layer)
    yv = _s5_y(u_g, hr, hi, toep, m_out, layer)
    return _s5_glu(yv, uv, d_skip, w_glu, b_glu, layer)


def _swa_kernel(sink_ref, q_ref, k_ref, v_ref, cos_ref, sin_ref, o_ref, s_scr, p_scr, kpad_scr, vext_scr):
    n = pl.program_id(0)
    blk = ATT_BLOCK
    hpt = LANES // HEAD_DIM
    nvar = ATT_KV_HEADS * hpt
    tiles_per_kv = ATT_GROUP // hpt
    row = lax.broadcasted_iota(jnp.int32, (blk, blk), 0)
    col = lax.broadcasted_iota(jnp.int32, (blk, blk), 1)

    @pl.when(n == 0)
    def _():
        kpad_scr[...] = jnp.zeros_like(kpad_scr)
        vext_scr[:, :, :LANES] = jnp.zeros((nvar, 2 * blk, LANES), BF16)
        vext_scr[:, :, LANES:] = jnp.ones((nvar, 2 * blk, LANES), BF16)

    @pl.when(n > 0)
    def _():
        for i in range(nvar):
            kpad_scr[i, :blk, :] = kpad_scr[i, blk:, :]
            vext_scr[i, :blk, :LANES] = jnp.where(row == 0, jnp.zeros((), BF16), vext_scr[i, blk:, :LANES])

    lo = lax.broadcasted_iota(jnp.int32, (blk, LANES), 1) < HEAD_DIM

    def lane_padded(tile, half_in):
        other = pltpu.roll(tile, HEAD_DIM, axis=1)
        at_lo, at_hi = (tile, other) if half_in == 0 else (other, tile)
        return jnp.where(lo, at_lo, 0.0).astype(BF16), jnp.where(lo, 0.0, at_hi).astype(BF16)

    cos = cos_ref[...]
    sin = sin_ref[...]
    for kv_tile in range(ATT_KV_HEADS // hpt):
        lanes = slice(kv_tile * LANES, (kv_tile + 1) * LANES)
        kt = _rope(k_ref[:, lanes].astype(F32), cos, sin)
        vt = v_ref[:, lanes].astype(F32)
        for half in range(hpt):
            h = kv_tile * hpt + half
            k_pad = lane_padded(kt, half)
            v_pad = lane_padded(vt, half)
            for e in range(hpt):
                kpad_scr[h * hpt + e, blk:, :] = k_pad[e]
                vext_scr[h * hpt + e, blk:, :LANES] = v_pad[e]

    cur = col <= row
    sink_slot = (col == 0).astype(F32)
    prev_bias = jnp.where(n > 0, 0.0, NEG_BIG)

    for tile in range(ATT_HEADS // hpt):
        qt = _rope(q_ref[:, tile * LANES:(tile + 1) * LANES].astype(F32) * ATT_Q_SCALE, cos, sin).astype(BF16)
        for e in range(hpt):
            var = (tile // tiles_per_kv) * hpt + e
            s_scr[tile * hpt + e] = lax.dot_general(qt, kpad_scr[var], (((1,), (1,)), ((), ())),
                                                    preferred_element_type=F32)

    for hd in range(ATT_HEADS):
        sk = sink_ref[hd] * LOG2E
        s = jnp.where(cur, s_scr[hd, :, blk:], s_scr[hd, :, :blk] + prev_bias)
        m = jnp.maximum(jnp.max(s, axis=-1, keepdims=True), sk)
        p = jnp.exp2(s - m)
        p_scr[hd, :, :blk] = jnp.where(cur, jnp.exp2(sk - m) * sink_slot, p).astype(BF16)
        p_scr[hd, :, blk:] = jnp.where(cur, p, 0.0).astype(BF16)

    for tile in range(ATT_HEADS // hpt):
        acc = None
        for e in range(hpt):
            var = (tile // tiles_per_kv) * hpt + e
            oe = jnp.dot(p_scr[tile * hpt + e], vext_scr[var], preferred_element_type=F32)
            term = oe[:, :LANES] / oe[:, LANES:]
            acc = term if acc is None else acc + term
        o_ref[:, tile * LANES:(tile + 1) * LANES] = acc.astype(o_ref.dtype)


def _swa(proj, sinks, cos_t, sin_t):
    assert WINDOW == ATT_BLOCK
    s = proj.shape[0]
    blk = ATT_BLOCK
    nvar = ATT_KV_HEADS * (LANES // HEAD_DIM)
    tab = pl.BlockSpec((blk, LANES), lambda n: (n, 0))
    return pl.pallas_call(
        _swa_kernel,
        out_shape=jax.ShapeDtypeStruct((s, ATT_Q_WIDTH), BF16),
        grid=(s // blk,),
        in_specs=[
            pl.BlockSpec(memory_space=pltpu.SMEM),
            pl.BlockSpec((blk, ATT_Q_WIDTH), lambda n: (n, COL_AQ // ATT_Q_WIDTH)),
            pl.BlockSpec((blk, ATT_KV_WIDTH), lambda n: (n, COL_AK // ATT_KV_WIDTH)),
            pl.BlockSpec((blk, ATT_KV_WIDTH), lambda n: (n, COL_AV // ATT_KV_WIDTH)),
            tab, tab,
        ],
        out_specs=pl.BlockSpec((blk, ATT_Q_WIDTH), lambda n: (n, 0)),
        scratch_shapes=[pltpu.VMEM((ATT_HEADS, blk, 2 * blk), F32), pltpu.VMEM((ATT_HEADS, blk, 2 * blk), BF16),
                        pltpu.VMEM((nvar, 2 * blk, LANES), BF16), pltpu.VMEM((nvar, 2 * blk, 2 * LANES), BF16)],
        compiler_params=_cparams(("arbitrary",)),
        name="swa",
    )(sinks, proj, proj, proj, cos_t, sin_t)


def _log_sigmoid(x):
    return jnp.minimum(x, 0.0) - jnp.log(1.0 + jnp.exp(-jnp.abs(x)))


def _gla_kernel(q_ref, k_ref, v_ref, glr_ref, gout_ref, wg_ref, bg_ref, ng_ref, tri_ref, o_ref,
                st_ref, qin_scr, att_scr, upd_scr, sb_scr, *, rb):
    @pl.when(pl.program_id(0) == 0)
    def _():
        st_ref[...] = jnp.zeros_like(st_ref)

    ch = GLA_CHUNK
    ncb = rb // ch
    nh = GLA_HEADS
    logit = jnp.dot(glr_ref[...], wg_ref[...], preferred_element_type=F32) + bg_ref[...]
    log_a = _log_sigmoid(logit) * (1.0 / GLA_TAU)
    tri = tri_ref[...]
    a_hi = log_a.astype(BF16)
    a_lo = (log_a - a_hi.astype(F32)).astype(BF16)
    bcum = (jnp.dot(tri, a_hi, preferred_element_type=F32)
            + jnp.dot(tri, a_lo, preferred_element_type=F32))
    qin_scr[...] = (q_ref[...].astype(F32) * (GLA_DK ** -0.5) * jnp.exp(bcum)).astype(BF16)
    k = k_ref[...].astype(F32)
    k_in = (k * jnp.exp(-bcum)).astype(BF16)
    causal = (lax.broadcasted_iota(jnp.int32, (ch, ch), 1)
              <= lax.broadcasted_iota(jnp.int32, (ch, ch), 0))

    decays = []
    for c in range(ncb):
        rows = slice(c * ch, (c + 1) * ch)
        b_last = bcum[(c + 1) * ch - 1:(c + 1) * ch, :]
        k_dec = (k[rows] * jnp.exp(b_last - bcum[rows])).astype(BF16)
        decays.append(jnp.exp(b_last))
        for h in range(nh):
            ks = slice(h * GLA_DK, (h + 1) * GLA_DK)
            att = lax.dot_general(qin_scr[rows, ks], k_in[rows, ks], (((1,), (1,)), ((), ())),
                                  preferred_element_type=F32)
            att_scr[c * nh + h] = jnp.where(causal, att, 0.0).astype(BF16)
            upd_scr[c * nh + h] = lax.dot_general(v_ref[rows, h * GLA_DV:(h + 1) * GLA_DV], k_dec[:, ks],
                                                  (((0,), (0,)), ((), ())), preferred_element_type=F32)

    for h in range(nh):
        ks = slice(h * GLA_DK, (h + 1) * GLA_DK)
        st = st_ref[h]
        for c in range(ncb):
            sb_scr[c * nh + h] = st.astype(BF16)
            st = st * decays[c][:, ks] + upd_scr[c * nh + h]
        st_ref[h] = st

    for c in range(ncb):
        rows = slice(c * ch, (c + 1) * ch)
        for h in range(nh):
            ks = slice(h * GLA_DK, (h + 1) * GLA_DK)
            vs = slice(h * GLA_DV, (h + 1) * GLA_DV)
            o = jnp.dot(att_scr[c * nh + h], v_ref[rows, vs], preferred_element_type=F32)
            o = o + lax.dot_general(qin_scr[rows, ks], sb_scr[c * nh + h], (((1,), (1,)), ((), ())),
                                    preferred_element_type=F32)
            o = o * lax.rsqrt(jnp.mean(o * o, axis=-1, keepdims=True) + EPS)
            o = o * ng_ref[:, vs]
            go = gout_ref[rows, vs].astype(F32)
            o = o * (go * jax.nn.sigmoid(go))
            o_ref[rows, vs] = o.astype(o_ref.dtype)


def _gla(proj, w_gate, b_gate, norm_g, layer, rb=512):
    s = proj.shape[0]
    ch = GLA_CHUNK
    idx = np.arange(rb)
    tri = jnp.asarray((idx[None, :] <= idx[:, None]) & (idx[None, :] // ch == idx[:, None] // ch), BF16)
    nblk = (rb // ch) * GLA_HEADS
    return pl.pallas_call(
        functools.partial(_gla_kernel, rb=rb),
        out_shape=jax.ShapeDtypeStruct((s, GLA_V_WIDTH), BF16),
        grid=(s // rb,),
        in_specs=[
            pl.BlockSpec((rb, GLA_QK_WIDTH), lambda i: (i, COL_GQ // GLA_QK_WIDTH)),
            pl.BlockSpec((rb, GLA_QK_WIDTH), lambda i: (i, COL_GK // GLA_QK_WIDTH)),
            pl.BlockSpec((rb, GLA_V_WIDTH), lambda i: (i, COL_GV // GLA_V_WIDTH)),
            pl.BlockSpec((rb, LANES), lambda i: (i, COL_GLR // LANES)),
            pl.BlockSpec((rb, GLA_V_WIDTH), lambda i: (i, COL_GOUT // GLA_V_WIDTH)),
            pl.BlockSpec((None, LANES, GLA_QK_WIDTH), lambda i: (layer, 0, 0)),
            pl.BlockSpec((None, 1, GLA_QK_WIDTH), lambda i: (layer, 0, 0)),
            pl.BlockSpec((None, 1, GLA_V_WIDTH), lambda i: (layer, 0, 0)),
            pl.BlockSpec((rb, rb), lambda i: (0, 0)),
        ],
        out_specs=pl.BlockSpec((rb, GLA_V_WIDTH), lambda i: (i, 0)),
        scratch_shapes=[pltpu.VMEM((GLA_HEADS, GLA_DV, GLA_DK), F32),
                        pltpu.VMEM((rb, GLA_QK_WIDTH), BF16),
                        pltpu.VMEM((nblk, ch, ch), BF16),
                        pltpu.VMEM((nblk, GLA_DV, GLA_DK), F32),
                        pltpu.VMEM((nblk, GLA_DV, GLA_DK), BF16)],
        compiler_params=_cparams(("arbitrary",)),
        name="gla",
    )(proj, proj, proj, proj, proj, w_gate, b_gate, norm_g, tri)


MIX_TN = 512


def _mix_kernel(x_ref, ys_ref, ya_ref, yg_ref, gate_ref, ws_ref, wa_ref, wg_ref, wo_ref, w1f_ref, w2f_ref,
                o_ref, w1b_ref, w2b_ref, m_scr):
    d = D_MODEL
    w1b_ref[...] = w1f_ref[...].astype(BF16)
    w2b_ref[...] = w2f_ref[...].astype(BF16)
    for j in range(d // MIX_TN):
        cols = slice(j * MIX_TN, (j + 1) * MIX_TN)
        m = None
        for k, (y_ref, w_ref) in enumerate(((ys_ref, ws_ref), (ya_ref, wa_ref), (yg_ref, wg_ref))):
            proj = jnp.dot(y_ref[...], w_ref[:, cols], preferred_element_type=F32)
            gate = gate_ref[:, k * d + j * MIX_TN:k * d + (j + 1) * MIX_TN].astype(F32)
            term = jax.nn.sigmoid(gate) * proj
            m = term if m is None else m + term
        m_scr[:, cols] = m.astype(BF16)
    o_ref[...] = x_ref[...] + jnp.dot(m_scr[...], wo_ref[...], preferred_element_type=F32)


def _mix(x, proj, y_ssm, y_att, y_gla, w_bs, w_ba, w_bg, w_out, w_ff1, w_ff2, layer, tm=256):
    s, d = x.shape
    bw = y_ssm.shape[1]
    dff = w_ff1.shape[2]
    steps = s // tm
    r1, r2 = d // steps, dff // steps
    yblk = pl.BlockSpec((tm, bw), lambda i: (i, 0))
    wblk = pl.BlockSpec((bw, d), lambda i: (0, 0), pipeline_mode=pl.Buffered(1))
    return pl.pallas_call(
        _mix_kernel,
        out_shape=[jax.ShapeDtypeStruct((s, d), F32),
                   jax.ShapeDtypeStruct((d, dff), BF16),
                   jax.ShapeDtypeStruct((dff, d), BF16)],
        grid=(steps,),
        in_specs=[
            pl.BlockSpec((tm, d), lambda i: (i, 0)),
            yblk, yblk, yblk,
            pl.BlockSpec((tm, 3 * d), lambda i: (i, COL_MERGE // (3 * d))),
            wblk, wblk, wblk,
            pl.BlockSpec((d, d), lambda i: (0, 0), pipeline_mode=pl.Buffered(1)),
            pl.BlockSpec((None, r1, dff), lambda i: (layer, i, 0)),
            pl.BlockSpec((None, r2, d), lambda i: (layer, i, 0)),
        ],
        out_specs=[pl.BlockSpec((tm, d), lambda i: (i, 0)),
                   pl.BlockSpec((r1, dff), lambda i: (i, 0)),
                   pl.BlockSpec((r2, d), lambda i: (i, 0))],
        scratch_shapes=[pltpu.VMEM((tm, d), BF16)],
        compiler_params=_cparams(("parallel",)),
        name="mix",
    )(x, y_ssm, y_att, y_gla, proj, w_bs, w_ba, w_bg, w_out, w_ff1, w_ff2)


FFN_SLABS = 2


def _ffn_kernel(x_ref, g_ref, w1_ref, w2_ref, fg_ref, o_ref, h_ref, *, tm, nf, final_norm):
    f = pl.program_id(1)

    @pl.when(f == 0)
    def _():
        _rms_rows(x_ref, g_ref, h_ref, tm)
        o_ref[...] = x_ref[...]

    slab = tm // FFN_SLABS
    for r in range(FFN_SLABS):
        rows = slice(r * slab, (r + 1) * slab)
        a = jnp.dot(h_ref[rows, :], w1_ref[...], preferred_element_type=F32)
        a = jnp.square(jnp.maximum(a, 0.0)).astype(BF16)
        o_ref[rows, :] += jnp.dot(a, w2_ref[...], preferred_element_type=F32)

    if final_norm:
        @pl.when(f == nf - 1)
        def _():
            _rms_rows(o_ref, fg_ref, o_ref, tm)


def _ffn(x, g, w1, w2, fg, layer, final_norm, tm=1024, tf=512):
    s, d = x.shape
    dff = w1.shape[1]
    nf = dff // tf
    return pl.pallas_call(
        functools.partial(_ffn_kernel, tm=tm, nf=nf, final_norm=final_norm),
        out_shape=jax.ShapeDtypeStruct((s, d), F32),
        grid=(s // tm, nf),
        in_specs=[
            pl.BlockSpec((tm, d), lambda i, f: (i, 0)),
            pl.BlockSpec((None, 1, d), lambda i, f: (layer, 0, 0)),
            pl.BlockSpec((d, tf), lambda i, f: (0, f)),
            pl.BlockSpec((tf, d), lambda i, f: (f, 0)),
            pl.BlockSpec((1, d), lambda i, f: (0, 0)),
        ],
        out_specs=pl.BlockSpec((tm, d), lambda i, f: (i, 0)),
        scratch_shapes=[pltpu.VMEM((tm, d), BF16)],
        compiler_params=_cparams(("parallel", "arbitrary")),
        name="ffn",
    )(x, g, w1, w2, fg)


PACK_TN = 512
PACK_ALIGN = 16
PACK_COPY, PACK_GLR = 0, 1


def _pack_tables():
    sizes = [SSM_WIDTH, ATT_Q_WIDTH, ATT_KV_WIDTH, ATT_KV_WIDTH, GLA_QK_WIDTH, GLA_QK_WIDTH,
             GLA_V_WIDTH, GLA_GATE_RANK, GLA_V_WIDTH, 3 * D_MODEL]
    o_u, o_aq, o_ak, o_av, o_gq, o_gk, o_gv, o_glr, o_gout, o_merge = np.concatenate([[0], np.cumsum(sizes)])[:-1]
    assert o_av == o_ak + ATT_KV_WIDTH
    pieces = [(o_u, SSM_WIDTH), (o_merge, 3 * D_MODEL), (o_aq, ATT_Q_WIDTH), (o_gv, GLA_V_WIDTH),
              (o_gout, GLA_V_WIDTH), (o_gq, GLA_QK_WIDTH), (o_gk, GLA_QK_WIDTH), (o_ak, 2 * ATT_KV_WIDTH)]
    src, mode = [], []
    for start, width in pieces:
        assert start % PACK_ALIGN == 0 and width % PACK_TN == 0
        for k in range(width // PACK_TN):
            src.append((start + k * PACK_TN) // PACK_ALIGN)
            mode.append(PACK_COPY)
    assert o_glr % PACK_ALIGN == 0 and o_glr + PACK_TN <= sum(sizes)
    src.append(o_glr // PACK_ALIGN)
    mode.append(PACK_GLR)
    assert len(mode) * PACK_TN == INPROJ_TN + PROJ_WIDTH
    return [jnp.asarray(np.array(t, np.int32)) for t in (src, mode)]


def _pack_kernel(src, mode, a_ref, o_ref, *, chunk=128):
    md = mode[pl.program_id(1)]

    def for_chunks(fn):
        def body(r, c):
            sl = pl.ds(pl.multiple_of(r * chunk, chunk), chunk)
            o_ref[sl, :] = fn(r, a_ref[0, sl, :]).astype(o_ref.dtype)
            return c
        lax.fori_loop(0, PACK_TN // chunk, body, 0)

    @pl.when(md == PACK_COPY)
    def _():
        for_chunks(lambda r, a: a)

    @pl.when(md == PACK_GLR)
    def _():
        row = lax.broadcasted_iota(jnp.int32, (chunk, a_ref.shape[2]), 0)
        for_chunks(lambda r, a: jnp.where(row + r * chunk < GLA_GATE_RANK, a, 0.0))


def _pack_w_in(w):
    nl, d, _ = w.shape
    wt = jnp.swapaxes(w, 1, 2)
    src, mode = _pack_tables()
    nt = (INPROJ_TN + PROJ_WIDTH) // PACK_TN
    return pl.pallas_call(
        _pack_kernel,
        out_shape=jax.ShapeDtypeStruct((nl, nt * PACK_TN, d), BF16),
        grid_spec=pltpu.PrefetchScalarGridSpec(
            num_scalar_prefetch=2,
            grid=(nl, nt),
            in_specs=[
                pl.BlockSpec((pl.Element(1), pl.Element(PACK_TN), pl.Element(d)),
                             lambda l, j, s, m: (l, pl.multiple_of(s[j] * PACK_ALIGN, PACK_ALIGN), 0)),
            ],
            out_specs=pl.BlockSpec((None, PACK_TN, d), lambda l, j, s, m: (l, j, 0)),
        ),
        compiler_params=_cparams(("parallel", "parallel")),
        name="pack_w_in",
    )(src, mode, wt)


def _rope_tables():
    half = HEAD_DIM // 2
    inv = np.float32(ROPE_THETA) ** (-np.arange(half, dtype=np.float32) / np.float32(half))
    ang = (np.arange(SEQ, dtype=np.float32)[:, None] * inv[None, :].astype(np.float32)).astype(np.float32)
    cos = np.cos(ang.astype(np.float64)).astype(np.float32)
    sin = np.sin(ang.astype(np.float64)).astype(np.float32)
    cos_t = np.concatenate([cos, cos, cos, cos], axis=1)
    sin_t = np.concatenate([-sin, sin, -sin, sin], axis=1)
    return jnp.asarray(cos_t), jnp.asarray(sin_t)


def kernel(x, norm1_g, w_in, ssm_lam_re, ssm_lam_im, ssm_log_step, ssm_b_re, ssm_b_im, ssm_c_re, ssm_c_im, ssm_d, ssm_w_glu, ssm_b_glu, att_sinks, gla_w_gate, gla_b_gate, gla_norm_g, w_branch_ssm, w_branch_att, w_branch_gla, w_out, norm2_g, w_ff1, w_ff2, final_norm_g):
    assert x.shape == (1, SEQ, D_MODEL)
    nl = DEPTH
    xs = x.reshape(SEQ, D_MODEL).astype(F32)
    cos_t, sin_t = _rope_tables()
    fg = final_norm_g.reshape(1, D_MODEL).astype(F32)
    w_in_p = _pack_w_in(w_in)
    n1 = norm1_g.reshape(nl, 1, D_MODEL).astype(F32)
    n2 = norm2_g.reshape(nl, 1, D_MODEL).astype(F32)
    k_lhs, k_rhs, m_in, m_out, at_r, at_i = jax.vmap(_s5_prepare)(
        ssm_lam_re, ssm_lam_im, ssm_log_step, ssm_b_re, ssm_b_im, ssm_c_re, ssm_c_im)
    prep = (_s5_toeplitz(k_lhs, k_rhs), m_in, m_out, at_r, at_i)
    d_skip = ssm_d.reshape(nl, 1, SSM_WIDTH).astype(F32)
    b_glu = ssm_b_glu.reshape(nl, 1, SSM_WIDTH).astype(F32)
    sinks = att_sinks.astype(F32)
    wg_pad = jnp.concatenate(
        [gla_w_gate.astype(BF16), jnp.zeros((nl, LANES - GLA_GATE_RANK, GLA_QK_WIDTH), BF16)], axis=1)
    bg = gla_b_gate.reshape(nl, 1, GLA_QK_WIDTH).astype(F32)
    ng = gla_norm_g.reshape(nl, 1, GLA_V_WIDTH).astype(F32)
    small_f32 = [w.astype(F32) for w in (ssm_w_glu, w_branch_ssm, w_branch_att, w_branch_gla, w_out)]
    w_ff1 = w_ff1.astype(F32)
    w_ff2 = w_ff2.astype(F32)
    for l in range(nl):
        proj, uv, (w_glu, w_bs, w_ba, w_bg, w_o) = _inproj(xs, n1, w_in_p, small_f32, l)
        y_ssm = _s5_branch(uv, prep, d_skip, w_glu, b_glu, l)
        y_att = _swa(proj, sinks[l], cos_t, sin_t)
        y_gla = _gla(proj, wg_pad, bg, ng, l)
        xs, w1, w2 = _mix(xs, proj, y_ssm, y_att, y_gla, w_bs, w_ba, w_bg, w_o, w_ff1, w_ff2, l)
        xs = _ffn(xs, n2, w1, w2, fg, l, final_norm=(l == nl - 1))
    return xs.reshape(1, SEQ, D_MODEL)
```

```python
import functools
import math

import jax
import jax.numpy as jnp
import numpy as np
from jax import lax
from jax.experimental import pallas as pl
from jax.experimental.pallas import tpu as pltpu

F32 = jnp.float32
BF16 = jnp.bfloat16

D_MODEL = 2048
SEQ = 16384
DEPTH = 2
SSM_WIDTH = 1024
SSM_GROUP = 16
SSM_GROUPS = 64
SSM_STATE = 64
HEAD_DIM = 64
ATT_HEADS = 16
ATT_KV_HEADS = 4
ATT_GROUP = 4
ATT_Q_WIDTH = 1024
ATT_KV_WIDTH = 256
WINDOW = 128
ATT_BLOCK = 128
ROPE_THETA = 10000.0
GLA_HEADS = 4
GLA_V_WIDTH = 1024
GLA_DV = 256
GLA_DK = 128
GLA_QK_WIDTH = 512
GLA_GATE_RANK = 16
GLA_TAU = 16.0
GLA_CHUNK = 64
EPS = 1e-6

LANES = 128
VMEM_LIMIT = 56 * 1024 * 1024

COL_MERGE = 0
COL_AQ = 6144
COL_GV = 7168
COL_GOUT = 8192
COL_GQ = 9216
COL_GK = 9728
COL_AK = 10240
COL_AV = 10496
COL_GLR = 10752
PROJ_WIDTH = 11264
INPROJ_TN = 1024
INPROJ_CAST_STEPS = 4

SSM_T = 16
SSM_NC = SEQ // SSM_T
SSM_TC = SSM_T * SSM_GROUP
SSM_PAIRS = SSM_GROUPS // 2
SSM_STATES = SSM_GROUPS * SSM_STATE
SSM_OCT = LANES // SSM_GROUP
SSM_NOCT = SSM_GROUPS // SSM_OCT
NEG_BIG = -1e30
LOG2E = math.log2(math.e)
ATT_Q_SCALE = HEAD_DIM ** -0.5 * LOG2E


def _cparams(sem, vmem=VMEM_LIMIT):
    return pltpu.CompilerParams(dimension_semantics=sem, vmem_limit_bytes=vmem)


def _rms_rows(x_ref, g_ref, dst_ref, rows, chunk=128):
    g = g_ref[...]

    def body(r, c):
        sl = pl.ds(pl.multiple_of(r * chunk, chunk), chunk)
        xv = x_ref[sl, :].astype(F32)
        ms = jnp.mean(xv * xv, axis=-1, keepdims=True)
        dst_ref[sl, :] = (xv * lax.rsqrt(ms + EPS) * g).astype(dst_ref.dtype)
        return c

    lax.fori_loop(0, rows // chunk, body, 0)


def _rope(x, cos, sin_signed):
    lane = lax.broadcasted_iota(jnp.int32, x.shape, 1) % HEAD_DIM
    swapped = jnp.where(lane < HEAD_DIM // 2,
                        pltpu.roll(x, LANES - HEAD_DIM // 2, axis=1),
                        pltpu.roll(x, HEAD_DIM // 2, axis=1))
    return x * cos + swapped * sin_signed


def _inproj_kernel(x_ref, g_ref, w_ref, *refs, tm, n_cast):
    cast_src = refs[:n_cast]
    o_ref, uv_ref = refs[n_cast:n_cast + 2]
    cast_dst = refs[n_cast + 2:2 * n_cast + 2]
    h_ref, scr_ref = refs[2 * n_cast + 2:]
    j = pl.program_id(1)

    def project():
        return lax.dot_general(h_ref[...], w_ref[...], (((1,), (1,)), ((), ())), preferred_element_type=F32)

    @pl.when(j == 0)
    def _():
        _rms_rows(x_ref, g_ref, h_ref, tm)
        res = project()
        nt = res.shape[1] // LANES
        for k in range(nt):
            scr_ref[k] = res[:, k * LANES:(k + 1) * LANES]
        for t in range(SSM_T):
            for k in range(nt):
                uv_ref[t, :, k * LANES:(k + 1) * LANES] = scr_ref[
                    k, pl.ds(t, tm // SSM_T, stride=SSM_T), :].astype(uv_ref.dtype)

    @pl.when(j > 0)
    def _():
        for src, dst in zip(cast_src, cast_dst):
            dst[...] = src[...].astype(dst.dtype)
        o_ref[...] = project().astype(o_ref.dtype)


def _inproj(x, g, w, cast_weights, layer, tm=1024):
    s, d = x.shape
    tn = INPROJ_TN
    n = w.shape[1]
    tiles = s // tm
    cast_in, cast_out, cast_shapes = [], [], []

    def slab_index(i, j):
        return i * INPROJ_CAST_STEPS + jnp.clip(j - 1, 0, INPROJ_CAST_STEPS - 1)

    for cw in cast_weights:
        _, rows, cols = cw.shape
        slab = rows // (tiles * INPROJ_CAST_STEPS)
        cast_in.append(pl.BlockSpec((None, slab, cols), lambda i, j: (layer, slab_index(i, j), 0)))
        cast_out.append(pl.BlockSpec((slab, cols), lambda i, j: (slab_index(i, j), 0)))
        cast_shapes.append(jax.ShapeDtypeStruct((rows, cols), BF16))
    outs = pl.pallas_call(
        functools.partial(_inproj_kernel, tm=tm, n_cast=len(cast_weights)),
        out_shape=[jax.ShapeDtypeStruct((s, n - tn), BF16),
                   jax.ShapeDtypeStruct((SSM_T, s // SSM_T, SSM_WIDTH), BF16)] + cast_shapes,
        grid=(tiles, n // tn),
        in_specs=[
            pl.BlockSpec((tm, d), lambda i, j: (i, 0)),
            pl.BlockSpec((None, 1, d), lambda i, j: (layer, 0, 0)),
            pl.BlockSpec((None, tn, d), lambda i, j: (layer, j, 0)),
        ] + cast_in,
        out_specs=[pl.BlockSpec((tm, tn), lambda i, j: (i, jnp.maximum(j - 1, 0))),
                   pl.BlockSpec((SSM_T, tm // SSM_T, SSM_WIDTH), lambda i, j: (0, i, 0))] + cast_out,
        scratch_shapes=[pltpu.VMEM((tm, d), BF16), pltpu.VMEM((tn // LANES, tm, LANES), F32)],
        compiler_params=_cparams(("parallel", "arbitrary")),
        name="inproj",
    )(x, g, w, *cast_weights)
    return outs[0], outs[1], outs[2:]


def _s5_prepare(lam_re, lam_im, log_step, b_re, b_im, c_re, c_im):
    t_len = SSM_T
    g_n, p_n = SSM_GROUPS, SSM_STATE
    step = jnp.exp(log_step.astype(F32))[:, None]
    lr = lam_re.astype(F32)
    li = lam_im.astype(F32)
    mag = jnp.exp(lr * step)
    ar = mag * jnp.cos(li * step)
    ai = mag * jnp.sin(li * step)
    den = lr * lr + li * li
    fr = ((ar - 1.0) * lr + ai * li) / den
    fi = (ai * lr - (ar - 1.0) * li) / den
    br = b_re.astype(F32)
    bi = b_im.astype(F32)
    bbr = fr[..., None] * br - fi[..., None] * bi
    bbi = fr[..., None] * bi + fi[..., None] * br
    jj = jnp.arange(t_len + 1, dtype=F32)[None, None, :]
    pmag = jnp.exp((lr * step)[..., None] * jj)
    pang = (li * step)[..., None] * jj
    pw_r = pmag * jnp.cos(pang)
    pw_i = pmag * jnp.sin(pang)
    cr_t = jnp.transpose(c_re.astype(F32), (0, 2, 1))[:, :, None, :]
    ci_t = jnp.transpose(c_im.astype(F32), (0, 2, 1))[:, :, None, :]

    def c_times_powers(first):
        qr = pw_r[:, :, first:first + t_len, None]
        qi = pw_i[:, :, first:first + t_len, None]
        return ((cr_t * qr - ci_t * qi).reshape(g_n, p_n, SSM_TC),
                (cr_t * qi + ci_t * qr).reshape(g_n, p_n, SSM_TC))

    bbr_t = jnp.transpose(bbr, (0, 2, 1))
    bbi_t = jnp.transpose(bbi, (0, 2, 1))
    k_r, k_i = c_times_powers(0)
    k_lhs = jnp.concatenate([bbr_t, bbi_t], axis=-1)
    k_rhs = jnp.concatenate([k_r, -k_i], axis=1)
    d_r, d_i = c_times_powers(1)
    rev_r = jnp.transpose(pw_r, (0, 2, 1))[:, t_len - 1::-1][:, :t_len, None, :]
    rev_i = jnp.transpose(pw_i, (0, 2, 1))[:, t_len - 1::-1][:, :t_len, None, :]
    e_r = (rev_r * bbr_t[:, None] - rev_i * bbi_t[:, None]).reshape(g_n, SSM_TC, p_n)
    e_i = (rev_r * bbi_t[:, None] + rev_i * bbr_t[:, None]).reshape(g_n, SSM_TC, p_n)
    z_in = jnp.zeros((SSM_PAIRS, SSM_TC, p_n), F32)
    e_r2 = e_r.reshape(SSM_PAIRS, 2, SSM_TC, p_n)
    e_i2 = e_i.reshape(SSM_PAIRS, 2, SSM_TC, p_n)
    min_top = jnp.concatenate([e_r2[:, 0], z_in, e_i2[:, 0], z_in], axis=-1)
    min_bot = jnp.concatenate([z_in, e_r2[:, 1], z_in, e_i2[:, 1]], axis=-1)
    m_in = jnp.concatenate([min_top, min_bot], axis=1)
    z_out = jnp.zeros((SSM_PAIRS, p_n, SSM_TC), F32)
    d_r2 = d_r.reshape(SSM_PAIRS, 2, p_n, SSM_TC)
    d_i2 = d_i.reshape(SSM_PAIRS, 2, p_n, SSM_TC)
    m_out = jnp.concatenate([
        jnp.concatenate([d_r2[:, 0], z_out], axis=-1),
        jnp.concatenate([z_out, d_r2[:, 1]], axis=-1),
        jnp.concatenate([-d_i2[:, 0], z_out], axis=-1),
        jnp.concatenate([z_out, -d_i2[:, 1]], axis=-1)], axis=1)
    at_r = pw_r[:, :, t_len].reshape(1, SSM_STATES)
    at_i = pw_i[:, :, t_len].reshape(1, SSM_STATES)
    return k_lhs, k_rhs, m_in.astype(BF16), m_out.astype(BF16), at_r, at_i


def _s5_toeplitz_kernel(a_ref, b_ref, o_ref):
    lane = lax.broadcasted_iota(jnp.int32, (SSM_GROUP, SSM_TC), 1)

    def body(g, c):
        kr = jnp.dot(a_ref[g], b_ref[g], preferred_element_type=F32,
                     precision=lax.Precision.HIGHEST)
        for s in range(SSM_T):
            shifted = pltpu.roll(kr, s * SSM_GROUP, axis=1) if s else kr
            o_ref[g, s * SSM_GROUP:(s + 1) * SSM_GROUP, :] = jnp.where(
                lane >= s * SSM_GROUP, shifted, 0.0).astype(o_ref.dtype)
        return c

    lax.fori_loop(0, SSM_GROUPS, body, 0)


def _s5_toeplitz(k_lhs, k_rhs):
    nl = k_lhs.shape[0]
    return pl.pallas_call(
        _s5_toeplitz_kernel,
        out_shape=jax.ShapeDtypeStruct((nl, SSM_GROUPS, SSM_TC, SSM_TC), BF16),
        grid=(nl,),
        in_specs=[pl.BlockSpec((None, SSM_GROUPS, SSM_GROUP, 2 * SSM_STATE), lambda l: (l, 0, 0, 0)),
                  pl.BlockSpec((None, SSM_GROUPS, 2 * SSM_STATE, SSM_TC), lambda l: (l, 0, 0, 0))],
        out_specs=pl.BlockSpec((None, SSM_GROUPS, SSM_TC, SSM_TC), lambda l: (l, 0, 0, 0)),
        compiler_params=_cparams(("parallel",)),
        name="s5_toeplitz",
    )(k_lhs, k_rhs)


def _granule_transpose(sets):
    xs = [list(src) for src in sets]
    grp = lax.broadcasted_iota(jnp.int32, xs[0][0].shape, 1) // SSM_GROUP
    stride = SSM_OCT // 2
    while stride:
        upper = (grp & stride) != 0
        for x in xs:
            for a in range(SSM_OCT):
                if a & stride:
                    continue
                lo, hi = x[a], x[a + stride]
                x[a] = jnp.where(upper, pltpu.roll(hi, stride * SSM_GROUP, axis=1), lo)
                x[a + stride] = jnp.where(upper, hi, pltpu.roll(lo, LANES - stride * SSM_GROUP, axis=1))
        stride //= 2
    return xs


S5_RELAYOUT_ROWS = 128
S5_RELAYOUT_UNROLL = 1


def _s5_x_kernel(uv_ref, min_ref, u_ref, xr_ref, xi_ref):
    rb = S5_RELAYOUT_ROWS

    def body(r, c):
        rows = pl.ds(pl.multiple_of(r * rb, rb), rb)
        outs = _granule_transpose(
            [[uv_ref[half * SSM_OCT + tp, rows, :].astype(F32) for tp in range(SSM_OCT)] for half in range(2)])
        for half in range(2):
            for g in range(SSM_OCT):
                u_ref[g, rows, half * LANES:(half + 1) * LANES] = outs[half][g].astype(BF16)
        return c

    lax.fori_loop(0, SSM_NC // rb, body, 0, unroll=S5_RELAYOUT_UNROLL)
    for pr in range(SSM_OCT // 2):
        up = jnp.concatenate([u_ref[2 * pr], u_ref[2 * pr + 1]], axis=1)
        x = jnp.dot(up, min_ref[pr], preferred_element_type=F32)
        xr_ref[:, pr * LANES:(pr + 1) * LANES] = x[:, :LANES]
        xi_ref[:, pr * LANES:(pr + 1) * LANES] = x[:, LANES:]


def _s5_x(uv, m_in, layer):
    nc = SSM_NC
    ppo = SSM_OCT // 2
    sw = ppo * LANES
    return pl.pallas_call(
        _s5_x_kernel,
        out_shape=[jax.ShapeDtypeStruct((SSM_GROUPS, nc, SSM_TC), BF16),
                   jax.ShapeDtypeStruct((nc, SSM_STATES), F32),
                   jax.ShapeDtypeStruct((nc, SSM_STATES), F32)],
        grid=(SSM_NOCT,),
        in_specs=[
            pl.BlockSpec((SSM_T, nc, LANES), lambda k: (0, 0, k)),
            pl.BlockSpec((None, ppo, 2 * SSM_TC, 256), lambda k: (layer, k, 0, 0)),
        ],
        out_specs=[pl.BlockSpec((SSM_OCT, nc, SSM_TC), lambda k: (k, 0, 0)),
                   pl.BlockSpec((nc, sw), lambda k: (0, k)),
                   pl.BlockSpec((nc, sw), lambda k: (0, k))],
        compiler_params=_cparams(("parallel",)),
        name="s5_x",
    )(uv, m_in)


def _s5_scan_kernel(xr_ref, xi_ref, ar_ref, ai_ref, hr_ref, hi_ref, *, nc):
    ar = ar_ref[...]
    ai = ai_ref[...]

    def body(n, carry):
        hr, hi = carry
        row = pl.ds(n, 1)
        hr_ref[row, :] = hr
        hi_ref[row, :] = hi
        xr = xr_ref[row, :]
        xi = xi_ref[row, :]
        return (ar * hr - ai * hi + xr, ar * hi + ai * hr + xi)

    zero = jnp.zeros_like(ar)
    lax.fori_loop(0, nc, body, (zero, zero), unroll=8)


def _s5_scan(xr, xi, at_r, at_i, layer, lb=1024):
    nc, ns = xr.shape
    blk = pl.BlockSpec((nc, lb), lambda i: (0, i))
    tab = pl.BlockSpec((None, 1, lb), lambda i: (layer, 0, i))
    return pl.pallas_call(
        functools.partial(_s5_scan_kernel, nc=nc),
        out_shape=[jax.ShapeDtypeStruct((nc, ns), F32)] * 2,
        grid=(ns // lb,),
        in_specs=[blk, blk, tab, tab],
        out_specs=[blk, blk],
        compiler_params=_cparams(("parallel",)),
        name="s5_scan",
    )(xr, xi, at_r, at_i)


def _s5_y_kernel(u_ref, hr_ref, hi_ref, toep_ref, mout_ref, yv_ref, y_scr):
    for pr in range(SSM_OCT // 2):
        hcat = jnp.concatenate([hr_ref[:, pr * LANES:(pr + 1) * LANES],
                                hi_ref[:, pr * LANES:(pr + 1) * LANES]], axis=1).astype(BF16)
        carry = jnp.dot(hcat, mout_ref[pr], preferred_element_type=F32)
        for k in range(2):
            g = 2 * pr + k
            y = jnp.dot(u_ref[g], toep_ref[g], preferred_element_type=F32)
            y_scr[g] = y + carry[:, k * SSM_TC:(k + 1) * SSM_TC]

    rb = S5_RELAYOUT_ROWS

    def body(r, c):
        rows = pl.ds(pl.multiple_of(r * rb, rb), rb)
        outs = _granule_transpose(
            [[y_scr[g, rows, half * LANES:(half + 1) * LANES] for g in range(SSM_OCT)] for half in range(2)])
        for half in range(2):
            for tp in range(SSM_OCT):
                yv_ref[half * SSM_OCT + tp, rows, :] = outs[half][tp].astype(BF16)
        return c

    lax.fori_loop(0, SSM_NC // rb, body, 0, unroll=S5_RELAYOUT_UNROLL)


def _s5_y(u_g, hr, hi, toep, m_out, layer):
    nc = SSM_NC
    ppo = SSM_OCT // 2
    hblk = pl.BlockSpec((nc, ppo * LANES), lambda k: (0, k))
    return pl.pallas_call(
        _s5_y_kernel,
        out_shape=jax.ShapeDtypeStruct((SSM_T, nc, SSM_WIDTH), BF16),
        grid=(SSM_NOCT,),
        in_specs=[
            pl.BlockSpec((SSM_OCT, nc, SSM_TC), lambda k: (k, 0, 0)),
            hblk, hblk,
            pl.BlockSpec((None, SSM_OCT, SSM_TC, SSM_TC), lambda k: (layer, k, 0, 0)),
            pl.BlockSpec((None, ppo, 256, 2 * SSM_TC), lambda k: (layer, k, 0, 0)),
        ],
        out_specs=pl.BlockSpec((SSM_T, nc, LANES), lambda k: (0, 0, k)),
        scratch_shapes=[pltpu.VMEM((SSM_OCT, nc, SSM_TC), F32)],
        compiler_params=_cparams(("parallel",)),
        name="s5_y",
    )(u_g, hr, hi, toep, m_out)


def _s5_glu_kernel(yv_ref, uv_ref, d_ref, w_ref, b_ref, o_ref, z_scr, zb_scr, g_scr, nat_scr, *, ncb):
    d = d_ref[...]
    for t in range(SSM_T):
        z = jax.nn.gelu(yv_ref[t].astype(F32) + d * uv_ref[t].astype(F32))
        z_scr[t * ncb:(t + 1) * ncb, :] = z
        zb_scr[t * ncb:(t + 1) * ncb, :] = z.astype(BF16)
    g_scr[...] = jnp.dot(zb_scr[...], w_ref[...], preferred_element_type=F32)
    b = b_ref[...]
    nt = SSM_WIDTH // LANES
    for t in range(SSM_T):
        rows = slice(t * ncb, (t + 1) * ncb)
        out = z_scr[rows, :] * jax.nn.sigmoid(g_scr[rows, :] + b)
        for k in range(nt):
            nat_scr[k, pl.ds(t, ncb, stride=SSM_T), :] = out[:, k * LANES:(k + 1) * LANES]
    for k in range(nt):
        o_ref[:, k * LANES:(k + 1) * LANES] = nat_scr[k].astype(o_ref.dtype)


def _s5_glu(yv, uv, d_skip, w_glu, b_glu, layer, ncb=64):
    w = SSM_WIDTH
    tm = ncb * SSM_T
    vblk = pl.BlockSpec((SSM_T, ncb, w), lambda i: (0, i, 0))
    vec = pl.BlockSpec((None, 1, w), lambda i: (layer, 0, 0))
    return pl.pallas_call(
        functools.partial(_s5_glu_kernel, ncb=ncb),
        out_shape=jax.ShapeDtypeStruct((SEQ, w), BF16),
        grid=(SSM_NC // ncb,),
        in_specs=[vblk, vblk, vec,
                  pl.BlockSpec((w, w), lambda i: (0, 0)),
                  vec],
        out_specs=pl.BlockSpec((tm, w), lambda i: (i, 0)),
        scratch_shapes=[pltpu.VMEM((tm, w), F32), pltpu.VMEM((tm, w), BF16), pltpu.VMEM((tm, w), F32),
                        pltpu.VMEM((w // LANES, tm, LANES), F32)],
        compiler_params=_cparams(("parallel",)),
        name="s5_glu",
    )(yv, uv, d_skip, w_glu, b_glu)


def _s5_branch(uv, prep, d_skip, w_glu, b_glu, layer):
    toep, m_in, m_out, at_r, at_i = prep
    u_g, xr, xi = _s5_x(uv, m_in, layer)
    hr, hi = _s5_scan(xr, xi, at_r, at_i, layer)
    yv = _s5_y(u_g, hr, hi, toep, m_out, layer)
    return _s5_glu(yv, uv, d_skip, w_glu, b_glu, layer)


def _swa_kernel(sink_ref, q_ref, k_ref, v_ref, cos_ref, sin_ref, o_ref, s_scr, p_scr, kpad_scr, vext_scr):
    n = pl.program_id(0)
    blk = ATT_BLOCK
    hpt = LANES // HEAD_DIM
    nvar = ATT_KV_HEADS * hpt
    tiles_per_kv = ATT_GROUP // hpt
    row = lax.broadcasted_iota(jnp.int32, (blk, blk), 0)
    col = lax.broadcasted_iota(jnp.int32, (blk, blk), 1)

    @pl.when(n == 0)
    def _():
        kpad_scr[...] = jnp.zeros_like(kpad_scr)
        vext_scr[:, :, :LANES] = jnp.zeros((nvar, 2 * blk, LANES), BF16)
        vext_scr[:, :, LANES:] = jnp.ones((nvar, 2 * blk, LANES), BF16)

    @pl.when(n > 0)
    def _():
        for i in range(nvar):
            kpad_scr[i, :blk, :] = kpad_scr[i, blk:, :]
            vext_scr[i, :blk, :LANES] = jnp.where(row == 0, jnp.zeros((), BF16), vext_scr[i, blk:, :LANES])

    lo = lax.broadcasted_iota(jnp.int32, (blk, LANES), 1) < HEAD_DIM

    def lane_padded(tile, half_in):
        other = pltpu.roll(tile, HEAD_DIM, axis=1)
        at_lo, at_hi = (tile, other) if half_in == 0 else (other, tile)
        return jnp.where(lo, at_lo, 0.0).astype(BF16), jnp.where(lo, 0.0, at_hi).astype(BF16)

    cos = cos_ref[...]
    sin = sin_ref[...]
    for kv_tile in range(ATT_KV_HEADS // hpt):
        lanes = slice(kv_tile * LANES, (kv_tile + 1) * LANES)
        kt = _rope(k_ref[:, lanes].astype(F32), cos, sin)
        vt = v_ref[:, lanes].astype(F32)
        for half in range(hpt):
            h = kv_tile * hpt + half
            k_pad = lane_padded(kt, half)
            v_pad = lane_padded(vt, half)
            for e in range(hpt):
                kpad_scr[h * hpt + e, blk:, :] = k_pad[e]
                vext_scr[h * hpt + e, blk:, :LANES] = v_pad[e]

    cur = col <= row
    sink_slot = (col == 0).astype(F32)
    prev_bias = jnp.where(n > 0, 0.0, NEG_BIG)

    for tile in range(ATT_HEADS // hpt):
        qt = _rope(q_ref[:, tile * LANES:(tile + 1) * LANES].astype(F32) * ATT_Q_SCALE, cos, sin).astype(BF16)
        for e in range(hpt):
            var = (tile // tiles_per_kv) * hpt + e
            s_scr[tile * hpt + e] = lax.dot_general(qt, kpad_scr[var], (((1,), (1,)), ((), ())),
                                                    preferred_element_type=F32)

    for hd in range(ATT_HEADS):
        sk = sink_ref[hd] * LOG2E
        s = jnp.where(cur, s_scr[hd, :, blk:], s_scr[hd, :, :blk] + prev_bias)
        m = jnp.maximum(jnp.max(s, axis=-1, keepdims=True), sk)
        p = jnp.exp2(s - m)
        p_scr[hd, :, :blk] = jnp.where(cur, jnp.exp2(sk - m) * sink_slot, p).astype(BF16)
        p_scr[hd, :, blk:] = jnp.where(cur, p, 0.0).astype(BF16)

    for tile in range(ATT_HEADS // hpt):
        acc = None
        for e in range(hpt):
            var = (tile // tiles_per_kv) * hpt + e
            oe = jnp.dot(p_scr[tile * hpt + e], vext_scr[var], preferred_element_type=F32)
            term = oe[:, :LANES] / oe[:, LANES:]
            acc = term if acc is None else acc + term
        o_ref[:, tile * LANES:(tile + 1) * LANES] = acc.astype(o_ref.dtype)


def _swa(proj, sinks, cos_t, sin_t):
    assert WINDOW == ATT_BLOCK
    s = proj.shape[0]
    blk = ATT_BLOCK
    nvar = ATT_KV_HEADS * (LANES // HEAD_DIM)
    tab = pl.BlockSpec((blk, LANES), lambda n: (n, 0))
    return pl.pallas_call(
        _swa_kernel,
        out_shape=jax.ShapeDtypeStruct((s, ATT_Q_WIDTH), BF16),
        grid=(s // blk,),
        in_specs=[
            pl.BlockSpec(memory_space=pltpu.SMEM),
            pl.BlockSpec((blk, ATT_Q_WIDTH), lambda n: (n, COL_AQ // ATT_Q_WIDTH)),
            pl.BlockSpec((blk, ATT_KV_WIDTH), lambda n: (n, COL_AK // ATT_KV_WIDTH)),
            pl.BlockSpec((blk, ATT_KV_WIDTH), lambda n: (n, COL_AV // ATT_KV_WIDTH)),
            tab, tab,
        ],
        out_specs=pl.BlockSpec((blk, ATT_Q_WIDTH), lambda n: (n, 0)),
        scratch_shapes=[pltpu.VMEM((ATT_HEADS, blk, 2 * blk), F32), pltpu.VMEM((ATT_HEADS, blk, 2 * blk), BF16),
                        pltpu.VMEM((nvar, 2 * blk, LANES), BF16), pltpu.VMEM((nvar, 2 * blk, 2 * LANES), BF16)],
        compiler_params=_cparams(("arbitrary",)),
        name="swa",
    )(sinks, proj, proj, proj, cos_t, sin_t)


def _log_sigmoid(x):
    return jnp.minimum(x, 0.0) - jnp.log(1.0 + jnp.exp(-jnp.abs(x)))


def _gla_kernel(q_ref, k_ref, v_ref, glr_ref, gout_ref, wg_ref, bg_ref, ng_ref, tri_ref, o_ref,
                st_ref, qin_scr, att_scr, upd_scr, sb_scr, *, rb):
    @pl.when(pl.program_id(0) == 0)
    def _():
        st_ref[...] = jnp.zeros_like(st_ref)

    ch = GLA_CHUNK
    ncb = rb // ch
    nh = GLA_HEADS
    logit = jnp.dot(glr_ref[...], wg_ref[...], preferred_element_type=F32) + bg_ref[...]
    log_a = _log_sigmoid(logit) * (1.0 / GLA_TAU)
    tri = tri_ref[...]
    a_hi = log_a.astype(BF16)
    a_lo = (log_a - a_hi.astype(F32)).astype(BF16)
    bcum = (jnp.dot(tri, a_hi, preferred_element_type=F32)
            + jnp.dot(tri, a_lo, preferred_element_type=F32))
    qin_scr[...] = (q_ref[...].astype(F32) * (GLA_DK ** -0.5) * jnp.exp(bcum)).astype(BF16)
    k = k_ref[...].astype(F32)
    k_in = (k * jnp.exp(-bcum)).astype(BF16)
    causal = (lax.broadcasted_iota(jnp.int32, (ch, ch), 1)
              <= lax.broadcasted_iota(jnp.int32, (ch, ch), 0))

    decays = []
    for c in range(ncb):
        rows = slice(c * ch, (c + 1) * ch)
        b_last = bcum[(c + 1) * ch - 1:(c + 1) * ch, :]
        k_dec = (k[rows] * jnp.exp(b_last - bcum[rows])).astype(BF16)
        decays.append(jnp.exp(b_last))
        for h in range(nh):
            ks = slice(h * GLA_DK, (h + 1) * GLA_DK)
            att = lax.dot_general(qin_scr[rows, ks], k_in[rows, ks], (((1,), (1,)), ((), ())),
                                  preferred_element_type=F32)
            att_scr[c * nh + h] = jnp.where(causal, att, 0.0).astype(BF16)
            upd_scr[c * nh + h] = lax.dot_general(v_ref[rows, h * GLA_DV:(h + 1) * GLA_DV], k_dec[:, ks],
                                                  (((0,), (0,)), ((), ())), preferred_element_type=F32)

    for h in range(nh):
        ks = slice(h * GLA_DK, (h + 1) * GLA_DK)
        st = st_ref[h]
        for c in range(ncb):
            sb_scr[c * nh + h] = st.astype(BF16)
            st = st * decays[c][:, ks] + upd_scr[c * nh + h]
        st_ref[h] = st

    for c in range(ncb):
        rows = slice(c * ch, (c + 1) * ch)
        for h in range(nh):
            ks = slice(h * GLA_DK, (h + 1) * GLA_DK)
            vs = slice(h * GLA_DV, (h + 1) * GLA_DV)
            o = jnp.dot(att_scr[c * nh + h], v_ref[rows, vs], preferred_element_type=F32)
            o = o + lax.dot_general(qin_scr[rows, ks], sb_scr[c * nh + h], (((1,), (1,)), ((), ())),
                                    preferred_element_type=F32)
            o = o * lax.rsqrt(jnp.mean(o * o, axis=-1, keepdims=True) + EPS)
            o = o * ng_ref[:, vs]
            go = gout_ref[rows, vs].astype(F32)
            o = o * (go * jax.nn.sigmoid(go))
            o_ref[rows, vs] = o.astype(o_ref.dtype)


def _gla(proj, w_gate, b_gate, norm_g, layer, rb=512):
    s = proj.shape[0]
    ch = GLA_CHUNK
    idx = np.arange(rb)
    tri = jnp.asarray((idx[None, :] <= idx[:, None]) & (idx[None, :] // ch == idx[:, None] // ch), BF16)
    nblk = (rb // ch) * GLA_HEADS
    return pl.pallas_call(
        functools.partial(_gla_kernel, rb=rb),
        out_shape=jax.ShapeDtypeStruct((s, GLA_V_WIDTH), BF16),
        grid=(s // rb,),
        in_specs=[
            pl.BlockSpec((rb, GLA_QK_WIDTH), lambda i: (i, COL_GQ // GLA_QK_WIDTH)),
            pl.BlockSpec((rb, GLA_QK_WIDTH), lambda i: (i, COL_GK // GLA_QK_WIDTH)),
            pl.BlockSpec((rb, GLA_V_WIDTH), lambda i: (i, COL_GV // GLA_V_WIDTH)),
            pl.BlockSpec((rb, LANES), lambda i: (i, COL_GLR // LANES)),
            pl.BlockSpec((rb, GLA_V_WIDTH), lambda i: (i, COL_GOUT // GLA_V_WIDTH)),
            pl.BlockSpec((None, LANES, GLA_QK_WIDTH), lambda i: (layer, 0, 0)),
            pl.BlockSpec((None, 1, GLA_QK_WIDTH), lambda i: (layer, 0, 0)),
            pl.BlockSpec((None, 1, GLA_V_WIDTH), lambda i: (layer, 0, 0)),
            pl.BlockSpec((rb, rb), lambda i: (0, 0)),
        ],
        out_specs=pl.BlockSpec((rb, GLA_V_WIDTH), lambda i: (i, 0)),
        scratch_shapes=[pltpu.VMEM((GLA_HEADS, GLA_DV, GLA_DK), F32),
                        pltpu.VMEM((rb, GLA_QK_WIDTH), BF16),
                        pltpu.VMEM((nblk, ch, ch), BF16),
                        pltpu.VMEM((nblk, GLA_DV, GLA_DK), F32),
                        pltpu.VMEM((nblk, GLA_DV, GLA_DK), BF16)],
        compiler_params=_cparams(("arbitrary",)),
        name="gla",
    )(proj, proj, proj, proj, proj, w_gate, b_gate, norm_g, tri)


MIX_TN = 512


def _mix_kernel(x_ref, ys_ref, ya_ref, yg_ref, gate_ref, ws_ref, wa_ref, wg_ref, wo_ref, w1f_ref, w2f_ref,
                o_ref, w1b_ref, w2b_ref, m_scr):
    d = D_MODEL
    w1b_ref[...] = w1f_ref[...].astype(BF16)
    w2b_ref[...] = w2f_ref[...].astype(BF16)
    for j in range(d // MIX_TN):
        cols = slice(j * MIX_TN, (j + 1) * MIX_TN)
        m = None
        for k, (y_ref, w_ref) in enumerate(((ys_ref, ws_ref), (ya_ref, wa_ref), (yg_ref, wg_ref))):
            proj = jnp.dot(y_ref[...], w_ref[:, cols], preferred_element_type=F32)
            gate = gate_ref[:, k * d + j * MIX_TN:k * d + (j + 1) * MIX_TN].astype(F32)
            term = jax.nn.sigmoid(gate) * proj
            m = term if m is None else m + term
        m_scr[:, cols] = m.astype(BF16)
    o_ref[...] = x_ref[...] + jnp.dot(m_scr[...], wo_ref[...], preferred_element_type=F32)


def _mix(x, proj, y_ssm, y_att, y_gla, w_bs, w_ba, w_bg, w_out, w_ff1, w_ff2, layer, tm=256):
    s, d = x.shape
    bw = y_ssm.shape[1]
    dff = w_ff1.shape[2]
    steps = s // tm
    r1, r2 = d // steps, dff // steps
    yblk = pl.BlockSpec((tm, bw), lambda i: (i, 0))
    wblk = pl.BlockSpec((bw, d), lambda i: (0, 0), pipeline_mode=pl.Buffered(1))
    return pl.pallas_call(
        _mix_kernel,
        out_shape=[jax.ShapeDtypeStruct((s, d), F32),
                   jax.ShapeDtypeStruct((d, dff), BF16),
                   jax.ShapeDtypeStruct((dff, d), BF16)],
        grid=(steps,),
        in_specs=[
            pl.BlockSpec((tm, d), lambda i: (i, 0)),
            yblk, yblk, yblk,
            pl.BlockSpec((tm, 3 * d), lambda i: (i, COL_MERGE // (3 * d))),
            wblk, wblk, wblk,
            pl.BlockSpec((d, d), lambda i: (0, 0), pipeline_mode=pl.Buffered(1)),
            pl.BlockSpec((None, r1, dff), lambda i: (layer, i, 0)),
            pl.BlockSpec((None, r2, d), lambda i: (layer, i, 0)),
        ],
        out_specs=[pl.BlockSpec((tm, d), lambda i: (i, 0)),
                   pl.BlockSpec((r1, dff), lambda i: (i, 0)),
                   pl.BlockSpec((r2, d), lambda i: (i, 0))],
        scratch_shapes=[pltpu.VMEM((tm, d), BF16)],
        compiler_params=_cparams(("parallel",)),
        name="mix",
    )(x, y_ssm, y_att, y_gla, proj, w_bs, w_ba, w_bg, w_out, w_ff1, w_ff2)


FFN_SLABS = 2


def _ffn_kernel(x_ref, g_ref, w1_ref, w2_ref, fg_ref, o_ref, h_ref, *, tm, nf, final_norm):
    f = pl.program_id(1)

    @pl.when(f == 0)
    def _():
        _rms_rows(x_ref, g_ref, h_ref, tm)
        o_ref[...] = x_ref[...]

    slab = tm // FFN_SLABS
    for r in range(FFN_SLABS):
        rows = slice(r * slab, (r + 1) * slab)
        a = jnp.dot(h_ref[rows, :], w1_ref[...], preferred_element_type=F32)
        a = jnp.square(jnp.maximum(a, 0.0)).astype(BF16)
        o_ref[rows, :] += jnp.dot(a, w2_ref[...], preferred_element_type=F32)

    if final_norm:
        @pl.when(f == nf - 1)
        def _():
            _rms_rows(o_ref, fg_ref, o_ref, tm)


def _ffn(x, g, w1, w2, fg, layer, final_norm, tm=1024, tf=512):
    s, d = x.shape
    dff = w1.shape[1]
    nf = dff // tf
    return pl.pallas_call(
        functools.partial(_ffn_kernel, tm=tm, nf=nf, final_norm=final_norm),
        out_shape=jax.ShapeDtypeStruct((s, d), F32),
        grid=(s // tm, nf),
        in_specs=[
            pl.BlockSpec((tm, d), lambda i, f: (i, 0)),
            pl.BlockSpec((None, 1, d), lambda i, f: (layer, 0, 0)),
            pl.BlockSpec((d, tf), lambda i, f: (0, f)),
            pl.BlockSpec((tf, d), lambda i, f: (f, 0)),
            pl.BlockSpec((1, d), lambda i, f: (0, 0)),
        ],
        out_specs=pl.BlockSpec((tm, d), lambda i, f: (i, 0)),
        scratch_shapes=[pltpu.VMEM((tm, d), BF16)],
        compiler_params=_cparams(("parallel", "arbitrary")),
        name="ffn",
    )(x, g, w1, w2, fg)


PACK_TN = 512
PACK_ALIGN = 16
PACK_COPY, PACK_GLR = 0, 1


def _pack_tables():
    sizes = [SSM_WIDTH, ATT_Q_WIDTH, ATT_KV_WIDTH, ATT_KV_WIDTH, GLA_QK_WIDTH, GLA_QK_WIDTH,
             GLA_V_WIDTH, GLA_GATE_RANK, GLA_V_WIDTH, 3 * D_MODEL]
    o_u, o_aq, o_ak, o_av, o_gq, o_gk, o_gv, o_glr, o_gout, o_merge = np.concatenate([[0], np.cumsum(sizes)])[:-1]
    assert o_av == o_ak + ATT_KV_WIDTH
    pieces = [(o_u, SSM_WIDTH), (o_merge, 3 * D_MODEL), (o_aq, ATT_Q_WIDTH), (o_gv, GLA_V_WIDTH),
              (o_gout, GLA_V_WIDTH), (o_gq, GLA_QK_WIDTH), (o_gk, GLA_QK_WIDTH), (o_ak, 2 * ATT_KV_WIDTH)]
    src, mode = [], []
    for start, width in pieces:
        assert start % PACK_ALIGN == 0 and width % PACK_TN == 0
        for k in range(width // PACK_TN):
            src.append((start + k * PACK_TN) // PACK_ALIGN)
            mode.append(PACK_COPY)
    assert o_glr % PACK_ALIGN == 0 and o_glr + PACK_TN <= sum(sizes)
    src.append(o_glr // PACK_ALIGN)
    mode.append(PACK_GLR)
    assert len(mode) * PACK_TN == INPROJ_TN + PROJ_WIDTH
    return [jnp.asarray(np.array(t, np.int32)) for t in (src, mode)]


def _pack_kernel(src, mode, a_ref, o_ref, *, chunk=128):
    md = mode[pl.program_id(1)]

    def for_chunks(fn):
        def body(r, c):
            sl = pl.ds(pl.multiple_of(r * chunk, chunk), chunk)
            o_ref[sl, :] = fn(r, a_ref[0, sl, :]).astype(o_ref.dtype)
            return c
        lax.fori_loop(0, PACK_TN // chunk, body, 0)

    @pl.when(md == PACK_COPY)
    def _():
        for_chunks(lambda r, a: a)

    @pl.when(md == PACK_GLR)
    def _():
        row = lax.broadcasted_iota(jnp.int32, (chunk, a_ref.shape[2]), 0)
        for_chunks(lambda r, a: jnp.where(row + r * chunk < GLA_GATE_RANK, a, 0.0))


def _pack_w_in(w):
    nl, d, _ = w.shape
    wt = jnp.swapaxes(w, 1, 2)
    src, mode = _pack_tables()
    nt = (INPROJ_TN + PROJ_WIDTH) // PACK_TN
    return pl.pallas_call(
        _pack_kernel,
        out_shape=jax.ShapeDtypeStruct((nl, nt * PACK_TN, d), BF16),
        grid_spec=pltpu.PrefetchScalarGridSpec(
            num_scalar_prefetch=2,
            grid=(nl, nt),
            in_specs=[
                pl.BlockSpec((pl.Element(1), pl.Element(PACK_TN), pl.Element(d)),
                             lambda l, j, s, m: (l, pl.multiple_of(s[j] * PACK_ALIGN, PACK_ALIGN), 0)),
            ],
            out_specs=pl.BlockSpec((None, PACK_TN, d), lambda l, j, s, m: (l, j, 0)),
        ),
        compiler_params=_cparams(("parallel", "parallel")),
        name="pack_w_in",
    )(src, mode, wt)


def _rope_tables():
    half = HEAD_DIM // 2
    inv = np.float32(ROPE_THETA) ** (-np.arange(half, dtype=np.float32) / np.float32(half))
    ang = (np.arange(SEQ, dtype=np.float32)[:, None] * inv[None, :].astype(np.float32)).astype(np.float32)
    cos = np.cos(ang.astype(np.float64)).astype(np.float32)
    sin = np.sin(ang.astype(np.float64)).astype(np.float32)
    cos_t = np.concatenate([cos, cos, cos, cos], axis=1)
    sin_t = np.concatenate([-sin, sin, -sin, sin], axis=1)
    return jnp.asarray(cos_t), jnp.asarray(sin_t)


def kernel(x, norm1_g, w_in, ssm_lam_re, ssm_lam_im, ssm_log_step, ssm_b_re, ssm_b_im, ssm_c_re, ssm_c_im, ssm_d, ssm_w_glu, ssm_b_glu, att_sinks, gla_w_gate, gla_b_gate, gla_norm_g, w_branch_ssm, w_branch_att, w_branch_gla, w_out, norm2_g, w_ff1, w_ff2, final_norm_g):
    assert x.shape == (1, SEQ, D_MODEL)
    nl = DEPTH
    xs = x.reshape(SEQ, D_MODEL).astype(F32)
    cos_t, sin_t = _rope_tables()
    fg = final_norm_g.reshape(1, D_MODEL).astype(F32)
    w_in_p = _pack_w_in(w_in)
    n1 = norm1_g.reshape(nl, 1, D_MODEL).astype(F32)
    n2 = norm2_g.reshape(nl, 1, D_MODEL).astype(F32)
    k_lhs, k_rhs, m_in, m_out, at_r, at_i = jax.vmap(_s5_prepare)(
        ssm_lam_re, ssm_lam_im, ssm_log_step, ssm_b_re, ssm_b_im, ssm_c_re, ssm_c_im)
    prep = (_s5_toeplitz(k_lhs, k_rhs), m_in, m_out, at_r, at_i)
    d_skip = ssm_d.reshape(nl, 1, SSM_WIDTH).astype(F32)
    b_glu = ssm_b_glu.reshape(nl, 1, SSM_WIDTH).astype(F32)
    sinks = att_sinks.astype(F32)
    wg_pad = jnp.concatenate(
        [gla_w_gate.astype(BF16), jnp.zeros((nl, LANES - GLA_GATE_RANK, GLA_QK_WIDTH), BF16)], axis=1)
    bg = gla_b_gate.reshape(nl, 1, GLA_QK_WIDTH).astype(F32)
    ng = gla_norm_g.reshape(nl, 1, GLA_V_WIDTH).astype(F32)
    small_f32 = [w.astype(F32) for w in (ssm_w_glu, w_branch_ssm, w_branch_att, w_branch_gla, w_out)]
    w_ff1 = w_ff1.astype(F32)
    w_ff2 = w_ff2.astype(F32)
    for l in range(nl):
        proj, uv, (w_glu, w_bs, w_ba, w_bg, w_o) = _inproj(xs, n1, w_in_p, small_f32, l)
        y_ssm = _s5_branch(uv, prep, d_skip, w_glu, b_glu, l)
        y_att = _swa(proj, sinks[l], cos_t, sin_t)
        y_gla = _gla(proj, wg_pad, bg, ng, l)
        xs, w1, w2 = _mix(xs, proj, y_ssm, y_att, y_gla, w_bs, w_ba, w_bg, w_o, w_ff1, w_ff2, l)
        xs = _ffn(xs, n2, w1, w2, fg, l, final_norm=(l == nl - 1))
    return xs.reshape(1, SEQ, D_MODEL)
```
